```python
import jax, jax.numpy as jnp
from jax import lax
import numpy as np

D_MODEL = 2048
BATCH = 4
SEQ = 2048
DEPTH = 2

GRID_W = 64
CTX_LEN = 256
EPS = 1e-6
ROPE_BASE = 10000.0
Q_BLOCK = 128

ML_HEADS = 4
ML_HEAD_DIM = 256
ML_WIDTH = ML_HEADS * ML_HEAD_DIM
ML_CHUNK = 64
ML_GATES = 2 * 2 * ML_HEADS
ML_FORGET_BIAS = 3.0

LA_HEADS = 8
LA_NOPE = 128
LA_ROPE = 64
LA_VDIM = 128
LA_Q_RANK = 512
LA_KV_RANK = 256
LA_WIDTH = LA_HEADS * LA_VDIM

GA_HEADS = 8
GA_KV_HEADS = 2
GA_HEAD_DIM = 128
GA_WIDTH = GA_HEADS * GA_HEAD_DIM
GA_KV_WIDTH = GA_KV_HEADS * GA_HEAD_DIM

N_BRANCH = 3

N_EXPERTS = 16
EXPERT_FF = 1024
CAPACITY_FACTOR = 2

PROJ_LAYOUT = (('ml_q', ML_WIDTH), ('ml_k', ML_WIDTH), ('ml_v', ML_WIDTH), ('ml_o', ML_WIDTH), ('ml_g', ML_GATES),
               ('la_cq', LA_Q_RANK), ('la_ckv', LA_KV_RANK), ('la_kr', LA_ROPE),
               ('ga_q', GA_WIDTH), ('ga_k', GA_KV_WIDTH), ('ga_v', GA_KV_WIDTH),
               ('gate', N_BRANCH * D_MODEL))
PROJ_WIDTH = (4 * ML_WIDTH + ML_GATES + LA_Q_RANK + LA_KV_RANK + LA_ROPE
              + GA_WIDTH + 2 * GA_KV_WIDTH + N_BRANCH * D_MODEL)

kernel_name = 'hybrid_mlstm_mla_gqa_ecmoe_dit'


def rms_norm(x, gain=None):
    xf = x.astype(jnp.float32)
    y = xf * lax.rsqrt(jnp.mean(xf * xf, axis=-1, keepdims=True) + EPS)
    if gain is not None:
        y = y * gain.astype(jnp.float32)
    return y.astype(x.dtype)


def modulate(x, shift, scale):
    return rms_norm(x) * (1 + scale) + shift


def to_heads(t, n):
    b, n_tok, w = t.shape
    return t.reshape(b, n_tok, n, w // n).transpose(0, 2, 1, 3)


def merge_heads(t):
    b, n, n_tok, d = t.shape
    return t.transpose(0, 2, 1, 3).reshape(b, n_tok, n * d)


def split_columns(p):
    out = {}
    off = 0
    for name, size in PROJ_LAYOUT:
        out[name] = p[..., off:off + size]
        off += size
    return out


def axial_rope_tables(n_tok, dim):
    rows = n_tok // GRID_W
    quarter = dim // 4
    inv = ROPE_BASE ** (-jnp.arange(quarter, dtype=jnp.float32) / quarter)
    row = jnp.broadcast_to(jnp.arange(rows, dtype=jnp.float32)[:, None], (rows, GRID_W)).reshape(-1)
    col = jnp.broadcast_to(jnp.arange(GRID_W, dtype=jnp.float32)[None, :], (rows, GRID_W)).reshape(-1)
    ang = jnp.concatenate([row[:, None] * inv[None, :], col[:, None] * inv[None, :]], axis=-1)
    return jnp.cos(ang), jnp.sin(ang)


def apply_axial_rope(x, cos, sin):
    q4 = x.shape[-1] // 4
    xr = x.astype(jnp.float32).reshape(*x.shape[:-1], 2, 2, q4)
    a, b = xr[..., 0, :], xr[..., 1, :]
    c = cos.reshape(-1, 2, q4)
    s = sin.reshape(-1, 2, q4)
    out = jnp.stack([a * c - b * s, a * s + b * c], axis=-2)
    return out.reshape(x.shape).astype(x.dtype)


def block_attention(q, k, v):
    b, kh, g, n_tok, dk = q.shape
    nb = n_tok // Q_BLOCK
    qb = jnp.moveaxis(q.reshape(b, kh, g, nb, Q_BLOCK, dk), 3, 0)

    def one_block(q_blk):
        s = jnp.einsum('bkgqd,bktd->bkgqt', q_blk, k).astype(jnp.float32)
        p = jax.nn.softmax(s, axis=-1).astype(v.dtype)
        return jnp.einsum('bkgqt,bktd->bkgqd', p, v)

    o = lax.map(one_block, qb)
    return jnp.moveaxis(o, 0, 3).reshape(b, kh, g, n_tok, -1)


def grouped_attention(q, k, v):
    b, h, n_tok, dk = q.shape
    kh = k.shape[1]
    qg = (q * dk ** -0.5).reshape(b, kh, h // kh, n_tok, dk)
    o = block_attention(qg, k, v)
    return merge_heads(o.reshape(b, h, n_tok, -1))


def mlstm_chunkwise(q, k, v, i_pre, f_pre, state):
    b, h, n_tok, dk = q.shape
    nc = n_tok // ML_CHUNK
    q = q * dk ** -0.5
    log_f = jax.nn.log_sigmoid(f_pre)
    log_i = i_pre

    def to_chunks(a):
        return jnp.moveaxis(a.reshape(b, h, nc, ML_CHUNK, *a.shape[3:]), 2, 0)

    tri = jnp.tril(jnp.ones((ML_CHUNK, ML_CHUNK), dtype=bool))

    def step(carry, inp):
        c_mat, n_vec, m = carry
        qc, kc, vc, li, lf = inp
        cum = jnp.cumsum(lf, axis=-1)
        dmat = jnp.where(tri, cum[..., :, None] - cum[..., None, :] + li[..., None, :], -jnp.inf)
        inter = cum + m[..., None]
        m_row = jnp.maximum(inter, jnp.max(dmat, axis=-1))
        w = jnp.exp(dmat - m_row[..., None])
        s = jnp.einsum('bhld,bhsd->bhls', qc, kc) * w
        w_inter = jnp.exp(inter - m_row)
        num = jnp.einsum('bhls,bhse->bhle', s, vc) + w_inter[..., None] * jnp.einsum('bhed,bhld->bhle', c_mat, qc)
        den = jnp.sum(s, axis=-1) + w_inter * jnp.einsum('bhd,bhld->bhl', n_vec, qc)
        h_out = num / jnp.maximum(jnp.abs(den), jnp.exp(-m_row))[..., None]
        g = cum[..., -1:] - cum + li
        m_new = jnp.maximum(cum[..., -1] + m, jnp.max(g, axis=-1))
        wk = jnp.exp(g - m_new[..., None])
        decay = jnp.exp(cum[..., -1] + m - m_new)
        c_new = decay[..., None, None] * c_mat + jnp.einsum('bhse,bhsd->bhed', wk[..., None] * vc, kc)
        n_new = decay[..., None] * n_vec + jnp.einsum('bhs,bhsd->bhd', wk, kc)
        return (c_new, n_new, m_new), h_out

    state, hs = lax.scan(step, state, tuple(to_chunks(a) for a in (q, k, v, log_i, log_f)))
    return jnp.moveaxis(hs, 0, 2).reshape(b, h, n_tok, -1), state


def mlstm_branch(p_ctx, p_lat, gate_bias, norm_gain):
    def prep(p):
        bsz, n_tok, _ = p['ml_q'].shape
        q = to_heads(p['ml_q'], ML_HEADS).astype(jnp.float32)
        k = to_heads(p['ml_k'], ML_HEADS).astype(jnp.float32)
        v = to_heads(p['ml_v'], ML_HEADS).astype(jnp.float32)
        g = (p['ml_g'] + gate_bias).astype(jnp.float32).reshape(bsz, n_tok, 2, 2, ML_HEADS)
        return q, k, v, jnp.transpose(g, (2, 3, 0, 4, 1))

    q_c, k_c, v_c, g_c = prep(p_ctx)
    q_l, k_l, v_l, g_l = prep(p_lat)
    bsz = q_c.shape[0]
    h_c = jnp.zeros_like(v_c)
    h_l = jnp.zeros_like(v_l)
    for d in range(2):
        rev = (lambda t: jnp.flip(t, axis=2)) if d == 1 else (lambda t: t)
        state0 = (jnp.zeros((bsz, ML_HEADS, ML_HEAD_DIM, ML_HEAD_DIM), jnp.float32),
                  jnp.zeros((bsz, ML_HEADS, ML_HEAD_DIM), jnp.float32),
                  jnp.zeros((bsz, ML_HEADS), jnp.float32))
        o_c, state_c = mlstm_chunkwise(rev(q_c), rev(k_c), rev(v_c), rev(g_c[d, 0]), rev(g_c[d, 1]), state0)
        o_l, _ = mlstm_chunkwise(rev(q_l), rev(k_l), rev(v_l), rev(g_l[d, 0]), rev(g_l[d, 1]), state_c)
        h_c = h_c + rev(o_c)
        h_l = h_l + rev(o_l)

    def finish(h, p):
        h = rms_norm(h, norm_gain.reshape(ML_HEADS, 1, ML_HEAD_DIM))
        return (merge_heads(h) * jax.nn.sigmoid(p['ml_o'].astype(jnp.float32))).astype(p['ml_o'].dtype)

    return finish(h_c, p_ctx), finish(h_l, p_lat)


def mla_q(p, q_gain, w_uq, rope):
    q = to_heads(rms_norm(p['la_cq'], q_gain) @ w_uq, LA_HEADS)
    if rope is not None:
        q = jnp.concatenate([q[..., :LA_NOPE], apply_axial_rope(q[..., LA_NOPE:], *rope)], axis=-1)
    return q


def mla_kv(p, kv_gain, w_ukv, rope):
    kv = to_heads(rms_norm(p['la_ckv'], kv_gain) @ w_ukv, LA_HEADS)
    k_nope, v = kv[..., :LA_NOPE], kv[..., LA_NOPE:]
    k_rope = p['la_kr'][:, None]
    if rope is not None:
        k_rope = apply_axial_rope(k_rope, *rope)
    k = jnp.concatenate([k_nope, jnp.broadcast_to(k_rope, k_nope.shape[:-1] + (LA_ROPE,))], axis=-1)
    return k, v


def gqa_q(p, gain, rope):
    q = rms_norm(to_heads(p['ga_q'], GA_HEADS), gain)
    return q if rope is None else apply_axial_rope(q, *rope)


def gqa_kv(p, gain, rope):
    k = rms_norm(to_heads(p['ga_k'], GA_KV_HEADS), gain)
    if rope is not None:
        k = apply_axial_rope(k, *rope)
    return k, to_heads(p['ga_v'], GA_KV_HEADS)


def merge_branches(gate_pre, y_ml, y_la, y_ga, w_br_ml, w_br_la, w_br_ga, w_out):
    g = jax.nn.sigmoid(gate_pre.astype(jnp.float32)).astype(y_la.dtype)
    g = g.reshape(*g.shape[:-1], N_BRANCH, D_MODEL)
    merged = (g[..., 0, :] * (y_ml @ w_br_ml) + g[..., 1, :] * (y_la @ w_br_la)
              + g[..., 2, :] * (y_ga @ w_br_ga))
    return merged @ w_out


def token_mixer(h_lat, h_ctx, w_in, ml_gate_bias, ml_norm, la_q_norm, la_kv_norm, la_w_uq, la_w_ukv,
                ga_q_norm, ga_k_norm, w_br_ml, w_br_la, w_br_ga, w_out, rope_la, rope_ga, need_ctx):
    p_lat = split_columns(h_lat @ w_in)
    p_ctx = split_columns(h_ctx @ w_in)
    y_ml_c, y_ml_l = mlstm_branch(p_ctx, p_lat, ml_gate_bias, ml_norm)
    kla_c, vla_c = mla_kv(p_ctx, la_kv_norm, la_w_ukv, None)
    kla_l, vla_l = mla_kv(p_lat, la_kv_norm, la_w_ukv, rope_la)
    y_la_l = grouped_attention(mla_q(p_lat, la_q_norm, la_w_uq, rope_la),
                               jnp.concatenate([kla_c, kla_l], axis=2), jnp.concatenate([vla_c, vla_l], axis=2))
    kga_c, vga_c = gqa_kv(p_ctx, ga_k_norm, None)
    kga_l, vga_l = gqa_kv(p_lat, ga_k_norm, rope_ga)
    y_ga_l = grouped_attention(gqa_q(p_lat, ga_q_norm, rope_ga),
                               jnp.concatenate([kga_c, kga_l], axis=2), jnp.concatenate([vga_c, vga_l], axis=2))
    y_lat = merge_branches(p_lat['gate'], y_ml_l, y_la_l, y_ga_l, w_br_ml, w_br_la, w_br_ga, w_out)
    if not need_ctx:
        return y_lat, None
    y_la_c = grouped_attention(mla_q(p_ctx, la_q_norm, la_w_uq, None), kla_c, vla_c)
    y_ga_c = grouped_attention(gqa_q(p_ctx, ga_q_norm, None), kga_c, vga_c)
    y_ctx = merge_branches(p_ctx['gate'], y_ml_c, y_la_c, y_ga_c, w_br_ml, w_br_la, w_br_ga, w_out)
    return y_lat, y_ctx


def expert_choice_ffn(h, w_router, w_e1, w_e3, w_e2):
    n_tok, d = h.shape[1], h.shape[2]
    cap = CAPACITY_FACTOR * n_tok // N_EXPERTS
    aff = jax.nn.softmax((h @ w_router).astype(jnp.float32), axis=-1)
    vals, idx = lax.top_k(jnp.swapaxes(aff, 1, 2), cap)
    xg = jax.vmap(lambda hb, ib: hb[ib])(h, idx)
    a = jnp.einsum('becd,edf->becf', xg, w_e1)
    u = jnp.einsum('becd,edf->becf', xg, w_e3)
    y = jnp.einsum('becf,efd->becd', jax.nn.silu(a) * u, w_e2) * vals[..., None].astype(h.dtype)
    return jax.vmap(lambda ib, yb: jnp.zeros((n_tok, d), h.dtype).at[ib].add(yb))(idx, y)


def setup_inputs(seed: int = 0) -> dict:
    key = jax.random.key(seed)
    ks = jax.random.split(key, 28)
    f32 = jnp.float32

    def nrm(k, shape, fan_in, s=1.0):
        return jax.random.normal(k, shape, f32) * (s * fan_in ** -0.5)

    def gain(k, shape):
        return 1.0 + 0.05 * jax.random.normal(k, shape, f32)

    gate_base = jnp.array([0.0, ML_FORGET_BIAS], f32)[None, None, :, None]
    ml_gate_bias = (gate_base + 0.1 * jax.random.normal(ks[7], (DEPTH, 2, 2, ML_HEADS), f32)).reshape(DEPTH, ML_GATES)
    return {
        'x': jax.random.normal(ks[0], (BATCH, SEQ, D_MODEL), f32),
        'c': jax.random.normal(ks[1], (BATCH, D_MODEL), f32),
        'ctx': jax.random.normal(ks[2], (BATCH, CTX_LEN, D_MODEL), f32),
        'c_ctx': jax.random.normal(ks[3], (D_MODEL,), f32),
        'w_mod': nrm(ks[4], (DEPTH, D_MODEL, 6 * D_MODEL), D_MODEL, 0.5),
        'b_mod': 0.02 * jax.random.normal(ks[5], (DEPTH, 6 * D_MODEL), f32),
        'w_in': nrm(ks[6], (DEPTH, D_MODEL, PROJ_WIDTH), D_MODEL),
        'ml_gate_bias': ml_gate_bias,
        'ml_norm': gain(ks[8], (DEPTH, ML_WIDTH)),
        'la_q_norm': gain(ks[9], (DEPTH, LA_Q_RANK)),
        'la_kv_norm': gain(ks[10], (DEPTH, LA_KV_RANK)),
        'la_w_uq': nrm(ks[11], (DEPTH, LA_Q_RANK, LA_HEADS * (LA_NOPE + LA_ROPE)), LA_Q_RANK),
        'la_w_ukv': nrm(ks[12], (DEPTH, LA_KV_RANK, LA_HEADS * (LA_NOPE + LA_VDIM)), LA_KV_RANK),
        'ga_q_norm': gain(ks[13], (DEPTH, GA_HEAD_DIM)),
        'ga_k_norm': gain(ks[14], (DEPTH, GA_HEAD_DIM)),
        'w_br_ml': nrm(ks[15], (DEPTH, ML_WIDTH, D_MODEL), ML_WIDTH),
        'w_br_la': nrm(ks[16], (DEPTH, LA_WIDTH, D_MODEL), LA_WIDTH),
        'w_br_ga': nrm(ks[17], (DEPTH, GA_WIDTH, D_MODEL), GA_WIDTH),
        'w_out': nrm(ks[18], (DEPTH, D_MODEL, D_MODEL), D_MODEL),
        'w_router': nrm(ks[19], (DEPTH, D_MODEL, N_EXPERTS), D_MODEL),
        'w_e1': nrm(ks[20], (DEPTH, N_EXPERTS, D_MODEL, EXPERT_FF), D_MODEL),
        'w_e3': nrm(ks[21], (DEPTH, N_EXPERTS, D_MODEL, EXPERT_FF), D_MODEL),
        'w_e2': nrm(ks[22], (DEPTH, N_EXPERTS, EXPERT_FF, D_MODEL), EXPERT_FF),
        'final_norm': gain(ks[23], (D_MODEL,)),
    }


def reference(x, c, ctx, c_ctx, w_mod, b_mod, w_in, ml_gate_bias, ml_norm, la_q_norm, la_kv_norm, la_w_uq,
              la_w_ukv, ga_q_norm, ga_k_norm, w_br_ml, w_br_la, w_br_ga, w_out, w_router, w_e1, w_e3, w_e2,
              final_norm):
    n_lat = x.shape[1]
    rope_la = axial_rope_tables(n_lat, LA_ROPE)
    rope_ga = axial_rope_tables(n_lat, GA_HEAD_DIM)
    x_lat, x_ctx = x, ctx
    for l in range(DEPTH):
        need_ctx = l < DEPTH - 1
        mods_l = (jax.nn.silu(c) @ w_mod[l] + b_mod[l])[:, None, :]
        mods_c = jax.nn.silu(c_ctx) @ w_mod[l] + b_mod[l]
        sh1_l, sc1_l, g1_l, sh2_l, sc2_l, g2_l = jnp.split(mods_l, 6, axis=-1)
        sh1_c, sc1_c, g1_c, sh2_c, sc2_c, g2_c = jnp.split(mods_c, 6, axis=-1)
        y_lat, y_ctx = token_mixer(modulate(x_lat, sh1_l, sc1_l), modulate(x_ctx, sh1_c, sc1_c), w_in[l],
                                   ml_gate_bias[l], ml_norm[l], la_q_norm[l], la_kv_norm[l], la_w_uq[l],
                                   la_w_ukv[l], ga_q_norm[l], ga_k_norm[l], w_br_ml[l], w_br_la[l], w_br_ga[l],
                                   w_out[l], rope_la, rope_ga, need_ctx)
        x_lat = x_lat + g1_l * y_lat
        x_lat = x_lat + g2_l * expert_choice_ffn(modulate(x_lat, sh2_l, sc2_l), w_router[l], w_e1[l], w_e3[l], w_e2[l])
        if need_ctx:
            x_ctx = x_ctx + g1_c * y_ctx
            x_ctx = x_ctx + g2_c * expert_choice_ffn(modulate(x_ctx, sh2_c, sc2_c), w_router[l], w_e1[l], w_e3[l], w_e2[l])
    return rms_norm(x_lat, final_norm)
```

```python
import functools

import jax
import jax.numpy as jnp
from jax import lax
from jax.experimental import pallas as pl
from jax.experimental.pallas import tpu as pltpu

F32 = jnp.float32
BF16 = jnp.bfloat16
I32 = jnp.int32

D_MODEL = 2048
BATCH = 4
SEQ = 2048
DEPTH = 2
GRID_W = 64
CTX_LEN = 256
EPS = 1e-6
ROPE_BASE = 10000.0

ML_HEADS = 4
ML_HEAD_DIM = 256
ML_WIDTH = ML_HEADS * ML_HEAD_DIM
ML_GATES = 2 * 2 * ML_HEADS
ML_CHUNK = 256

LA_HEADS = 8
LA_NOPE = 128
LA_ROPE = 64
LA_VDIM = 128
LA_Q_RANK = 512
LA_KV_RANK = 256
LA_DK = 256

GA_HEADS = 8
GA_KV_HEADS = 2
GA_HEAD_DIM = 128
GA_WIDTH = GA_HEADS * GA_HEAD_DIM
GA_KV_WIDTH = GA_KV_HEADS * GA_HEAD_DIM

N_EXPERTS = 16
EXPERT_FF = 1024
CAPACITY_FACTOR = 2

M_LAT = BATCH * SEQ
M_CTX = BATCH * CTX_LEN
M_ALL = M_LAT + M_CTX
MODS_ROWS = 16
N_GROUPS = BATCH + 1

LANES = 128

OFF_MLG = 4 * ML_WIDTH
OFF_CQ = OFF_MLG + ML_GATES
OFF_CKV = OFF_CQ + LA_Q_RANK
OFF_KR = OFF_CKV + LA_KV_RANK
OFF_GAQ = OFF_KR + LA_ROPE
OFF_GAK = OFF_GAQ + GA_WIDTH
OFF_GAV = OFF_GAK + GA_KV_WIDTH
OFF_GATE = OFF_GAV + GA_KV_WIDTH
SM_CKV = LA_Q_RANK
SM_TAIL = SM_CKV + LA_KV_RANK
SMALL_WIDTH = SM_TAIL + LANES
TAIL_MLG = LA_ROPE

VMEM_LIMIT = 56 * 1024 * 1024


def _params(semantics, vmem=VMEM_LIMIT):
    return pltpu.CompilerParams(dimension_semantics=semantics, vmem_limit_bytes=vmem)


def _dot(a, b):
    return jnp.dot(a, b, preferred_element_type=F32)


def _dot_nt(a, b):
    return lax.dot_general(a, b, (((1,), (1,)), ((), ())), preferred_element_type=F32)


def _rms(x):
    return x * lax.rsqrt(jnp.mean(x * x, axis=-1, keepdims=True) + EPS)


def _group_of_block(tm):
    return lambda i: (i * tm) // SEQ


def _mods_kernel(c_ref, w_ref, b_ref, o_ref):
    c = c_ref[...]
    a = (c * jax.nn.sigmoid(c)).astype(BF16)
    o_ref[...] = _dot(a, w_ref[...].astype(BF16)) + b_ref[...]


def _mods(cc, w_mod, b_mod):
    depth, d, n = w_mod.shape
    rows = cc.shape[0]
    tn = 1024
    return pl.pallas_call(
        _mods_kernel,
        grid=(depth, n // tn),
        in_specs=[pl.BlockSpec((rows, d), lambda l, j: (0, 0)),
                  pl.BlockSpec((None, d, tn), lambda l, j: (l, 0, j)),
                  pl.BlockSpec((None, 1, tn), lambda l, j: (l, 0, j))],
        out_specs=pl.BlockSpec((None, rows, tn), lambda l, j: (l, 0, j)),
        out_shape=jax.ShapeDtypeStruct((depth, rows, n), F32),
        compiler_params=_params(("parallel", "parallel")),
        name="mods",
    )(cc, w_mod, b_mod)


def _rms_kernel(x_ref, scale_ref, shift_ref, o_ref):
    o_ref[...] = (_rms(x_ref[...]) * scale_ref[...] + shift_ref[...]).astype(o_ref.dtype)


def _rms_gain_kernel(x_ref, scale_ref, o_ref):
    o_ref[...] = (_rms(x_ref[...]) * scale_ref[...]).astype(o_ref.dtype)


def _rms_mod(x, scale_tab, shift_tab, m_rows, tm=512):
    d = x.shape[1]
    grp = _group_of_block(tm)
    return pl.pallas_call(
        _rms_kernel,
        grid=(m_rows // tm,),
        in_specs=[pl.BlockSpec((tm, d), lambda i: (i, 0)),
                  pl.BlockSpec((None, 1, d), lambda i: (grp(i), 0, 0)),
                  pl.BlockSpec((None, 1, d), lambda i: (grp(i), 0, 0))],
        out_specs=pl.BlockSpec((tm, d), lambda i: (i, 0)),
        out_shape=jax.ShapeDtypeStruct((m_rows, d), BF16),
        compiler_params=_params(("parallel",)),
        name="rms_mod",
    )(x, scale_tab, shift_tab)


def _rms_gain(x, gain, out_dtype, tm=512, name="rms_gain"):
    m_rows, width = x.shape
    return pl.pallas_call(
        _rms_gain_kernel,
        grid=(m_rows // tm,),
        in_specs=[pl.BlockSpec((tm, width), lambda i: (i, 0)),
                  pl.BlockSpec((1, width), lambda i: (0, 0))],
        out_specs=pl.BlockSpec((tm, width), lambda i: (i, 0)),
        out_shape=jax.ShapeDtypeStruct((m_rows, width), out_dtype),
        compiler_params=_params(("parallel",)),
        name=name,
    )(x, gain)


def _mm_kernel(a_ref, w_ref, o_ref):
    o_ref[...] = _dot(a_ref[...], w_ref[...]).astype(o_ref.dtype)


def _mm_res_kernel(a_ref, w_ref, x_ref, g_ref, o_ref):
    o_ref[...] = x_ref[...] + g_ref[...] * _dot(a_ref[...], w_ref[...])


def _mm(a, w, *, tm=1024, tn=1024, out_dtype=F32, name="mm"):
    m_rows, k = a.shape
    n_cols = w.shape[1]
    tm = min(tm, m_rows)
    tn = min(tn, n_cols)
    assert m_rows % tm == 0 and n_cols % tn == 0
    return pl.pallas_call(
        _mm_kernel,
        grid=(n_cols // tn, m_rows // tm),
        in_specs=[pl.BlockSpec((tm, k), lambda j, i: (i, 0)),
                  pl.BlockSpec((k, tn), lambda j, i: (0, j))],
        out_specs=pl.BlockSpec((tm, tn), lambda j, i: (i, j)),
        out_shape=jax.ShapeDtypeStruct((m_rows, n_cols), out_dtype),
        compiler_params=_params(("parallel", "arbitrary")),
        name=name,
    )(a, w)


def _mm_residual(a, w, x, gate_tab, m_rows, *, tm=1024, tn=1024, name="mm_res"):
    k = a.shape[1]
    n = w.shape[1]
    grp = _group_of_block(tm)
    return pl.pallas_call(
        _mm_res_kernel,
        grid=(n // tn, m_rows // tm),
        in_specs=[pl.BlockSpec((tm, k), lambda j, i: (i, 0)),
                  pl.BlockSpec((k, tn), lambda j, i: (0, j)),
                  pl.BlockSpec((tm, tn), lambda j, i: (i, j)),
                  pl.BlockSpec((None, 1, tn), lambda j, i: (grp(i), 0, j))],
        out_specs=pl.BlockSpec((tm, tn), lambda j, i: (i, j)),
        out_shape=jax.ShapeDtypeStruct((m_rows, n), F32),
        compiler_params=_params(("parallel", "arbitrary")),
        name=name,
    )(a, w, x, gate_tab)


def _merge_kernel(h_ref, y0_ref, y1_ref, y2_ref, wg0_ref, wg1_ref, wg2_ref, wb0_ref, wb1_ref, wb2_ref, o_ref):
    h = h_ref[...]
    acc = jax.nn.sigmoid(_dot(h, wg0_ref[...])) * _dot(y0_ref[...], wb0_ref[...])
    acc = acc + jax.nn.sigmoid(_dot(h, wg1_ref[...])) * _dot(y1_ref[...], wb1_ref[...])
    acc = acc + jax.nn.sigmoid(_dot(h, wg2_ref[...])) * _dot(y2_ref[...], wb2_ref[...])
    o_ref[...] = acc.astype(o_ref.dtype)


def _merge(h, y_ml, y_la, y_ga, w_gate, w_br_ml, w_br_la, w_br_ga, m_rows, tm=512, tn=512):
    d = D_MODEL
    nb = d // tn
    y_spec = pl.BlockSpec((tm, ML_WIDTH), lambda j, i: (i, 0))
    wb_spec = pl.BlockSpec((ML_WIDTH, tn), lambda j, i: (0, j))
    return pl.pallas_call(
        _merge_kernel,
        grid=(nb, m_rows // tm),
        in_specs=[pl.BlockSpec((tm, d), lambda j, i: (i, 0)), y_spec, y_spec, y_spec,
                  pl.BlockSpec((d, tn), lambda j, i: (0, j)),
                  pl.BlockSpec((d, tn), lambda j, i: (0, nb + j)),
                  pl.BlockSpec((d, tn), lambda j, i: (0, 2 * nb + j)),
                  wb_spec, wb_spec, wb_spec],
        out_specs=pl.BlockSpec((tm, tn), lambda j, i: (i, j)),
        out_shape=jax.ShapeDtypeStruct((m_rows, d), BF16),
        compiler_params=_params(("parallel", "arbitrary")),
        name="merge",
    )(h, y_ml, y_la, y_ga, w_gate, w_gate, w_gate, w_br_ml, w_br_la, w_br_ga)


def _mlstm_chunk(q, k, v, g_rows, g_cols, row_i, c_mat, n_vec, m, reverse):
    length = q.shape[0]
    li_r = g_rows[row_i:row_i + 1, :]
    lf_r = g_rows[row_i + 1:row_i + 2, :]
    li_c = g_cols[:, row_i:row_i + 1]
    lf_c = g_cols[:, row_i + 1:row_i + 2]
    ri = lax.broadcasted_iota(I32, (length, length), 0)
    ci = lax.broadcasted_iota(I32, (length, length), 1)
    if reverse:
        allowed = ci >= ri
        allowed_t = ri >= ci
    else:
        allowed = ci <= ri
        allowed_t = ri <= ci
    cum_c = jnp.sum(jnp.where(allowed, lf_r, 0.0), axis=1, keepdims=True)
    cum_r = jnp.sum(jnp.where(allowed_t, lf_c, 0.0), axis=0, keepdims=True)
    total = jnp.sum(lf_r, axis=1, keepdims=True)
    dmat = jnp.where(allowed, cum_c - cum_r + li_r, -jnp.inf)
    inter = cum_c + m
    m_row = jnp.maximum(inter, jnp.max(dmat, axis=1, keepdims=True))
    w = jnp.exp(dmat - m_row)
    qs = q * (ML_HEAD_DIM ** -0.5)
    qb = qs.astype(BF16)
    kb = k.astype(BF16)
    vb = v.astype(BF16)
    s = _dot_nt(qb, kb) * w
    w_inter = jnp.exp(inter - m_row)
    num = _dot(s.astype(BF16), vb) + w_inter * _dot_nt(qb, c_mat.astype(BF16))
    den = jnp.sum(s, axis=1, keepdims=True) + w_inter * jnp.sum(qs * n_vec, axis=1, keepdims=True)
    h_out = num / jnp.maximum(jnp.abs(den), jnp.exp(-m_row))
    g_r = total - cum_r + li_r
    g_c = total - cum_c + li_c
    m_new = jnp.maximum(total + m, jnp.max(g_r, axis=1, keepdims=True))
    wk_c = jnp.exp(g_c - m_new)
    decay = jnp.exp(total + m - m_new)
    wv_t = jnp.transpose(wk_c * v).astype(BF16)
    c_new = decay * c_mat + _dot(wv_t, kb)
    n_new = decay * n_vec + jnp.sum(wk_c * k, axis=0, keepdims=True)
    return h_out, c_new, n_new, m_new


def _mlstm_kernel(ql_ref, kl_ref, vl_ref, ol_ref, qc_ref, kc_ref, vc_ref, oc_ref,
                  grl_ref, gcl_ref, grc_ref, gcc_ref, gain_ref, y_ref,
                  accf_ref, accb_ref, yall_ref, c_ref, n_ref, m_ref):
    step = pl.program_id(2)
    n_lat = SEQ // ML_CHUNK
    length = ML_CHUNK

    @pl.when(step == 0)
    def _():
        c_ref[...] = jnp.zeros_like(c_ref)
        n_ref[...] = jnp.zeros_like(n_ref)
        m_ref[...] = jnp.zeros_like(m_ref)

        def run_chunk(direction, q, k, v, g_rows, g_cols, acc_start):
            h_out, c_new, n_new, m_new = _mlstm_chunk(q, k, v, g_rows, g_cols, 2 * direction, c_ref[direction],
                                                      n_ref[direction], m_ref[direction], direction == 1)
            c_ref[direction] = c_new
            n_ref[direction] = n_new
            m_ref[direction] = m_new
            acc_ref = accb_ref if direction == 1 else accf_ref
            acc_ref[pl.ds(acc_start, length), :] = h_out

        for direction in range(2):
            run_chunk(direction, qc_ref[...], kc_ref[...], vc_ref[...], grc_ref[...], gcc_ref[...], 0)

        def body(s_idx, carry):
            for direction in range(2):
                chunk = (n_lat - 1 - s_idx) if direction == 1 else s_idx
                start = pl.multiple_of(chunk * length, length)
                sl = pl.ds(start, length)
                run_chunk(direction, ql_ref[sl, :], kl_ref[sl, :], vl_ref[sl, :], grl_ref[chunk], gcl_ref[chunk],
                          pl.multiple_of(CTX_LEN + start, length))
            return carry

        lax.fori_loop(0, n_lat, body, 0)

        gain = gain_ref[...]

        def finish(rows, o_pre):
            h_sum = accf_ref[rows, :] + accb_ref[rows, :]
            return (_rms(h_sum) * gain * jax.nn.sigmoid(o_pre)).astype(BF16)

        yall_ref[0:CTX_LEN, :] = finish(slice(0, CTX_LEN), oc_ref[...])
        yall_ref[CTX_LEN:, :] = finish(slice(CTX_LEN, CTX_LEN + SEQ), ol_ref[...])

    y_ref[...] = yall_ref[pl.ds(pl.multiple_of(step * length, length), length), :]


def _mlstm(p_ml, g_row_lat, g_col_lat, g_row_ctx, g_col_ctx, gain):
    assert CTX_LEN == ML_CHUNK
    hd = ML_HEAD_DIM
    length = ML_CHUNK
    n_lat = SEQ // length
    ctx0 = M_LAT // CTX_LEN

    def lat_spec(col):
        return pl.BlockSpec((SEQ, hd), lambda b, h, s: (b, col * ML_HEADS + h))

    def ctx_spec(col):
        return pl.BlockSpec((CTX_LEN, hd), lambda b, h, s: (ctx0 + b, col * ML_HEADS + h))

    def out_map(b, h, s):
        return (jnp.where(s == 0, ctx0 + b, b * n_lat + s - 1), h)

    return pl.pallas_call(
        _mlstm_kernel,
        grid=(BATCH, ML_HEADS, 1 + n_lat),
        in_specs=[lat_spec(0), lat_spec(1), lat_spec(2), lat_spec(3),
                  ctx_spec(0), ctx_spec(1), ctx_spec(2), ctx_spec(3),
                  pl.BlockSpec((None, None, n_lat, 4, length), lambda b, h, s: (b, h, 0, 0, 0)),
                  pl.BlockSpec((None, None, n_lat, length, 4), lambda b, h, s: (b, h, 0, 0, 0)),
                  pl.BlockSpec((None, None, 4, length), lambda b, h, s: (b, h, 0, 0)),
                  pl.BlockSpec((None, None, length, 4), lambda b, h, s: (b, h, 0, 0)),
                  pl.BlockSpec((1, hd), lambda b, h, s: (0, h))],
        out_specs=pl.BlockSpec((length, hd), out_map),
        out_shape=jax.ShapeDtypeStruct((M_ALL, ML_WIDTH), BF16),
        scratch_shapes=[pltpu.VMEM((CTX_LEN + SEQ, hd), F32),
                        pltpu.VMEM((CTX_LEN + SEQ, hd), F32),
                        pltpu.VMEM((CTX_LEN + SEQ, hd), BF16),
                        pltpu.VMEM((2, hd, hd), F32),
                        pltpu.VMEM((2, 1, hd), F32),
                        pltpu.VMEM((2, 1, 1), F32)],
        compiler_params=_params(("parallel", "parallel", "arbitrary")),
        name="mlstm",
    )(p_ml, p_ml, p_ml, p_ml, p_ml, p_ml, p_ml, p_ml, g_row_lat, g_col_lat, g_row_ctx, g_col_ctx, gain)


def _rope_partner(x, half):
    lane = lax.broadcasted_iota(I32, x.shape, 1)
    first = (lane & (2 * half - 1)) < half
    return jnp.where(first, pltpu.roll(x, LANES - half, 1), pltpu.roll(x, half, 1))


def _ga_prep_kernel(h_ref, w_ref, gq_ref, gk_ref, cos_ref, sin_ref, q_ref, k_ref, v_ref):
    p = _dot(h_ref[...], w_ref[...])
    cos = cos_ref[...]
    sin = sin_ref[...]
    dh = GA_HEAD_DIM

    def norm_rope(x, gain):
        y = _rms(x) * gain
        return y * cos + _rope_partner(y, dh // 4) * sin

    for h in range(GA_HEADS):
        q = norm_rope(p[:, h * dh:(h + 1) * dh], gq_ref[...])
        q_ref[h] = (q * dh ** -0.5).astype(BF16)
    for g in range(GA_KV_HEADS):
        k0 = GA_WIDTH + g * dh
        v0 = GA_WIDTH + GA_KV_WIDTH + g * dh
        k_ref[g] = norm_rope(p[:, k0:k0 + dh], gk_ref[...]).astype(BF16)
        v_ref[g] = p[:, v0:v0 + dh].astype(BF16)


def _ga_prep(h, w_ga, gain_q, gain_k, cos, sin, tm=512):
    d = h.shape[1]
    n = w_ga.shape[1]
    dh = GA_HEAD_DIM

    def head_spec(n_heads):
        return pl.BlockSpec((n_heads, tm, dh), lambda i: (0, i, 0))

    return pl.pallas_call(
        _ga_prep_kernel,
        grid=(M_ALL // tm,),
        in_specs=[pl.BlockSpec((tm, d), lambda i: (i, 0)),
                  pl.BlockSpec((d, n), lambda i: (0, 0)),
                  pl.BlockSpec((1, dh), lambda i: (0, 0)),
                  pl.BlockSpec((1, dh), lambda i: (0, 0)),
                  pl.BlockSpec((tm, dh), lambda i: (i, 0)),
                  pl.BlockSpec((tm, dh), lambda i: (i, 0))],
        out_specs=[head_spec(GA_HEADS), head_spec(GA_KV_HEADS), head_spec(GA_KV_HEADS)],
        out_shape=[jax.ShapeDtypeStruct((GA_HEADS, M_ALL, dh), BF16),
                   jax.ShapeDtypeStruct((GA_KV_HEADS, M_ALL, dh), BF16),
                   jax.ShapeDtypeStruct((GA_KV_HEADS, M_ALL, dh), BF16)],
        compiler_params=_params(("parallel",)),
        name="ga_prep",
    )(h, w_ga, gain_q, gain_k, cos, sin)


def _la_prep_kernel(h_ref, wsm_ref, gq_ref, gkv_ref, wuq_ref, wukv_ref, cos_ref, sin_ref,
                    q_ref, k_ref, v_ref, tail_ref):
    p = _dot(h_ref[...], wsm_ref[...])
    tail = p[:, SM_TAIL:SM_TAIL + LANES]
    tail_ref[...] = tail
    cq = (_rms(p[:, :LA_Q_RANK]) * gq_ref[...]).astype(BF16)
    ckv = (_rms(p[:, SM_CKV:SM_CKV + LA_KV_RANK]) * gkv_ref[...]).astype(BF16)
    qf = _dot(cq, wuq_ref[...])
    kvf = _dot(ckv, wukv_ref[...])
    cos = cos_ref[...]
    sin = sin_ref[...]
    scale = (LA_NOPE + LA_ROPE) ** -0.5
    low = lax.broadcasted_iota(I32, tail.shape, 1) < LA_ROPE

    def rope(x):
        return x * cos + _rope_partner(x, LA_ROPE // 4) * sin

    kr = rope(tail)
    kr2 = jnp.where(low, kr, pltpu.roll(kr, LA_ROPE, 1)).astype(BF16)
    heads_per_tile = LANES // LA_ROPE
    for j in range(LA_HEADS // heads_per_tile):
        c0 = LA_HEADS * LA_NOPE + j * LANES
        qr = rope(qf[:, c0:c0 + LANES]) * scale
        q_ref[2 * j, :, LA_NOPE:] = jnp.where(low, qr, 0.0).astype(BF16)
        q_ref[2 * j + 1, :, LA_NOPE:] = jnp.where(low, 0.0, qr).astype(BF16)
    for h in range(LA_HEADS):
        q_ref[h, :, :LA_NOPE] = (qf[:, h * LA_NOPE:(h + 1) * LA_NOPE] * scale).astype(BF16)
        kv0 = h * (LA_NOPE + LA_VDIM)
        k_ref[h, :, :LA_NOPE] = kvf[:, kv0:kv0 + LA_NOPE].astype(BF16)
        k_ref[h, :, LA_NOPE:] = kr2
        v_ref[h] = kvf[:, kv0 + LA_NOPE:kv0 + LA_NOPE + LA_VDIM].astype(BF16)


def _la_prep(h, w_small, gain_q, gain_kv, w_uq, w_ukv, cos, sin, tm=512):
    assert LA_HEADS % (LANES // LA_ROPE) == 0 and LA_NOPE == LANES and LA_DK == LA_NOPE + LANES
    d = h.shape[1]

    def full(a):
        return pl.BlockSpec(a.shape, lambda i: (0,) * a.ndim)

    return pl.pallas_call(
        _la_prep_kernel,
        grid=(M_ALL // tm,),
        in_specs=[pl.BlockSpec((tm, d), lambda i: (i, 0)), full(w_small), full(gain_q), full(gain_kv),
                  full(w_uq), full(w_ukv),
                  pl.BlockSpec((tm, LANES), lambda i: (i, 0)),
                  pl.BlockSpec((tm, LANES), lambda i: (i, 0))],
        out_specs=[pl.BlockSpec((LA_HEADS, tm, LA_DK), lambda i: (0, i, 0)),
                   pl.BlockSpec((LA_HEADS, tm, LA_DK), lambda i: (0, i, 0)),
                   pl.BlockSpec((LA_HEADS, tm, LA_VDIM), lambda i: (0, i, 0)),
                   pl.BlockSpec((tm, LANES), lambda i: (i, 0))],
        out_shape=[jax.ShapeDtypeStruct((LA_HEADS, M_ALL, LA_DK), BF16),
                   jax.ShapeDtypeStruct((LA_HEADS, M_ALL, LA_DK), BF16),
                   jax.ShapeDtypeStruct((LA_HEADS, M_ALL, LA_VDIM), BF16),
                   jax.ShapeDtypeStruct((M_ALL, LANES), F32)],
        compiler_params=_params(("parallel",)),
        name="la_prep",
    )(h, w_small, gain_q, gain_kv, w_uq, w_ukv, cos, sin)


ATT_TQ = 256
ATT_ROWS = 1024


def _attn_kernel(q_ref, kl_ref, vl_ref, kc_ref, vc_ref, kca_ref, vca_ref, o_ref, *, n_lat_blocks):
    i = pl.program_id(1)

    @pl.when(i < n_lat_blocks)
    def _():
        for sub in range(ATT_ROWS // ATT_TQ):
            rows = slice(sub * ATT_TQ, (sub + 1) * ATT_TQ)
            q = q_ref[rows, :]
            s_l = _dot_nt(q, kl_ref[...])
            s_c = _dot_nt(q, kc_ref[...])
            mx = jnp.maximum(jnp.max(s_l, axis=1, keepdims=True), jnp.max(s_c, axis=1, keepdims=True))
            p_l = jnp.exp(s_l - mx)
            p_c = jnp.exp(s_c - mx)
            den = jnp.sum(p_l, axis=1, keepdims=True) + jnp.sum(p_c, axis=1, keepdims=True)
            o = _dot(p_l.astype(BF16), vl_ref[...]) + _dot(p_c.astype(BF16), vc_ref[...])
            o_ref[rows, :] = (o / den).astype(o_ref.dtype)

    @pl.when(i >= n_lat_blocks)
    def _():
        for sub in range(ATT_ROWS // CTX_LEN):
            rows = slice(sub * CTX_LEN, (sub + 1) * CTX_LEN)
            s = _dot_nt(q_ref[rows, :], kca_ref[rows, :])
            mx = jnp.max(s, axis=1, keepdims=True)
            p = jnp.exp(s - mx)
            den = jnp.sum(p, axis=1, keepdims=True)
            o = _dot(p.astype(BF16), vca_ref[rows, :])
            o_ref[rows, :] = (o / den).astype(o_ref.dtype)


def _attention(q, k, v, kv_group, need_ctx, name):
    assert M_CTX == ATT_ROWS and CTX_LEN == ATT_TQ
    n_heads, _, dk = q.shape
    dv = v.shape[2]
    n_lat_blocks = M_LAT // ATT_ROWS
    per_sample = SEQ // ATT_ROWS
    ctx0 = M_LAT // CTX_LEN
    n_blocks = n_lat_blocks + (1 if need_ctx else 0)

    def sample(i):
        return jnp.minimum(i // per_sample, BATCH - 1)

    def lat_spec(dim):
        return pl.BlockSpec((None, SEQ, dim), lambda h, i: (h // kv_group, sample(i), 0))

    def ctx_spec(dim):
        return pl.BlockSpec((None, CTX_LEN, dim), lambda h, i: (h // kv_group, ctx0 + sample(i), 0))

    def ctx_all_spec(dim):
        return pl.BlockSpec((None, M_CTX, dim), lambda h, i: (h // kv_group, n_lat_blocks, 0))

    return pl.pallas_call(
        functools.partial(_attn_kernel, n_lat_blocks=n_lat_blocks),
        grid=(n_heads, n_blocks),
        in_specs=[pl.BlockSpec((None, ATT_ROWS, dk), lambda h, i: (h, i, 0)),
                  lat_spec(dk), lat_spec(dv), ctx_spec(dk), ctx_spec(dv), ctx_all_spec(dk), ctx_all_spec(dv)],
        out_specs=pl.BlockSpec((ATT_ROWS, dv), lambda h, i: (i, h)),
        out_shape=jax.ShapeDtypeStruct((n_blocks * ATT_ROWS, n_heads * dv), BF16),
        compiler_params=_params(("parallel", "arbitrary")),
        name=name,
    )(q, k, v, k, v, k, v)


def _router_kernel(h_ref, w_ref, o_ref):
    logits = _dot_nt(w_ref[...], h_ref[...])
    mx = jnp.max(logits, axis=0, keepdims=True)
    e = jnp.exp(logits - mx)
    o_ref[...] = e / jnp.sum(e, axis=0, keepdims=True)


def _router(h, w_router_t, m_rows, tm=1024):
    d = h.shape[1]
    return pl.pallas_call(
        _router_kernel,
        grid=(m_rows // tm,),
        in_specs=[pl.BlockSpec((tm, d), lambda i: (i, 0)),
                  pl.BlockSpec((N_EXPERTS, d), lambda i: (0, 0))],
        out_specs=pl.BlockSpec((N_EXPERTS, tm), lambda i: (0, i)),
        out_shape=jax.ShapeDtypeStruct((N_EXPERTS, m_rows), F32),
        compiler_params=_params(("parallel",)),
        name="router",
    )(h, w_router_t)


RANK_BLOCK = 256


def _topk_kernel(aff_ref, rank_ref, *, cap):
    a_row = aff_ref[...]
    n_tok = a_row.shape[1]
    rb = RANK_BLOCK
    n_blk = n_tok // rb
    ri = lax.broadcasted_iota(I32, (rb, rb), 0)
    ci = lax.broadcasted_iota(I32, (rb, rb), 1)
    earlier = ri < ci
    counts = [jnp.zeros((rb, rb), F32) for _ in range(n_blk)]
    for blk in range(n_blk):
        rival = jnp.transpose(jnp.broadcast_to(a_row[:, blk * rb:(blk + 1) * rb], (rb, rb)))
        for c in range(n_blk):
            tok = a_row[:, c * rb:(c + 1) * rb]
            if c < blk:
                beats = rival > tok
            elif c > blk:
                beats = rival >= tok
            else:
                beats = (rival > tok) | ((rival == tok) & earlier)
            counts[c] = counts[c] + jnp.where(beats, 1.0, 0.0)
    upper = jnp.where(earlier, 1.0, 0.0).astype(BF16)
    carry = jnp.zeros((1, 1), F32)
    for c in range(n_blk):
        sel = jnp.sum(counts[c], axis=0, keepdims=True) < cap
        sel_f = jnp.where(sel, 1.0, 0.0)
        before = _dot(jnp.broadcast_to(sel_f, (16, rb)).astype(BF16), upper)[0:1, :] + carry
        carry = carry + jnp.sum(sel_f, axis=1, keepdims=True)
        rank_ref[:, c * rb:(c + 1) * rb] = jnp.where(sel, before.astype(I32), -1)


def _topk(aff3, n_tok, cap, blk0, n_sets):
    return pl.pallas_call(
        functools.partial(_topk_kernel, cap=cap),
        grid=(n_sets, N_EXPERTS),
        in_specs=[pl.BlockSpec((None, 1, n_tok), lambda s, e: (e, 0, blk0 + s))],
        out_specs=pl.BlockSpec((None, 1, n_tok), lambda s, e: (e, 0, s)),
        out_shape=jax.ShapeDtypeStruct((N_EXPERTS, 1, n_sets * n_tok), I32),
        compiler_params=_params(("parallel", "parallel")),
        name=f"topk_{n_tok}",
    )(aff3)


def _gather_kernel(rank_ref, aff_ref, h_ref, *rest, cap):
    xg_ref, val_ref = rest[-2:]
    r = rank_ref[...]
    n_tok = r.shape[1]
    slot = lax.broadcasted_iota(I32, (cap, n_tok), 0)
    hit = r == slot
    onehot = jnp.where(hit, 1.0, 0.0).astype(BF16)
    xg_ref[...] = _dot(onehot, h_ref[...]).astype(xg_ref.dtype)
    val_ref[...] = jnp.sum(jnp.where(hit, aff_ref[...], 0.0), axis=1, keepdims=True)


def _gather(rank3, aff3, h, n_tok, cap, blk0, row0, rows_total, prev):
    d = h.shape[1]
    rb0 = row0 // cap
    in_specs = [pl.BlockSpec((None, 1, n_tok), lambda b, e: (e, 0, b)),
                pl.BlockSpec((None, 1, n_tok), lambda b, e: (e, 0, blk0 + b)),
                pl.BlockSpec((n_tok, d), lambda b, e: (blk0 + b, 0))]
    args = [rank3, aff3, h]
    aliases = {}
    if prev is not None:
        in_specs += [pl.BlockSpec(memory_space=pl.ANY), pl.BlockSpec(memory_space=pl.ANY)]
        args += list(prev)
        aliases = {3: 0, 4: 1}
    return pl.pallas_call(
        functools.partial(_gather_kernel, cap=cap),
        grid=(BATCH, N_EXPERTS),
        in_specs=in_specs,
        out_specs=[pl.BlockSpec((None, cap, d), lambda b, e: (e, rb0 + b, 0)),
                   pl.BlockSpec((None, cap, 1), lambda b, e: (e, rb0 + b, 0))],
        out_shape=[jax.ShapeDtypeStruct((N_EXPERTS, rows_total, d), BF16),
                   jax.ShapeDtypeStruct((N_EXPERTS, rows_total, 1), F32)],
        input_output_aliases=aliases,
        compiler_params=_params(("parallel", "arbitrary")),
        name=f"moe_gather_{n_tok}",
    )(*args)


EXPERT_COLS = 512


def _expert_kernel(x_ref, w1_ref, w3_ref, w2_ref, val_ref, y_ref, acc_ref):
    f = pl.program_id(1)
    x = x_ref[...]
    a = _dot(x, w1_ref[...].astype(BF16))
    u = _dot(x, w3_ref[...].astype(BF16))
    hm = (a * jax.nn.sigmoid(a) * u).astype(BF16)
    d = acc_ref.shape[1]
    for c0 in range(0, d, EXPERT_COLS):
        cols = slice(c0, c0 + EXPERT_COLS)
        part = _dot(hm, w2_ref[:, cols].astype(BF16))

        @pl.when(f == 0)
        def _(part=part, cols=cols):
            acc_ref[:, cols] = part

        @pl.when(f > 0)
        def _(part=part, cols=cols):
            acc_ref[:, cols] += part

    @pl.when(f == pl.num_programs(1) - 1)
    def _():
        y_ref[...] = (acc_ref[...] * val_ref[...]).astype(y_ref.dtype)


def _experts(xg, vals, w_e1, w_e3, w_e2, layer, tf=256):
    n_exp, rows, d = xg.shape
    ff = w_e1.shape[3]
    return pl.pallas_call(
        _expert_kernel,
        grid=(n_exp, ff // tf),
        in_specs=[pl.BlockSpec((None, rows, d), lambda e, f: (e, 0, 0)),
                  pl.BlockSpec((None, None, d, tf), lambda e, f: (layer, e, 0, f)),
                  pl.BlockSpec((None, None, d, tf), lambda e, f: (layer, e, 0, f)),
                  pl.BlockSpec((None, None, tf, d), lambda e, f: (layer, e, f, 0)),
                  pl.BlockSpec((None, rows, 1), lambda e, f: (e, 0, 0))],
        out_specs=pl.BlockSpec((None, rows, d), lambda e, f: (e, 0, 0)),
        out_shape=jax.ShapeDtypeStruct((n_exp, rows, d), BF16),
        scratch_shapes=[pltpu.VMEM((rows, d), F32)],
        compiler_params=_params(("parallel", "arbitrary")),
        name="moe_experts",
    )(xg, w_e1, w_e3, w_e2, vals)


COMBINE_LANES = 1024


def _combine_kernel(rank_ref, y_ref, x_ref, g_ref, *rest, cap):
    o_ref, lhs_ref = rest[-2:]
    n_slots = N_EXPERTS * cap
    chunk = min(COMBINE_LANES, n_slots)
    shift = cap.bit_length() - 1

    @pl.when(pl.program_id(2) == 0)
    def _():
        r = rank_ref[...].astype(F32).astype(BF16)
        tm = r.shape[0]
        for c0 in range(0, n_slots, chunk):
            lane = c0 + lax.broadcasted_iota(I32, (LANES, chunk), 1)
            row = lax.broadcasted_iota(I32, (LANES, chunk), 0)
            expand = jnp.where(lax.shift_right_logical(lane, shift) == row, 1.0, 0.0).astype(BF16)
            r_exp = _dot(r, expand)
            slot = (c0 + lax.broadcasted_iota(I32, (tm, chunk), 1)) & (cap - 1)
            lhs_ref[:, c0:c0 + chunk] = jnp.where(r_exp == slot.astype(F32), 1.0, 0.0).astype(BF16)

    y = y_ref[...]
    o_ref[...] = x_ref[...] + g_ref[...] * _dot(lhs_ref[...], y.reshape(n_slots, y.shape[2]))


def _combine(rank_t, y, x, gate_tab, out_prev, n_tok, cap, blk0, row0, tm, tn=1024):
    assert cap & (cap - 1) == 0
    d = x.shape[1]
    tpb = n_tok // tm
    xrow0 = blk0 * tpb
    rb0 = row0 // cap

    def grp(b):
        return b if n_tok == SEQ else BATCH

    in_specs = [pl.BlockSpec((tm, LANES), lambda b, i, j: (b * tpb + i, 0)),
                pl.BlockSpec((N_EXPERTS, cap, tn), lambda b, i, j: (0, rb0 + b, j)),
                pl.BlockSpec((tm, tn), lambda b, i, j: (xrow0 + b * tpb + i, j)),
                pl.BlockSpec((None, 1, tn), lambda b, i, j: (grp(b), 0, j))]
    args = [rank_t, y, x, gate_tab]
    aliases = {}
    if out_prev is not None:
        in_specs.append(pl.BlockSpec(memory_space=pl.ANY))
        args.append(out_prev)
        aliases = {4: 0}
    return pl.pallas_call(
        functools.partial(_combine_kernel, cap=cap),
        grid=(BATCH, tpb, d // tn),
        in_specs=in_specs,
        out_specs=pl.BlockSpec((tm, tn), lambda b, i, j: (xrow0 + b * tpb + i, j)),
        out_shape=jax.ShapeDtypeStruct(x.shape, F32),
        scratch_shapes=[pltpu.VMEM((tm, N_EXPERTS * cap), BF16)],
        input_output_aliases=aliases,
        compiler_params=_params(("parallel", "parallel", "arbitrary")),
        name=f"moe_combine_{n_tok}",
    )(*args)


def _expert_choice(h2, x, gate_tab, w_router_t, w_e1, w_e3, w_e2, layer, need_ctx):
    m_rows = h2.shape[0]
    aff3 = _router(h2, w_router_t, m_rows).reshape(N_EXPERTS, 1, m_rows)
    sets = [(SEQ, 0)]
    if need_ctx:
        sets.append((CTX_LEN, M_LAT // CTX_LEN))
    caps = [CAPACITY_FACTOR * n_tok // N_EXPERTS for n_tok, _ in sets]
    rows_total = sum(BATCH * cap for cap in caps)
    ranks, gathered, row0 = [], None, 0
    for (n_tok, blk0), cap in zip(sets, caps):
        rank3 = _topk(aff3, n_tok, cap, blk0, BATCH)
        gathered = _gather(rank3, aff3, h2, n_tok, cap, blk0, row0, rows_total, gathered)
        ranks.append((rank3, row0))
        row0 += BATCH * cap
    y = _experts(gathered[0], gathered[1], w_e1, w_e3, w_e2, layer)
    out = None
    for (n_tok, blk0), cap, (rank3, row0) in zip(sets, caps, ranks):
        rank_t = jnp.pad(rank3.reshape(N_EXPERTS, BATCH * n_tok).T, ((0, 0), (0, LANES - N_EXPERTS)))
        out = _combine(rank_t, y, x, gate_tab, out, n_tok, cap, blk0, row0, tm=min(512, n_tok))
    return out


def _rope_tables(dim):
    rows = SEQ // GRID_W
    quarter = dim // 4
    inv = ROPE_BASE ** (-jnp.arange(quarter, dtype=F32) / quarter)
    row = jnp.broadcast_to(jnp.arange(rows, dtype=F32)[:, None], (rows, GRID_W)).reshape(-1)
    col = jnp.broadcast_to(jnp.arange(GRID_W, dtype=F32)[None, :], (rows, GRID_W)).reshape(-1)
    ang_r = row[:, None] * inv[None, :]
    ang_c = col[:, None] * inv[None, :]
    cos = jnp.concatenate([jnp.cos(ang_r), jnp.cos(ang_r), jnp.cos(ang_c), jnp.cos(ang_c)], axis=-1)
    sin = jnp.concatenate([-jnp.sin(ang_r), jnp.sin(ang_r), -jnp.sin(ang_c), jnp.sin(ang_c)], axis=-1)
    cos = jnp.tile(cos, (BATCH, LANES // dim))
    sin = jnp.tile(sin, (BATCH, LANES // dim))
    cos = jnp.concatenate([cos, jnp.ones((M_CTX, LANES), F32)], axis=0)
    sin = jnp.concatenate([sin, jnp.zeros((M_CTX, LANES), F32)], axis=0)
    return cos, sin


def _layer(x, mods, layer, w_in, ml_gate_bias, ml_norm, la_q_norm, la_kv_norm, la_w_uq, la_w_ukv,
           ga_q_norm, ga_k_norm, w_br_ml, w_br_la, w_br_ga, w_out, w_router, w_e1, w_e3, w_e2, ropes, need_ctx):
    d = D_MODEL
    m_rows = M_ALL if need_ctx else M_LAT
    sh1, sc1, g1, sh2, sc2, g2 = [mods[:N_GROUPS, k * d:(k + 1) * d].reshape(N_GROUPS, 1, d) for k in range(6)]
    (cos_la, sin_la), (cos_ga, sin_ga) = ropes

    w_ml = w_in[:, :OFF_MLG].astype(BF16)
    w_small = jnp.concatenate([w_in[:, OFF_CQ:OFF_GAQ], w_in[:, OFF_MLG:OFF_CQ],
                               jnp.zeros((d, SMALL_WIDTH - (OFF_GAQ - OFF_MLG)), F32)], axis=1).astype(BF16)
    w_ga = w_in[:, OFF_GAQ:OFF_GATE].astype(BF16)
    w_gate = w_in[:, OFF_GATE:].astype(BF16)
    w_uq = la_w_uq.reshape(LA_Q_RANK, LA_HEADS, LA_NOPE + LA_ROPE)
    w_uq = jnp.concatenate([w_uq[:, :, :LA_NOPE].reshape(LA_Q_RANK, LA_HEADS * LA_NOPE),
                            w_uq[:, :, LA_NOPE:].reshape(LA_Q_RANK, LA_HEADS * LA_ROPE)], axis=1).astype(BF16)

    h = _rms_mod(x, 1.0 + sc1, sh1, M_ALL)
    p_ml = _mm(h, w_ml, name="proj_ml")
    q_la, k_la, v_la, tail = _la_prep(h, w_small, la_q_norm.reshape(1, LA_Q_RANK), la_kv_norm.reshape(1, LA_KV_RANK),
                                      w_uq, la_w_ukv.astype(BF16), cos_la, sin_la)
    q_ga, k_ga, v_ga = _ga_prep(h, w_ga, ga_q_norm.reshape(1, GA_HEAD_DIM), ga_k_norm.reshape(1, GA_HEAD_DIM),
                                cos_ga, sin_ga)

    gates = (tail[:, TAIL_MLG:TAIL_MLG + ML_GATES] + ml_gate_bias[None, :]).reshape(M_ALL, 2, 2, ML_HEADS)
    gates = jnp.stack([gates[:, 0, 0], jax.nn.log_sigmoid(gates[:, 0, 1]),
                       gates[:, 1, 0], jax.nn.log_sigmoid(gates[:, 1, 1])], axis=-1)
    n_lat = SEQ // ML_CHUNK
    g_lat = gates[:M_LAT].reshape(BATCH, n_lat, ML_CHUNK, ML_HEADS, 4)
    g_ctx = gates[M_LAT:].reshape(BATCH, CTX_LEN, ML_HEADS, 4)
    y_ml = _mlstm(p_ml, jnp.transpose(g_lat, (0, 3, 1, 4, 2)), jnp.transpose(g_lat, (0, 3, 1, 2, 4)),
                  jnp.transpose(g_ctx, (0, 2, 3, 1)), jnp.transpose(g_ctx, (0, 2, 1, 3)),
                  ml_norm.reshape(1, ML_WIDTH))

    y_la = _attention(q_la, k_la, v_la, 1, need_ctx, "attn_mla")
    y_ga = _attention(q_ga, k_ga, v_ga, GA_HEADS // GA_KV_HEADS, need_ctx, "attn_gqa")

    merged = _merge(h, y_ml, y_la, y_ga, w_gate, w_br_ml.astype(BF16), w_br_la.astype(BF16), w_br_ga.astype(BF16),
                    m_rows)
    x = _mm_residual(merged, w_out.astype(BF16), x, g1, m_rows, name="out_proj")

    h2 = _rms_mod(x, 1.0 + sc2, sh2, m_rows)
    return _expert_choice(h2, x, g2, w_router.T.astype(BF16), w_e1, w_e3, w_e2, layer, need_ctx)


def kernel(x, c, ctx, c_ctx, w_mod, b_mod, w_in, ml_gate_bias, ml_norm, la_q_norm, la_kv_norm, la_w_uq, la_w_ukv,
           ga_q_norm, ga_k_norm, w_br_ml, w_br_la, w_br_ga, w_out, w_router, w_e1, w_e3, w_e2, final_norm):
    d = D_MODEL
    xs = jnp.concatenate([x.reshape(M_LAT, d), ctx.reshape(M_CTX, d)], axis=0)
    cc = jnp.concatenate([c, c_ctx[None, :], jnp.zeros((MODS_ROWS - N_GROUPS, d), F32)], axis=0)
    mods = _mods(cc, w_mod, b_mod.reshape(DEPTH, 1, 6 * d))
    ropes = (_rope_tables(LA_ROPE), _rope_tables(GA_HEAD_DIM))
    for l in range(DEPTH):
        need_ctx = l < DEPTH - 1
        xs = _layer(xs, mods[l], l, w_in[l], ml_gate_bias[l], ml_norm[l], la_q_norm[l], la_kv_norm[l],
                    la_w_uq[l], la_w_ukv[l], ga_q_norm[l], ga_k_norm[l], w_br_ml[l], w_br_la[l], w_br_ga[l],
                    w_out[l], w_router[l], w_e1, w_e3, w_e2, ropes, need_ctx)
    out = _rms_gain(xs, final_norm.reshape(1, d), F32, name="final_norm")
    return out.reshape(BATCH, SEQ, d)
```

```python
import functools

import jax
import jax.numpy as jnp
from jax import lax
from jax.experimental import pallas as pl
from jax.experimental.pallas import tpu as pltpu

F32 = jnp.float32
BF16 = jnp.bfloat16
I32 = jnp.int32

D_MODEL = 2048
BATCH = 4
SEQ = 2048
DEPTH = 2
GRID_W = 64
CTX_LEN = 256
EPS = 1e-6
ROPE_BASE = 10000.0

ML_HEADS = 4
ML_HEAD_DIM = 256
ML_WIDTH = ML_HEADS * ML_HEAD_DIM
ML_GATES = 2 * 2 * ML_HEADS
ML_CHUNK = 256

LA_HEADS = 8
LA_NOPE = 128
LA_ROPE = 64
LA_VDIM = 128
LA_Q_RANK = 512
LA_KV_RANK = 256
LA_DK = 256

GA_HEADS = 8
GA_KV_HEADS = 2
GA_HEAD_DIM = 128
GA_WIDTH = GA_HEADS * GA_HEAD_DIM
GA_KV_WIDTH = GA_KV_HEADS * GA_HEAD_DIM

N_EXPERTS = 16
EXPERT_FF = 1024
CAPACITY_FACTOR = 2

M_LAT = BATCH * SEQ
M_CTX = BATCH * CTX_LEN
M_ALL = M_LAT + M_CTX
MODS_ROWS = 16
N_GROUPS = BATCH + 1

LANES = 128

OFF_MLG = 4 * ML_WIDTH
OFF_CQ = OFF_MLG + ML_GATES
OFF_CKV = OFF_CQ + LA_Q_RANK
OFF_KR = OFF_CKV + LA_KV_RANK
OFF_GAQ = OFF_KR + LA_ROPE
OFF_GAK = OFF_GAQ + GA_WIDTH
OFF_GAV = OFF_GAK + GA_KV_WIDTH
OFF_GATE = OFF_GAV + GA_KV_WIDTH
SM_CKV = LA_Q_RANK
SM_TAIL = SM_CKV + LA_KV_RANK
SMALL_WIDTH = SM_TAIL + LANES
TAIL_MLG = LA_ROPE

VMEM_LIMIT = 56 * 1024 * 1024


def _params(semantics, vmem=VMEM_LIMIT):
    return pltpu.CompilerParams(dimension_semantics=semantics, vmem_limit_bytes=vmem)


def _dot(a, b):
    return jnp.dot(a, b, preferred_element_type=F32)


def _dot_nt(a, b):
    return lax.dot_general(a, b, (((1,), (1,)), ((), ())), preferred_element_type=F32)


def _rms(x):
    return x * lax.rsqrt(jnp.mean(x * x, axis=-1, keepdims=True) + EPS)


def _group_of_block(tm):
    return lambda i: (i * tm) // SEQ


def _mods_kernel(c_ref, w_ref, b_ref, o_ref):
    c = c_ref[...]
    a = (c * jax.nn.sigmoid(c)).astype(BF16)
    o_ref[...] = _dot(a, w_ref[...].astype(BF16)) + b_ref[...]


def _mods(cc, w_mod, b_mod):
    depth, d, n = w_mod.shape
    rows = cc.shape[0]
    tn = 1024
    return pl.pallas_call(
        _mods_kernel,
        grid=(depth, n // tn),
        in_specs=[pl.BlockSpec((rows, d), lambda l, j: (0, 0)),
                  pl.BlockSpec((None, d, tn), lambda l, j: (l, 0, j)),
                  pl.BlockSpec((None, 1, tn), lambda l, j: (l, 0, j))],
        out_specs=pl.BlockSpec((None, rows, tn), lambda l, j: (l, 0, j)),
        out_shape=jax.ShapeDtypeStruct((depth, rows, n), F32),
        compiler_params=_params(("parallel", "parallel")),
        name="mods",
    )(cc, w_mod, b_mod)


def _rms_kernel(x_ref, scale_ref, shift_ref, o_ref):
    o_ref[...] = (_rms(x_ref[...]) * scale_ref[...] + shift_ref[...]).astype(o_ref.dtype)


def _rms_gain_kernel(x_ref, scale_ref, o_ref):
    o_ref[...] = (_rms(x_ref[...]) * scale_ref[...]).astype(o_ref.dtype)


def _rms_mod(x, scale_tab, shift_tab, m_rows, tm=512):
    d = x.shape[1]
    grp = _group_of_block(tm)
    return pl.pallas_call(
        _rms_kernel,
        grid=(m_rows // tm,),
        in_specs=[pl.BlockSpec((tm, d), lambda i: (i, 0)),
                  pl.BlockSpec((None, 1, d), lambda i: (grp(i), 0, 0)),
                  pl.BlockSpec((None, 1, d), lambda i: (grp(i), 0, 0))],
        out_specs=pl.BlockSpec((tm, d), lambda i: (i, 0)),
        out_shape=jax.ShapeDtypeStruct((m_rows, d), BF16),
        compiler_params=_params(("parallel",)),
        name="rms_mod",
    )(x, scale_tab, shift_tab)


def _rms_gain(x, gain, out_dtype, tm=512, name="rms_gain"):
    m_rows, width = x.shape
    return pl.pallas_call(
        _rms_gain_kernel,
        grid=(m_rows // tm,),
        in_specs=[pl.BlockSpec((tm, width), lambda i: (i, 0)),
                  pl.BlockSpec((1, width), lambda i: (0, 0))],
        out_specs=pl.BlockSpec((tm, width), lambda i: (i, 0)),
        out_shape=jax.ShapeDtypeStruct((m_rows, width), out_dtype),
        compiler_params=_params(("parallel",)),
        name=name,
    )(x, gain)


def _mm_kernel(a_ref, w_ref, o_ref):
    o_ref[...] = _dot(a_ref[...], w_ref[...]).astype(o_ref.dtype)


def _mm_res_kernel(a_ref, w_ref, x_ref, g_ref, o_ref):
    o_ref[...] = x_ref[...] + g_ref[...] * _dot(a_ref[...], w_ref[...])


def _mm(a, w, *, tm=1024, tn=1024, out_dtype=F32, name="mm"):
    m_rows, k = a.shape
    n_cols = w.shape[1]
    tm = min(tm, m_rows)
    tn = min(tn, n_cols)
    assert m_rows % tm == 0 and n_cols % tn == 0
    return pl.pallas_call(
        _mm_kernel,
        grid=(n_cols // tn, m_rows // tm),
        in_specs=[pl.BlockSpec((tm, k), lambda j, i: (i, 0)),
                  pl.BlockSpec((k, tn), lambda j, i: (0, j))],
        out_specs=pl.BlockSpec((tm, tn), lambda j, i: (i, j)),
        out_shape=jax.ShapeDtypeStruct((m_rows, n_cols), out_dtype),
        compiler_params=_params(("parallel", "arbitrary")),
        name=name,
    )(a, w)


def _mm_residual(a, w, x, gate_tab, m_rows, *, tm=1024, tn=1024, name="mm_res"):
    k = a.shape[1]
    n = w.shape[1]
    grp = _group_of_block(tm)
    return pl.pallas_call(
        _mm_res_kernel,
        grid=(n // tn, m_rows // tm),
        in_specs=[pl.BlockSpec((tm, k), lambda j, i: (i, 0)),
                  pl.BlockSpec((k, tn), lambda j, i: (0, j)),
                  pl.BlockSpec((tm, tn), lambda j, i: (i, j)),
                  pl.BlockSpec((None, 1, tn), lambda j, i: (grp(i), 0, j))],
        out_specs=pl.BlockSpec((tm, tn), lambda j, i: (i, j)),
        out_shape=jax.ShapeDtypeStruct((m_rows, n), F32),
        compiler_params=_params(("parallel", "arbitrary")),
        name=name,
    )(a, w, x, gate_tab)


def _merge_kernel(h_ref, y0_ref, y1_ref, y2_ref, wg0_ref, wg1_ref, wg2_ref, wb0_ref, wb1_ref, wb2_ref, o_ref):
    h = h_ref[...]
    acc = jax.nn.sigmoid(_dot(h, wg0_ref[...])) * _dot(y0_ref[...], wb0_ref[...])
    acc = acc + jax.nn.sigmoid(_dot(h, wg1_ref[...])) * _dot(y1_ref[...], wb1_ref[...])
    acc = acc + jax.nn.sigmoid(_dot(h, wg2_ref[...])) * _dot(y2_ref[...], wb2_ref[...])
    o_ref[...] = acc.astype(o_ref.dtype)


def _merge(h, y_ml, y_la, y_ga, w_gate, w_br_ml, w_br_la, w_br_ga, m_rows, tm=512, tn=512):
    d = D_MODEL
    nb = d // tn
    y_spec = pl.BlockSpec((tm, ML_WIDTH), lambda j, i: (i, 0))
    wb_spec = pl.BlockSpec((ML_WIDTH, tn), lambda j, i: (0, j))
    return pl.pallas_call(
        _merge_kernel,
        grid=(nb, m_rows // tm),
        in_specs=[pl.BlockSpec((tm, d), lambda j, i: (i, 0)), y_spec, y_spec, y_spec,
                  pl.BlockSpec((d, tn), lambda j, i: (0, j)),
                  pl.BlockSpec((d, tn), lambda j, i: (0, nb + j)),
                  pl.BlockSpec((d, tn), lambda j, i: (0, 2 * nb + j)),
                  wb_spec, wb_spec, wb_spec],
        out_specs=pl.BlockSpec((tm, tn), lambda j, i: (i, j)),
        out_shape=jax.ShapeDtypeStruct((m_rows, d), BF16),
        compiler_params=_params(("parallel", "arbitrary")),
        name="merge",
    )(h, y_ml, y_la, y_ga, w_gate, w_gate, w_gate, w_br_ml, w_br_la, w_br_ga)


def _mlstm_kernel(ql_ref, kl_ref, vl_ref, ol_ref, qc_ref, kc_ref, vc_ref, oc_ref,
                  grl_ref, gcl_ref, grc_ref, gcc_ref, gain_ref, y_ref,
                  qb_ref, kt_ref, vb_ref, s_ref, hsum_ref, yall_ref):
    step = pl.program_id(2)
    length = ML_CHUNK
    hd = ML_HEAD_DIM
    n_lat = SEQ // length
    n_chunks = n_lat + 1
    scale = hd ** -0.5

    @pl.when(step == 0)
    def _():
        def load(c, ctx_ref, lat_ref):
            return ctx_ref[...] if c == 0 else lat_ref[(c - 1) * length:c * length, :]

        for c in range(n_chunks):
            qb = (load(c, qc_ref, ql_ref) * scale).astype(BF16)
            kt = jnp.transpose(load(c, kc_ref, kl_ref)).astype(BF16)
            qb_ref[c] = qb
            kt_ref[c] = kt
            vb_ref[c] = load(c, vc_ref, vl_ref).astype(BF16)
            s_ref[c] = _dot(qb, kt)

        ri = lax.broadcasted_iota(I32, (length, length), 0)
        ci = lax.broadcasted_iota(I32, (length, length), 1)
        for direction in range(2):
            reverse = direction == 1
            row_i = 2 * direction
            allowed = (ci >= ri) if reverse else (ci <= ri)
            allowed_t = (ri >= ci) if reverse else (ri <= ci)
            order = [0] + ([n_lat - j for j in range(n_lat)] if reverse else list(range(1, n_chunks)))
            m = jnp.zeros((1, 1), F32)
            ct = jnp.zeros((hd, hd), F32)
            n_vec = jnp.zeros((1, hd), F32)
            for c in order:
                g_rows = grc_ref[...] if c == 0 else grl_ref[c - 1]
                g_cols = gcc_ref[...] if c == 0 else gcl_ref[c - 1]
                li_r = g_rows[row_i:row_i + 1, :]
                lf_r = g_rows[row_i + 1:row_i + 2, :]
                li_c = g_cols[:, row_i:row_i + 1]
                lf_c = g_cols[:, row_i + 1:row_i + 2]
                cum_c = jnp.sum(jnp.where(allowed, lf_r, 0.0), axis=1, keepdims=True)
                cum_r = jnp.sum(jnp.where(allowed_t, lf_c, 0.0), axis=0, keepdims=True)
                total = jnp.sum(lf_r, axis=1, keepdims=True)
                dmat = jnp.where(allowed, cum_c - cum_r + li_r, -jnp.inf)
                m_loc = jnp.max(dmat, axis=1, keepdims=True)
                s_loc = s_ref[c] * jnp.exp(dmat - m_loc)
                intra = _dot(s_loc.astype(BF16), vb_ref[c])
                intra_sum = jnp.sum(s_loc, axis=1, keepdims=True)
                inter = cum_c + m
                m_row = jnp.maximum(inter, m_loc)
                f_loc = jnp.exp(m_loc - m_row)
                w_inter = jnp.exp(inter - m_row)
                q32 = load(c, qc_ref, ql_ref) * scale
                num = intra * f_loc + w_inter * _dot(qb_ref[c], ct.astype(BF16))
                den = intra_sum * f_loc + w_inter * jnp.sum(q32 * n_vec, axis=1, keepdims=True)
                h_out = num / jnp.maximum(jnp.abs(den), jnp.exp(-m_row))
                rows = slice(c * length, (c + 1) * length)
                if reverse:
                    hsum_ref[rows, :] += h_out
                else:
                    hsum_ref[rows, :] = h_out
                if c != order[-1]:
                    g_r = total - cum_r + li_r
                    g_c = total - cum_c + li_c
                    m_new = jnp.maximum(total + m, jnp.max(g_r, axis=1, keepdims=True))
                    wk_c = jnp.exp(g_c - m_new)
                    decay = jnp.exp(total + m - m_new)
                    wv = (wk_c * load(c, vc_ref, vl_ref)).astype(BF16)
                    ct = decay * ct + _dot(kt_ref[c], wv)
                    n_vec = decay * n_vec + jnp.sum(wk_c * load(c, kc_ref, kl_ref), axis=0, keepdims=True)
                    m = m_new

        gain = gain_ref[...]

        def finish(rows, o_pre):
            return (_rms(hsum_ref[rows, :]) * gain * jax.nn.sigmoid(o_pre)).astype(BF16)

        yall_ref[0:CTX_LEN, :] = finish(slice(0, CTX_LEN), oc_ref[...])
        yall_ref[CTX_LEN:, :] = finish(slice(CTX_LEN, CTX_LEN + SEQ), ol_ref[...])

    y_ref[...] = yall_ref[pl.ds(pl.multiple_of(step * length, length), length), :]


def _mlstm(p_ml, g_row_lat, g_col_lat, g_row_ctx, g_col_ctx, gain):
    assert CTX_LEN == ML_CHUNK
    hd = ML_HEAD_DIM
    length = ML_CHUNK
    n_lat = SEQ // length
    ctx0 = M_LAT // CTX_LEN

    def lat_spec(col):
        return pl.BlockSpec((SEQ, hd), lambda b, h, s: (b, col * ML_HEADS + h))

    def ctx_spec(col):
        return pl.BlockSpec((CTX_LEN, hd), lambda b, h, s: (ctx0 + b, col * ML_HEADS + h))

    def out_map(b, h, s):
        return (jnp.where(s == 0, ctx0 + b, b * n_lat + s - 1), h)

    return pl.pallas_call(
        _mlstm_kernel,
        grid=(BATCH, ML_HEADS, 1 + n_lat),
        in_specs=[lat_spec(0), lat_spec(1), lat_spec(2), lat_spec(3),
                  ctx_spec(0), ctx_spec(1), ctx_spec(2), ctx_spec(3),
                  pl.BlockSpec((None, None, n_lat, 4, length), lambda b, h, s: (b, h, 0, 0, 0)),
                  pl.BlockSpec((None, None, n_lat, length, 4), lambda b, h, s: (b, h, 0, 0, 0)),
                  pl.BlockSpec((None, None, 4, length), lambda b, h, s: (b, h, 0, 0)),
                  pl.BlockSpec((None, None, length, 4), lambda b, h, s: (b, h, 0, 0)),
                  pl.BlockSpec((1, hd), lambda b, h, s: (0, h))],
        out_specs=pl.BlockSpec((length, hd), out_map),
        out_shape=jax.ShapeDtypeStruct((M_ALL, ML_WIDTH), BF16),
        scratch_shapes=[pltpu.VMEM((1 + n_lat, length, hd), BF16),
                        pltpu.VMEM((1 + n_lat, hd, length), BF16),
                        pltpu.VMEM((1 + n_lat, length, hd), BF16),
                        pltpu.VMEM((1 + n_lat, length, length), F32),
                        pltpu.VMEM((CTX_LEN + SEQ, hd), F32),
                        pltpu.VMEM((CTX_LEN + SEQ, hd), BF16)],
        compiler_params=_params(("parallel", "parallel", "arbitrary")),
        name="mlstm",
    )(p_ml, p_ml, p_ml, p_ml, p_ml, p_ml, p_ml, p_ml, g_row_lat, g_col_lat, g_row_ctx, g_col_ctx, gain)


def _rope_partner(x, half):
    lane = lax.broadcasted_iota(I32, x.shape, 1)
    first = (lane & (2 * half - 1)) < half
    return jnp.where(first, pltpu.roll(x, LANES - half, 1), pltpu.roll(x, half, 1))


def _ga_prep_kernel(h_ref, w_ref, gq_ref, gk_ref, cos_ref, sin_ref, q_ref, k_ref, v_ref):
    p = _dot(h_ref[...], w_ref[...])
    cos = cos_ref[...]
    sin = sin_ref[...]
    dh = GA_HEAD_DIM

    def norm_rope(x, gain):
        y = _rms(x) * gain
        return y * cos + _rope_partner(y, dh // 4) * sin

    for h in range(GA_HEADS):
        q = norm_rope(p[:, h * dh:(h + 1) * dh], gq_ref[...])
        q_ref[h] = (q * dh ** -0.5).astype(BF16)
    for g in range(GA_KV_HEADS):
        k0 = GA_WIDTH + g * dh
        v0 = GA_WIDTH + GA_KV_WIDTH + g * dh
        k_ref[g] = norm_rope(p[:, k0:k0 + dh], gk_ref[...]).astype(BF16)
        v_ref[g] = p[:, v0:v0 + dh].astype(BF16)


def _ga_prep(h, w_ga, gain_q, gain_k, cos, sin, tm=512):
    d = h.shape[1]
    n = w_ga.shape[1]
    dh = GA_HEAD_DIM

    def head_spec(n_heads):
        return pl.BlockSpec((n_heads, tm, dh), lambda i: (0, i, 0))

    return pl.pallas_call(
        _ga_prep_kernel,
        grid=(M_ALL // tm,),
        in_specs=[pl.BlockSpec((tm, d), lambda i: (i, 0)),
                  pl.BlockSpec((d, n), lambda i: (0, 0)),
                  pl.BlockSpec((1, dh), lambda i: (0, 0)),
                  pl.BlockSpec((1, dh), lambda i: (0, 0)),
                  pl.BlockSpec((tm, dh), lambda i: (i, 0)),
                  pl.BlockSpec((tm, dh), lambda i: (i, 0))],
        out_specs=[head_spec(GA_HEADS), head_spec(GA_KV_HEADS), head_spec(GA_KV_HEADS)],
        out_shape=[jax.ShapeDtypeStruct((GA_HEADS, M_ALL, dh), BF16),
                   jax.ShapeDtypeStruct((GA_KV_HEADS, M_ALL, dh), BF16),
                   jax.ShapeDtypeStruct((GA_KV_HEADS, M_ALL, dh), BF16)],
        compiler_params=_params(("parallel",)),
        name="ga_prep",
    )(h, w_ga, gain_q, gain_k, cos, sin)


def _la_prep_kernel(h_ref, wsm_ref, gq_ref, gkv_ref, wuq_ref, wukv_ref, cos_ref, sin_ref,
                    q_ref, k_ref, v_ref, tail_ref):
    p = _dot(h_ref[...], wsm_ref[...])
    tail = p[:, SM_TAIL:SM_TAIL + LANES]
    tail_ref[...] = tail
    cq = (_rms(p[:, :LA_Q_RANK]) * gq_ref[...]).astype(BF16)
    ckv = (_rms(p[:, SM_CKV:SM_CKV + LA_KV_RANK]) * gkv_ref[...]).astype(BF16)
    qf = _dot(cq, wuq_ref[...])
    kvf = _dot(ckv, wukv_ref[...])
    cos = cos_ref[...]
    sin = sin_ref[...]
    scale = (LA_NOPE + LA_ROPE) ** -0.5
    low = lax.broadcasted_iota(I32, tail.shape, 1) < LA_ROPE

    def rope(x):
        return x * cos + _rope_partner(x, LA_ROPE // 4) * sin

    kr = rope(tail)
    kr2 = jnp.where(low, kr, pltpu.roll(kr, LA_ROPE, 1)).astype(BF16)
    heads_per_tile = LANES // LA_ROPE
    for j in range(LA_HEADS // heads_per_tile):
        c0 = LA_HEADS * LA_NOPE + j * LANES
        qr = rope(qf[:, c0:c0 + LANES]) * scale
        q_ref[2 * j, :, LA_NOPE:] = jnp.where(low, qr, 0.0).astype(BF16)
        q_ref[2 * j + 1, :, LA_NOPE:] = jnp.where(low, 0.0, qr).astype(BF16)
    for h in range(LA_HEADS):
        q_ref[h, :, :LA_NOPE] = (qf[:, h * LA_NOPE:(h + 1) * LA_NOPE] * scale).astype(BF16)
        kv0 = h * (LA_NOPE + LA_VDIM)
        k_ref[h, :, :LA_NOPE] = kvf[:, kv0:kv0 + LA_NOPE].astype(BF16)
        k_ref[h, :, LA_NOPE:] = kr2
        v_ref[h] = kvf[:, kv0 + LA_NOPE:kv0 + LA_NOPE + LA_VDIM].astype(BF16)


def _la_prep(h, w_small, gain_q, gain_kv, w_uq, w_ukv, cos, sin, tm=512):
    assert LA_HEADS % (LANES // LA_ROPE) == 0 and LA_NOPE == LANES and LA_DK == LA_NOPE + LANES
    d = h.shape[1]

    def full(a):
        return pl.BlockSpec(a.shape, lambda i: (0,) * a.ndim)

    return pl.pallas_call(
        _la_prep_kernel,
        grid=(M_ALL // tm,),
        in_specs=[pl.BlockSpec((tm, d), lambda i: (i, 0)), full(w_small), full(gain_q), full(gain_kv),
                  full(w_uq), full(w_ukv),
                  pl.BlockSpec((tm, LANES), lambda i: (i, 0)),
                  pl.BlockSpec((tm, LANES), lambda i: (i, 0))],
        out_specs=[pl.BlockSpec((LA_HEADS, tm, LA_DK), lambda i: (0, i, 0)),
                   pl.BlockSpec((LA_HEADS, tm, LA_DK), lambda i: (0, i, 0)),
                   pl.BlockSpec((LA_HEADS, tm, LA_VDIM), lambda i: (0, i, 0)),
                   pl.BlockSpec((tm, LANES), lambda i: (i, 0))],
        out_shape=[jax.ShapeDtypeStruct((LA_HEADS, M_ALL, LA_DK), BF16),
                   jax.ShapeDtypeStruct((LA_HEADS, M_ALL, LA_DK), BF16),
                   jax.ShapeDtypeStruct((LA_HEADS, M_ALL, LA_VDIM), BF16),
                   jax.ShapeDtypeStruct((M_ALL, LANES), F32)],
        compiler_params=_params(("parallel",)),
        name="la_prep",
    )(h, w_small, gain_q, gain_kv, w_uq, w_ukv, cos, sin)


ATT_TQ = 256
ATT_ROWS = 1024


def _attn_kernel(q_ref, kl_ref, vl_ref, kc_ref, vc_ref, kca_ref, vca_ref, o_ref, *, n_lat_blocks):
    i = pl.program_id(1)

    @pl.when(i < n_lat_blocks)
    def _():
        for sub in range(ATT_ROWS // ATT_TQ):
            rows = slice(sub * ATT_TQ, (sub + 1) * ATT_TQ)
            q = q_ref[rows, :]
            s_l = _dot_nt(q, kl_ref[...])
            s_c = _dot_nt(q, kc_ref[...])
            mx = jnp.maximum(jnp.max(s_l, axis=1, keepdims=True), jnp.max(s_c, axis=1, keepdims=True))
            p_l = jnp.exp(s_l - mx)
            p_c = jnp.exp(s_c - mx)
            den = jnp.sum(p_l, axis=1, keepdims=True) + jnp.sum(p_c, axis=1, keepdims=True)
            o = _dot(p_l.astype(BF16), vl_ref[...]) + _dot(p_c.astype(BF16), vc_ref[...])
            o_ref[rows, :] = (o / den).astype(o_ref.dtype)

    @pl.when(i >= n_lat_blocks)
    def _():
        for sub in range(ATT_ROWS // CTX_LEN):
            rows = slice(sub * CTX_LEN, (sub + 1) * CTX_LEN)
            s = _dot_nt(q_ref[rows, :], kca_ref[rows, :])
            mx = jnp.max(s, axis=1, keepdims=True)
            p = jnp.exp(s - mx)
            den = jnp.sum(p, axis=1, keepdims=True)
            o = _dot(p.astype(BF16), vca_ref[rows, :])
            o_ref[rows, :] = (o / den).astype(o_ref.dtype)


def _attention(q, k, v, kv_group, need_ctx, name):
    assert M_CTX == ATT_ROWS and CTX_LEN == ATT_TQ
    n_heads, _, dk = q.shape
    dv = v.shape[2]
    n_lat_blocks = M_LAT // ATT_ROWS
    per_sample = SEQ // ATT_ROWS
    ctx0 = M_LAT // CTX_LEN
    n_blocks = n_lat_blocks + (1 if need_ctx else 0)

    def sample(i):
        return jnp.minimum(i // per_sample, BATCH - 1)

    def lat_spec(dim):
        return pl.BlockSpec((None, SEQ, dim), lambda h, i: (h // kv_group, sample(i), 0))

    def ctx_spec(dim):
        return pl.BlockSpec((None, CTX_LEN, dim), lambda h, i: (h // kv_group, ctx0 + sample(i), 0))

    def ctx_all_spec(dim):
        return pl.BlockSpec((None, M_CTX, dim), lambda h, i: (h // kv_group, n_lat_blocks, 0))

    return pl.pallas_call(
        functools.partial(_attn_kernel, n_lat_blocks=n_lat_blocks),
        grid=(n_heads, n_blocks),
        in_specs=[pl.BlockSpec((None, ATT_ROWS, dk), lambda h, i: (h, i, 0)),
                  lat_spec(dk), lat_spec(dv), ctx_spec(dk), ctx_spec(dv), ctx_all_spec(dk), ctx_all_spec(dv)],
        out_specs=pl.BlockSpec((ATT_ROWS, dv), lambda h, i: (i, h)),
        out_shape=jax.ShapeDtypeStruct((n_blocks * ATT_ROWS, n_heads * dv), BF16),
        compiler_params=_params(("parallel", "arbitrary")),
        name=name,
    )(q, k, v, k, v, k, v)


def _router_kernel(h_ref, w_ref, o_ref):
    logits = _dot_nt(w_ref[...], h_ref[...])
    mx = jnp.max(logits, axis=0, keepdims=True)
    e = jnp.exp(logits - mx)
    o_ref[...] = e / jnp.sum(e, axis=0, keepdims=True)


def _router(h, w_router_t, m_rows, tm=1024):
    d = h.shape[1]
    return pl.pallas_call(
        _router_kernel,
        grid=(m_rows // tm,),
        in_specs=[pl.BlockSpec((tm, d), lambda i: (i, 0)),
                  pl.BlockSpec((N_EXPERTS, d), lambda i: (0, 0))],
        out_specs=pl.BlockSpec((N_EXPERTS, tm), lambda i: (0, i)),
        out_shape=jax.ShapeDtypeStruct((N_EXPERTS, m_rows), F32),
        compiler_params=_params(("parallel",)),
        name="router",
    )(h, w_router_t)


RANK_BLOCK = 256


def _topk_kernel(aff_ref, rank_ref, *, cap):
    a_row = aff_ref[...]
    n_tok = a_row.shape[1]
    rb = RANK_BLOCK
    n_blk = n_tok // rb
    ri = lax.broadcasted_iota(I32, (rb, rb), 0)
    ci = lax.broadcasted_iota(I32, (rb, rb), 1)
    earlier = ri < ci
    counts = [jnp.zeros((rb, rb), F32) for _ in range(n_blk)]
    for blk in range(n_blk):
        rival = jnp.transpose(jnp.broadcast_to(a_row[:, blk * rb:(blk + 1) * rb], (rb, rb)))
        for c in range(n_blk):
            tok = a_row[:, c * rb:(c + 1) * rb]
            if c < blk:
                beats = rival > tok
            elif c > blk:
                beats = rival >= tok
            else:
                beats = (rival > tok) | ((rival == tok) & earlier)
            counts[c] = counts[c] + jnp.where(beats, 1.0, 0.0)
    upper = jnp.where(earlier, 1.0, 0.0).astype(BF16)
    carry = jnp.zeros((1, 1), F32)
    for c in range(n_blk):
        sel = jnp.sum(counts[c], axis=0, keepdims=True) < cap
        sel_f = jnp.where(sel, 1.0, 0.0)
        before = _dot(jnp.broadcast_to(sel_f, (16, rb)).astype(BF16), upper)[0:1, :] + carry
        carry = carry + jnp.sum(sel_f, axis=1, keepdims=True)
        rank_ref[:, c * rb:(c + 1) * rb] = jnp.where(sel, before.astype(I32), -1)


def _topk(aff3, n_tok, cap, blk0, n_sets):
    return pl.pallas_call(
        functools.partial(_topk_kernel, cap=cap),
        grid=(n_sets, N_EXPERTS),
        in_specs=[pl.BlockSpec((None, 1, n_tok), lambda s, e: (e, 0, blk0 + s))],
        out_specs=pl.BlockSpec((None, 1, n_tok), lambda s, e: (e, 0, s)),
        out_shape=jax.ShapeDtypeStruct((N_EXPERTS, 1, n_sets * n_tok), I32),
        compiler_params=_params(("parallel", "parallel")),
        name=f"topk_{n_tok}",
    )(aff3)


def _gather_kernel(rank_ref, aff_ref, h_ref, *rest, cap):
    xg_ref, val_ref = rest[-2:]
    r = rank_ref[...]
    n_tok = r.shape[1]
    slot = lax.broadcasted_iota(I32, (cap, n_tok), 0)
    hit = r == slot
    onehot = jnp.where(hit, 1.0, 0.0).astype(BF16)
    xg_ref[...] = _dot(onehot, h_ref[...]).astype(xg_ref.dtype)
    val_ref[...] = jnp.sum(jnp.where(hit, aff_ref[...], 0.0), axis=1, keepdims=True)


def _gather(rank3, aff3, h, n_tok, cap, blk0, row0, rows_total, prev):
    d = h.shape[1]
    rb0 = row0 // cap
    in_specs = [pl.BlockSpec((None, 1, n_tok), lambda b, e: (e, 0, b)),
                pl.BlockSpec((None, 1, n_tok), lambda b, e: (e, 0, blk0 + b)),
                pl.BlockSpec((n_tok, d), lambda b, e: (blk0 + b, 0))]
    args = [rank3, aff3, h]
    aliases = {}
    if prev is not None:
        in_specs += [pl.BlockSpec(memory_space=pl.ANY), pl.BlockSpec(memory_space=pl.ANY)]
        args += list(prev)
        aliases = {3: 0, 4: 1}
    return pl.pallas_call(
        functools.partial(_gather_kernel, cap=cap),
        grid=(BATCH, N_EXPERTS),
        in_specs=in_specs,
        out_specs=[pl.BlockSpec((None, cap, d), lambda b, e: (e, rb0 + b, 0)),
                   pl.BlockSpec((None, cap, 1), lambda b, e: (e, rb0 + b, 0))],
        out_shape=[jax.ShapeDtypeStruct((N_EXPERTS, rows_total, d), BF16),
                   jax.ShapeDtypeStruct((N_EXPERTS, rows_total, 1), F32)],
        input_output_aliases=aliases,
        compiler_params=_params(("parallel", "arbitrary")),
        name=f"moe_gather_{n_tok}",
    )(*args)


EXPERT_COLS = 512


def _expert_kernel(x_ref, w1_ref, w3_ref, w2_ref, val_ref, y_ref, acc_ref):
    f = pl.program_id(1)
    last = pl.num_programs(1) - 1
    col_slices = [slice(c0, c0 + EXPERT_COLS) for c0 in range(0, acc_ref.shape[1], EXPERT_COLS)]

    def hidden():
        x = x_ref[...]
        a = _dot(x, w1_ref[...].astype(BF16))
        u = _dot(x, w3_ref[...].astype(BF16))
        return (a * jax.nn.sigmoid(a) * u).astype(BF16)

    def down(hm, cols):
        return _dot(hm, w2_ref[:, cols].astype(BF16))

    @pl.when(f == 0)
    def _():
        hm = hidden()
        for cols in col_slices:
            acc_ref[:, cols] = down(hm, cols)

    @pl.when((f > 0) & (f < last))
    def _():
        hm = hidden()
        for cols in col_slices:
            acc_ref[:, cols] += down(hm, cols)

    @pl.when(f == last)
    def _():
        hm = hidden()
        val = val_ref[...]
        for cols in col_slices:
            y_ref[:, cols] = ((acc_ref[:, cols] + down(hm, cols)) * val).astype(y_ref.dtype)


def _experts(xg, vals, w_e1, w_e3, w_e2, layer, tf=256):
    n_exp, rows, d = xg.shape
    ff = w_e1.shape[3]
    assert ff // tf >= 2
    return pl.pallas_call(
        _expert_kernel,
        grid=(n_exp, ff // tf),
        in_specs=[pl.BlockSpec((None, rows, d), lambda e, f: (e, 0, 0)),
                  pl.BlockSpec((None, None, d, tf), lambda e, f: (layer, e, 0, f)),
                  pl.BlockSpec((None, None, d, tf), lambda e, f: (layer, e, 0, f)),
                  pl.BlockSpec((None, None, tf, d), lambda e, f: (layer, e, f, 0)),
                  pl.BlockSpec((None, rows, 1), lambda e, f: (e, 0, 0))],
        out_specs=pl.BlockSpec((None, rows, d), lambda e, f: (e, 0, 0)),
        out_shape=jax.ShapeDtypeStruct((n_exp, rows, d), BF16),
        scratch_shapes=[pltpu.VMEM((rows, d), F32)],
        compiler_params=_params(("parallel", "arbitrary")),
        name="moe_experts",
    )(xg, w_e1, w_e3, w_e2, vals)


COMBINE_LANES = 1024


def _combine_kernel(rank_ref, y_ref, x_ref, g_ref, *rest, cap):
    o_ref, lhs_ref = rest[-2:]
    n_slots = N_EXPERTS * cap
    chunk = min(COMBINE_LANES, n_slots)
    shift = cap.bit_length() - 1

    @pl.when(pl.program_id(2) == 0)
    def _():
        r = rank_ref[...].astype(F32).astype(BF16)
        tm = r.shape[0]
        for c0 in range(0, n_slots, chunk):
            lane = c0 + lax.broadcasted_iota(I32, (LANES, chunk), 1)
            row = lax.broadcasted_iota(I32, (LANES, chunk), 0)
            expand = jnp.where(lax.shift_right_logical(lane, shift) == row, 1.0, 0.0).astype(BF16)
            r_exp = _dot(r, expand)
            slot = (c0 + lax.broadcasted_iota(I32, (tm, chunk), 1)) & (cap - 1)
            lhs_ref[:, c0:c0 + chunk] = jnp.where(r_exp == slot.astype(F32), 1.0, 0.0).astype(BF16)

    y = y_ref[...]
    o_ref[...] = x_ref[...] + g_ref[...] * _dot(lhs_ref[...], y.reshape(n_slots, y.shape[2]))


def _combine(rank_t, y, x, gate_tab, out_prev, n_tok, cap, blk0, row0, tm, tn=1024):
    assert cap & (cap - 1) == 0
    d = x.shape[1]
    tpb = n_tok // tm
    xrow0 = blk0 * tpb
    rb0 = row0 // cap

    def grp(b):
        return b if n_tok == SEQ else BATCH

    in_specs = [pl.BlockSpec((tm, LANES), lambda b, i, j: (b * tpb + i, 0)),
                pl.BlockSpec((N_EXPERTS, cap, tn), lambda b, i, j: (0, rb0 + b, j)),
                pl.BlockSpec((tm, tn), lambda b, i, j: (xrow0 + b * tpb + i, j)),
                pl.BlockSpec((None, 1, tn), lambda b, i, j: (grp(b), 0, j))]
    args = [rank_t, y, x, gate_tab]
    aliases = {}
    if out_prev is not None:
        in_specs.append(pl.BlockSpec(memory_space=pl.ANY))
        args.append(out_prev)
        aliases = {4: 0}
    return pl.pallas_call(
        functools.partial(_combine_kernel, cap=cap),
        grid=(BATCH, tpb, d // tn),
        in_specs=in_specs,
        out_specs=pl.BlockSpec((tm, tn), lambda b, i, j: (xrow0 + b * tpb + i, j)),
        out_shape=jax.ShapeDtypeStruct(x.shape, F32),
        scratch_shapes=[pltpu.VMEM((tm, N_EXPERTS * cap), BF16)],
        input_output_aliases=aliases,
        compiler_params=_params(("parallel", "parallel", "arbitrary")),
        name=f"moe_combine_{n_tok}",
    )(*args)


def _expert_choice(h2, x, gate_tab, w_router_t, w_e1, w_e3, w_e2, layer, need_ctx):
    m_rows = h2.shape[0]
    aff3 = _router(h2, w_router_t, m_rows).reshape(N_EXPERTS, 1, m_rows)
    sets = [(SEQ, 0)]
    if need_ctx:
        sets.append((CTX_LEN, M_LAT // CTX_LEN))
    caps = [CAPACITY_FACTOR * n_tok // N_EXPERTS for n_tok, _ in sets]
    rows_total = sum(BATCH * cap for cap in caps)
    ranks, gathered, row0 = [], None, 0
    for (n_tok, blk0), cap in zip(sets, caps):
        rank3 = _topk(aff3, n_tok, cap, blk0, BATCH)
        gathered = _gather(rank3, aff3, h2, n_tok, cap, blk0, row0, rows_total, gathered)
        ranks.append((rank3, row0))
        row0 += BATCH * cap
    y = _experts(gathered[0], gathered[1], w_e1, w_e3, w_e2, layer)
    out = None
    for (n_tok, blk0), cap, (rank3, row0) in zip(sets, caps, ranks):
        rank_t = jnp.pad(rank3.reshape(N_EXPERTS, BATCH * n_tok).T, ((0, 0), (0, LANES - N_EXPERTS)))
        out = _combine(rank_t, y, x, gate_tab, out, n_tok, cap, blk0, row0, tm=min(512, n_tok))
    return out


def _rope_tables(dim):
    rows = SEQ // GRID_W
    quarter = dim // 4
    inv = ROPE_BASE ** (-jnp.arange(quarter, dtype=F32) / quarter)
    row = jnp.broadcast_to(jnp.arange(rows, dtype=F32)[:, None], (rows, GRID_W)).reshape(-1)
    col = jnp.broadcast_to(jnp.arange(GRID_W, dtype=F32)[None, :], (rows, GRID_W)).reshape(-1)
    ang_r = row[:, None] * inv[None, :]
    ang_c = col[:, None] * inv[None, :]
    cos = jnp.concatenate([jnp.cos(ang_r), jnp.cos(ang_r), jnp.cos(ang_c), jnp.cos(ang_c)], axis=-1)
    sin = jnp.concatenate([-jnp.sin(ang_r), jnp.sin(ang_r), -jnp.sin(ang_c), jnp.sin(ang_c)], axis=-1)
    cos = jnp.tile(cos, (BATCH, LANES // dim))
    sin = jnp.tile(sin, (BATCH, LANES // dim))
    cos = jnp.concatenate([cos, jnp.ones((M_CTX, LANES), F32)], axis=0)
    sin = jnp.concatenate([sin, jnp.zeros((M_CTX, LANES), F32)], axis=0)
    return cos, sin


def _layer(x, mods, layer, w_in, ml_gate_bias, ml_norm, la_q_norm, la_kv_norm, la_w_uq, la_w_ukv,
           ga_q_norm, ga_k_norm, w_br_ml, w_br_la, w_br_ga, w_out, w_router, w_e1, w_e3, w_e2, ropes, need_ctx):
    d = D_MODEL
    m_rows = M_ALL if need_ctx else M_LAT
    sh1, sc1, g1, sh2, sc2, g2 = [mods[:N_GROUPS, k * d:(k + 1) * d].reshape(N_GROUPS, 1, d) for k in range(6)]
    (cos_la, sin_la), (cos_ga, sin_ga) = ropes

    w_ml = w_in[:, :OFF_MLG].astype(BF16)
    w_small = jnp.concatenate([w_in[:, OFF_CQ:OFF_GAQ], w_in[:, OFF_MLG:OFF_CQ],
                               jnp.zeros((d, SMALL_WIDTH - (OFF_GAQ - OFF_MLG)), F32)], axis=1).astype(BF16)
    w_ga = w_in[:, OFF_GAQ:OFF_GATE].astype(BF16)
    w_gate = w_in[:, OFF_GATE:].astype(BF16)
    w_uq = la_w_uq.reshape(LA_Q_RANK, LA_HEADS, LA_NOPE + LA_ROPE)
    w_uq = jnp.concatenate([w_uq[:, :, :LA_NOPE].reshape(LA_Q_RANK, LA_HEADS * LA_NOPE),
                            w_uq[:, :, LA_NOPE:].reshape(LA_Q_RANK, LA_HEADS * LA_ROPE)], axis=1).astype(BF16)

    h = _rms_mod(x, 1.0 + sc1, sh1, M_ALL)
    p_ml = _mm(h, w_ml, name="proj_ml")
    q_la, k_la, v_la, tail = _la_prep(h, w_small, la_q_norm.reshape(1, LA_Q_RANK), la_kv_norm.reshape(1, LA_KV_RANK),
                                      w_uq, la_w_ukv.astype(BF16), cos_la, sin_la)
    q_ga, k_ga, v_ga = _ga_prep(h, w_ga, ga_q_norm.reshape(1, GA_HEAD_DIM), ga_k_norm.reshape(1, GA_HEAD_DIM),
                                cos_ga, sin_ga)

    gates = (tail[:, TAIL_MLG:TAIL_MLG + ML_GATES] + ml_gate_bias[None, :]).reshape(M_ALL, 2, 2, ML_HEADS)
    gates = jnp.stack([gates[:, 0, 0], jax.nn.log_sigmoid(gates[:, 0, 1]),
                       gates[:, 1, 0], jax.nn.log_sigmoid(gates[:, 1, 1])], axis=-1)
    n_lat = SEQ // ML_CHUNK
    g_lat = gates[:M_LAT].reshape(BATCH, n_lat, ML_CHUNK, ML_HEADS, 4)
    g_ctx = gates[M_LAT:].reshape(BATCH, CTX_LEN, ML_HEADS, 4)
    y_ml = _mlstm(p_ml, jnp.transpose(g_lat, (0, 3, 1, 4, 2)), jnp.transpose(g_lat, (0, 3, 1, 2, 4)),
                  jnp.transpose(g_ctx, (0, 2, 3, 1)), jnp.transpose(g_ctx, (0, 2, 1, 3)),
                  ml_norm.reshape(1, ML_WIDTH))

    y_la = _attention(q_la, k_la, v_la, 1, need_ctx, "attn_mla")
    y_ga = _attention(q_ga, k_ga, v_ga, GA_HEADS // GA_KV_HEADS, need_ctx, "attn_gqa")

    merged = _merge(h, y_ml, y_la, y_ga, w_gate, w_br_ml.astype(BF16), w_br_la.astype(BF16), w_br_ga.astype(BF16),
                    m_rows)
    x = _mm_residual(merged, w_out.astype(BF16), x, g1, m_rows, name="out_proj")

    h2 = _rms_mod(x, 1.0 + sc2, sh2, m_rows)
    return _expert_choice(h2, x, g2, w_router.T.astype(BF16), w_e1, w_e3, w_e2, layer, need_ctx)


def kernel(x, c, ctx, c_ctx, w_mod, b_mod, w_in, ml_gate_bias, ml_norm, la_q_norm, la_kv_norm, la_w_uq, la_w_ukv,
           ga_q_norm, ga_k_norm, w_br_ml, w_br_la, w_br_ga, w_out, w_router, w_e1, w_e3, w_e2, final_norm):
    d = D_MODEL
    xs = jnp.concatenate([x.reshape(M_LAT, d), ctx.reshape(M_CTX, d)], axis=0)
    cc = jnp.concatenate([c, c_ctx[None, :], jnp.zeros((MODS_ROWS - N_GROUPS, d), F32)], axis=0)
    mods = _mods(cc, w_mod, b_mod.reshape(DEPTH, 1, 6 * d))
    ropes = (_rope_tables(LA_ROPE), _rope_tables(GA_HEAD_DIM))
    for l in range(DEPTH):
        need_ctx = l < DEPTH - 1
        xs = _layer(xs, mods[l], l, w_in[l], ml_gate_bias[l], ml_norm[l], la_q_norm[l], la_kv_norm[l],
                    la_w_uq[l], la_w_ukv[l], ga_q_norm[l], ga_k_norm[l], w_br_ml[l], w_br_la[l], w_br_ga[l],
                    w_out[l], w_router[l], w_e1, w_e3, w_e2, ropes, need_ctx)
    out = _rms_gain(xs, final_norm.reshape(1, d), F32, name="final_norm")
    return out.reshape(BATCH, SEQ, d)
```

```python
import functools

import jax
import jax.numpy as jnp
from jax import lax
from jax.experimental import pallas as pl
from jax.experimental.pallas import tpu as pltpu

F32 = jnp.float32
BF16 = jnp.bfloat16
I32 = jnp.int32

D_MODEL = 2048
BATCH = 4
SEQ = 2048
DEPTH = 2
GRID_W = 64
CTX_LEN = 256
EPS = 1e-6
ROPE_BASE = 10000.0

ML_HEADS = 4
ML_HEAD_DIM = 256
ML_WIDTH = ML_HEADS * ML_HEAD_DIM
ML_GATES = 2 * 2 * ML_HEADS
ML_CHUNK = 256

LA_HEADS = 8
LA_NOPE = 128
LA_ROPE = 64
LA_VDIM = 128
LA_Q_RANK = 512
LA_KV_RANK = 256
LA_DK = 256

GA_HEADS = 8
GA_KV_HEADS = 2
GA_HEAD_DIM = 128
GA_WIDTH = GA_HEADS * GA_HEAD_DIM
GA_KV_WIDTH = GA_KV_HEADS * GA_HEAD_DIM

N_EXPERTS = 16
EXPERT_FF = 1024
CAPACITY_FACTOR = 2

M_LAT = BATCH * SEQ
M_CTX = BATCH * CTX_LEN
M_ALL = M_LAT + M_CTX
MODS_ROWS = 16
N_GROUPS = BATCH + 1

LANES = 128

OFF_MLG = 4 * ML_WIDTH
OFF_CQ = OFF_MLG + ML_GATES
OFF_CKV = OFF_CQ + LA_Q_RANK
OFF_KR = OFF_CKV + LA_KV_RANK
OFF_GAQ = OFF_KR + LA_ROPE
OFF_GAK = OFF_GAQ + GA_WIDTH
OFF_GAV = OFF_GAK + GA_KV_WIDTH
OFF_GATE = OFF_GAV + GA_KV_WIDTH
SM_CKV = LA_Q_RANK
SM_TAIL = SM_CKV + LA_KV_RANK
SMALL_WIDTH = SM_TAIL + LANES
TAIL_MLG = LA_ROPE

VMEM_LIMIT = 56 * 1024 * 1024


def _params(semantics, vmem=VMEM_LIMIT):
    return pltpu.CompilerParams(dimension_semantics=semantics, vmem_limit_bytes=vmem)


def _dot(a, b):
    return jnp.dot(a, b, preferred_element_type=F32)


def _dot_nt(a, b):
    return lax.dot_general(a, b, (((1,), (1,)), ((), ())), preferred_element_type=F32)


def _rms(x):
    return x * lax.rsqrt(jnp.mean(x * x, axis=-1, keepdims=True) + EPS)


def _group_of_block(tm):
    return lambda i: (i * tm) // SEQ


def _mods_kernel(c_ref, w_ref, b_ref, o_ref):
    c = c_ref[...]
    a = (c * jax.nn.sigmoid(c)).astype(BF16)
    o_ref[...] = _dot(a, w_ref[...].astype(BF16)) + b_ref[...]


def _mods(cc, w_mod, b_mod):
    depth, d, n = w_mod.shape
    rows = cc.shape[0]
    tn = 1024
    return pl.pallas_call(
        _mods_kernel,
        grid=(depth, n // tn),
        in_specs=[pl.BlockSpec((rows, d), lambda l, j: (0, 0)),
                  pl.BlockSpec((None, d, tn), lambda l, j: (l, 0, j)),
                  pl.BlockSpec((None, 1, tn), lambda l, j: (l, 0, j))],
        out_specs=pl.BlockSpec((None, rows, tn), lambda l, j: (l, 0, j)),
        out_shape=jax.ShapeDtypeStruct((depth, rows, n), F32),
        compiler_params=_params(("parallel", "parallel")),
        name="mods",
    )(cc, w_mod, b_mod)


def _rms_kernel(x_ref, scale_ref, shift_ref, o_ref):
    o_ref[...] = (_rms(x_ref[...]) * scale_ref[...] + shift_ref[...]).astype(o_ref.dtype)


def _rms_gain_kernel(x_ref, scale_ref, o_ref):
    o_ref[...] = (_rms(x_ref[...]) * scale_ref[...]).astype(o_ref.dtype)


def _rms_mod(x, scale_tab, shift_tab, m_rows, tm=512):
    d = x.shape[1]
    grp = _group_of_block(tm)
    return pl.pallas_call(
        _rms_kernel,
        grid=(m_rows // tm,),
        in_specs=[pl.BlockSpec((tm, d), lambda i: (i, 0)),
                  pl.BlockSpec((None, 1, d), lambda i: (grp(i), 0, 0)),
                  pl.BlockSpec((None, 1, d), lambda i: (grp(i), 0, 0))],
        out_specs=pl.BlockSpec((tm, d), lambda i: (i, 0)),
        out_shape=jax.ShapeDtypeStruct((m_rows, d), BF16),
        compiler_params=_params(("parallel",)),
        name="rms_mod",
    )(x, scale_tab, shift_tab)


def _rms_gain(x, gain, out_dtype, tm=512, name="rms_gain"):
    m_rows, width = x.shape
    return pl.pallas_call(
        _rms_gain_kernel,
        grid=(m_rows // tm,),
        in_specs=[pl.BlockSpec((tm, width), lambda i: (i, 0)),
                  pl.BlockSpec((1, width), lambda i: (0, 0))],
        out_specs=pl.BlockSpec((tm, width), lambda i: (i, 0)),
        out_shape=jax.ShapeDtypeStruct((m_rows, width), out_dtype),
        compiler_params=_params(("parallel",)),
        name=name,
    )(x, gain)


def _mm_kernel(a_ref, w_ref, o_ref):
    o_ref[...] = _dot(a_ref[...], w_ref[...]).astype(o_ref.dtype)


def _mm_res_kernel(a_ref, w_ref, x_ref, g_ref, o_ref):
    o_ref[...] = x_ref[...] + g_ref[...] * _dot(a_ref[...], w_ref[...])


def _mm(a, w, *, tm=1024, tn=1024, out_dtype=F32, name="mm"):
    m_rows, k = a.shape
    n_cols = w.shape[1]
    tm = min(tm, m_rows)
    tn = min(tn, n_cols)
    assert m_rows % tm == 0 and n_cols % tn == 0
    return pl.pallas_call(
        _mm_kernel,
        grid=(n_cols // tn, m_rows // tm),
        in_specs=[pl.BlockSpec((tm, k), lambda j, i: (i, 0)),
                  pl.BlockSpec((k, tn), lambda j, i: (0, j))],
        out_specs=pl.BlockSpec((tm, tn), lambda j, i: (i, j)),
        out_shape=jax.ShapeDtypeStruct((m_rows, n_cols), out_dtype),
        compiler_params=_params(("parallel", "arbitrary")),
        name=name,
    )(a, w)


def _mm_residual(a, w, x, gate_tab, m_rows, *, tm=1024, tn=1024, name="mm_res"):
    k = a.shape[1]
    n = w.shape[1]
    grp = _group_of_block(tm)
    return pl.pallas_call(
        _mm_res_kernel,
        grid=(n // tn, m_rows // tm),
        in_specs=[pl.BlockSpec((tm, k), lambda j, i: (i, 0)),
                  pl.BlockSpec((k, tn), lambda j, i: (0, j)),
                  pl.BlockSpec((tm, tn), lambda j, i: (i, j)),
                  pl.BlockSpec((None, 1, tn), lambda j, i: (grp(i), 0, j))],
        out_specs=pl.BlockSpec((tm, tn), lambda j, i: (i, j)),
        out_shape=jax.ShapeDtypeStruct((m_rows, n), F32),
        compiler_params=_params(("parallel", "arbitrary")),
        name=name,
    )(a, w, x, gate_tab)


def _merge_kernel(h_ref, y0l_ref, y0c_ref, y1_ref, y2_ref, wg0_ref, wg1_ref, wg2_ref, wb0_ref, wb1_ref, wb2_ref,
                  o_ref, *, n_lat_tiles):
    h = h_ref[...]
    y0 = jnp.where(pl.program_id(1) < n_lat_tiles, y0l_ref[...], y0c_ref[...])
    acc = jax.nn.sigmoid(_dot(h, wg0_ref[...])) * _dot(y0, wb0_ref[...])
    acc = acc + jax.nn.sigmoid(_dot(h, wg1_ref[...])) * _dot(y1_ref[...], wb1_ref[...])
    acc = acc + jax.nn.sigmoid(_dot(h, wg2_ref[...])) * _dot(y2_ref[...], wb2_ref[...])
    o_ref[...] = acc.astype(o_ref.dtype)


def _merge(h, y_ml_lat, y_ml_ctx, y_la, y_ga, w_gate, w_br_ml, w_br_la, w_br_ga, m_rows, tm=512, tn=512):
    d = D_MODEL
    nb = d // tn
    n_lat_tiles = M_LAT // tm
    y_spec = pl.BlockSpec((tm, ML_WIDTH), lambda j, i: (i, 0))
    wb_spec = pl.BlockSpec((ML_WIDTH, tn), lambda j, i: (0, j))
    return pl.pallas_call(
        functools.partial(_merge_kernel, n_lat_tiles=n_lat_tiles),
        grid=(nb, m_rows // tm),
        in_specs=[pl.BlockSpec((tm, d), lambda j, i: (i, 0)),
                  pl.BlockSpec((tm, ML_WIDTH), lambda j, i: (jnp.minimum(i, n_lat_tiles - 1), 0)),
                  pl.BlockSpec((tm, ML_WIDTH), lambda j, i: (jnp.maximum(i - n_lat_tiles, 0), 0)),
                  y_spec, y_spec,
                  pl.BlockSpec((d, tn), lambda j, i: (0, j)),
                  pl.BlockSpec((d, tn), lambda j, i: (0, nb + j)),
                  pl.BlockSpec((d, tn), lambda j, i: (0, 2 * nb + j)),
                  wb_spec, wb_spec, wb_spec],
        out_specs=pl.BlockSpec((tm, tn), lambda j, i: (i, j)),
        out_shape=jax.ShapeDtypeStruct((m_rows, d), BF16),
        compiler_params=_params(("parallel", "arbitrary")),
        name="merge",
    )(h, y_ml_lat, y_ml_ctx, y_la, y_ga, w_gate, w_gate, w_gate, w_br_ml, w_br_la, w_br_ga)


def _mlstm_kernel(ql_ref, kl_ref, vl_ref, ol_ref, qc_ref, kc_ref, vc_ref, oc_ref,
                  grl_ref, gcl_ref, grc_ref, gcc_ref, gain_ref, yl_ref, yc_ref,
                  qb_ref, kt_ref, vb_ref, s_ref, hsum_ref):
    length = ML_CHUNK
    hd = ML_HEAD_DIM
    n_lat = SEQ // length
    n_chunks = n_lat + 1
    scale = hd ** -0.5

    def load(c, ctx_ref, lat_ref):
        return ctx_ref[...] if c == 0 else lat_ref[(c - 1) * length:c * length, :]

    for c in range(n_chunks):
        qb = (load(c, qc_ref, ql_ref) * scale).astype(BF16)
        kt = jnp.transpose(load(c, kc_ref, kl_ref)).astype(BF16)
        qb_ref[c] = qb
        kt_ref[c] = kt
        vb_ref[c] = load(c, vc_ref, vl_ref).astype(BF16)
        s_ref[c] = _dot(qb, kt)

    ri = lax.broadcasted_iota(I32, (length, length), 0)
    ci = lax.broadcasted_iota(I32, (length, length), 1)
    for direction in range(2):
        reverse = direction == 1
        row_i = 2 * direction
        allowed = (ci >= ri) if reverse else (ci <= ri)
        allowed_t = (ri >= ci) if reverse else (ri <= ci)
        order = [0] + ([n_lat - j for j in range(n_lat)] if reverse else list(range(1, n_chunks)))
        m = jnp.zeros((1, 1), F32)
        ct = jnp.zeros((hd, hd), F32)
        n_vec = jnp.zeros((1, hd), F32)
        for c in order:
            g_rows = grc_ref[...] if c == 0 else grl_ref[c - 1]
            g_cols = gcc_ref[...] if c == 0 else gcl_ref[c - 1]
            li_r = g_rows[row_i:row_i + 1, :]
            lf_r = g_rows[row_i + 1:row_i + 2, :]
            li_c = g_cols[:, row_i:row_i + 1]
            lf_c = g_cols[:, row_i + 1:row_i + 2]
            cum_c = jnp.sum(jnp.where(allowed, lf_r, 0.0), axis=1, keepdims=True)
            cum_r = jnp.sum(jnp.where(allowed_t, lf_c, 0.0), axis=0, keepdims=True)
            total = jnp.sum(lf_r, axis=1, keepdims=True)
            dmat = jnp.where(allowed, cum_c - cum_r + li_r, -jnp.inf)
            m_loc = jnp.max(dmat, axis=1, keepdims=True)
            s_loc = s_ref[c] * jnp.exp(dmat - m_loc)
            intra = _dot(s_loc.astype(BF16), vb_ref[c])
            intra_sum = jnp.sum(s_loc, axis=1, keepdims=True)
            inter = cum_c + m
            m_row = jnp.maximum(inter, m_loc)
            f_loc = jnp.exp(m_loc - m_row)
            w_inter = jnp.exp(inter - m_row)
            q32 = load(c, qc_ref, ql_ref) * scale
            num = intra * f_loc + w_inter * _dot(qb_ref[c], ct.astype(BF16))
            den = intra_sum * f_loc + w_inter * jnp.sum(q32 * n_vec, axis=1, keepdims=True)
            h_out = num / jnp.maximum(jnp.abs(den), jnp.exp(-m_row))
            rows = slice(c * length, (c + 1) * length)
            if reverse:
                hsum_ref[rows, :] += h_out
            else:
                hsum_ref[rows, :] = h_out
            if c != order[-1]:
                g_r = total - cum_r + li_r
                g_c = total - cum_c + li_c
                m_new = jnp.maximum(total + m, jnp.max(g_r, axis=1, keepdims=True))
                wk_c = jnp.exp(g_c - m_new)
                decay = jnp.exp(total + m - m_new)
                wv = (wk_c * load(c, vc_ref, vl_ref)).astype(BF16)
                ct = decay * ct + _dot(kt_ref[c], wv)
                n_vec = decay * n_vec + jnp.sum(wk_c * load(c, kc_ref, kl_ref), axis=0, keepdims=True)
                m = m_new

    gain = gain_ref[...]

    def finish(rows, o_pre):
        return (_rms(hsum_ref[rows, :]) * gain * jax.nn.sigmoid(o_pre)).astype(BF16)

    yc_ref[...] = finish(slice(0, CTX_LEN), oc_ref[...])
    yl_ref[...] = finish(slice(CTX_LEN, CTX_LEN + SEQ), ol_ref[...])


def _mlstm(p_ml, g_row_lat, g_col_lat, g_row_ctx, g_col_ctx, gain):
    assert CTX_LEN == ML_CHUNK
    hd = ML_HEAD_DIM
    length = ML_CHUNK
    n_lat = SEQ // length
    ctx0 = M_LAT // CTX_LEN

    def lat_spec(col):
        return pl.BlockSpec((SEQ, hd), lambda b, h: (b, col * ML_HEADS + h))

    def ctx_spec(col):
        return pl.BlockSpec((CTX_LEN, hd), lambda b, h: (ctx0 + b, col * ML_HEADS + h))

    return pl.pallas_call(
        _mlstm_kernel,
        grid=(BATCH, ML_HEADS),
        in_specs=[lat_spec(0), lat_spec(1), lat_spec(2), lat_spec(3),
                  ctx_spec(0), ctx_spec(1), ctx_spec(2), ctx_spec(3),
                  pl.BlockSpec((None, None, n_lat, 4, length), lambda b, h: (b, h, 0, 0, 0)),
                  pl.BlockSpec((None, None, n_lat, length, 4), lambda b, h: (b, h, 0, 0, 0)),
                  pl.BlockSpec((None, None, 4, length), lambda b, h: (b, h, 0, 0)),
                  pl.BlockSpec((None, None, length, 4), lambda b, h: (b, h, 0, 0)),
                  pl.BlockSpec((1, hd), lambda b, h: (0, h))],
        out_specs=[pl.BlockSpec((SEQ, hd), lambda b, h: (b, h)),
                   pl.BlockSpec((CTX_LEN, hd), lambda b, h: (b, h))],
        out_shape=[jax.ShapeDtypeStruct((M_LAT, ML_WIDTH), BF16),
                   jax.ShapeDtypeStruct((M_CTX, ML_WIDTH), BF16)],
        scratch_shapes=[pltpu.VMEM((1 + n_lat, length, hd), BF16),
                        pltpu.VMEM((1 + n_lat, hd, length), BF16),
                        pltpu.VMEM((1 + n_lat, length, hd), BF16),
                        pltpu.VMEM((1 + n_lat, length, length), F32),
                        pltpu.VMEM((CTX_LEN + SEQ, hd), F32)],
        compiler_params=_params(("parallel", "parallel")),
        name="mlstm",
    )(p_ml, p_ml, p_ml, p_ml, p_ml, p_ml, p_ml, p_ml, g_row_lat, g_col_lat, g_row_ctx, g_col_ctx, gain)


def _rope_partner(x, half):
    lane = lax.broadcasted_iota(I32, x.shape, 1)
    first = (lane & (2 * half - 1)) < half
    return jnp.where(first, pltpu.roll(x, LANES - half, 1), pltpu.roll(x, half, 1))


def _ga_prep_kernel(h_ref, w_ref, gq_ref, gk_ref, cos_ref, sin_ref, q_ref, k_ref, v_ref):
    p = _dot(h_ref[...], w_ref[...])
    cos = cos_ref[...]
    sin = sin_ref[...]
    dh = GA_HEAD_DIM

    def norm_rope(x, gain):
        y = _rms(x) * gain
        return y * cos + _rope_partner(y, dh // 4) * sin

    for h in range(GA_HEADS):
        q = norm_rope(p[:, h * dh:(h + 1) * dh], gq_ref[...])
        q_ref[h] = (q * dh ** -0.5).astype(BF16)
    for g in range(GA_KV_HEADS):
        k0 = GA_WIDTH + g * dh
        v0 = GA_WIDTH + GA_KV_WIDTH + g * dh
        k_ref[g] = norm_rope(p[:, k0:k0 + dh], gk_ref[...]).astype(BF16)
        v_ref[g] = p[:, v0:v0 + dh].astype(BF16)


def _ga_prep(h, w_ga, gain_q, gain_k, cos, sin, tm=512):
    d = h.shape[1]
    n = w_ga.shape[1]
    dh = GA_HEAD_DIM

    def head_spec(n_heads):
        return pl.BlockSpec((n_heads, tm, dh), lambda i: (0, i, 0))

    return pl.pallas_call(
        _ga_prep_kernel,
        grid=(M_ALL // tm,),
        in_specs=[pl.BlockSpec((tm, d), lambda i: (i, 0)),
                  pl.BlockSpec((d, n), lambda i: (0, 0)),
                  pl.BlockSpec((1, dh), lambda i: (0, 0)),
                  pl.BlockSpec((1, dh), lambda i: (0, 0)),
                  pl.BlockSpec((tm, dh), lambda i: (i, 0)),
                  pl.BlockSpec((tm, dh), lambda i: (i, 0))],
        out_specs=[head_spec(GA_HEADS), head_spec(GA_KV_HEADS), head_spec(GA_KV_HEADS)],
        out_shape=[jax.ShapeDtypeStruct((GA_HEADS, M_ALL, dh), BF16),
                   jax.ShapeDtypeStruct((GA_KV_HEADS, M_ALL, dh), BF16),
                   jax.ShapeDtypeStruct((GA_KV_HEADS, M_ALL, dh), BF16)],
        compiler_params=_params(("parallel",)),
        name="ga_prep",
    )(h, w_ga, gain_q, gain_k, cos, sin)


def _la_prep_kernel(h_ref, wsm_ref, gq_ref, gkv_ref, wuq_ref, wukv_ref, cos_ref, sin_ref,
                    q_ref, k_ref, v_ref, tail_ref):
    p = _dot(h_ref[...], wsm_ref[...])
    tail = p[:, SM_TAIL:SM_TAIL + LANES]
    tail_ref[...] = tail
    cq = (_rms(p[:, :LA_Q_RANK]) * gq_ref[...]).astype(BF16)
    ckv = (_rms(p[:, SM_CKV:SM_CKV + LA_KV_RANK]) * gkv_ref[...]).astype(BF16)
    qf = _dot(cq, wuq_ref[...])
    kvf = _dot(ckv, wukv_ref[...])
    cos = cos_ref[...]
    sin = sin_ref[...]
    scale = (LA_NOPE + LA_ROPE) ** -0.5
    low = lax.broadcasted_iota(I32, tail.shape, 1) < LA_ROPE

    def rope(x):
        return x * cos + _rope_partner(x, LA_ROPE // 4) * sin

    kr = rope(tail)
    kr2 = jnp.where(low, kr, pltpu.roll(kr, LA_ROPE, 1)).astype(BF16)
    heads_per_tile = LANES // LA_ROPE
    for j in range(LA_HEADS // heads_per_tile):
        c0 = LA_HEADS * LA_NOPE + j * LANES
        qr = rope(qf[:, c0:c0 + LANES]) * scale
        q_ref[2 * j, :, LA_NOPE:] = jnp.where(low, qr, 0.0).astype(BF16)
        q_ref[2 * j + 1, :, LA_NOPE:] = jnp.where(low, 0.0, qr).astype(BF16)
    for h in range(LA_HEADS):
        q_ref[h, :, :LA_NOPE] = (qf[:, h * LA_NOPE:(h + 1) * LA_NOPE] * scale).astype(BF16)
        kv0 = h * (LA_NOPE + LA_VDIM)
        k_ref[h, :, :LA_NOPE] = kvf[:, kv0:kv0 + LA_NOPE].astype(BF16)
        k_ref[h, :, LA_NOPE:] = kr2
        v_ref[h] = kvf[:, kv0 + LA_NOPE:kv0 + LA_NOPE + LA_VDIM].astype(BF16)


def _la_prep(h, w_small, gain_q, gain_kv, w_uq, w_ukv, cos, sin, tm=512):
    assert LA_HEADS % (LANES // LA_ROPE) == 0 and LA_NOPE == LANES and LA_DK == LA_NOPE + LANES
    d = h.shape[1]

    def full(a):
        return pl.BlockSpec(a.shape, lambda i: (0,) * a.ndim)

    return pl.pallas_call(
        _la_prep_kernel,
        grid=(M_ALL // tm,),
        in_specs=[pl.BlockSpec((tm, d), lambda i: (i, 0)), full(w_small), full(gain_q), full(gain_kv),
                  full(w_uq), full(w_ukv),
                  pl.BlockSpec((tm, LANES), lambda i: (i, 0)),
                  pl.BlockSpec((tm, LANES), lambda i: (i, 0))],
        out_specs=[pl.BlockSpec((LA_HEADS, tm, LA_DK), lambda i: (0, i, 0)),
                   pl.BlockSpec((LA_HEADS, tm, LA_DK), lambda i: (0, i, 0)),
                   pl.BlockSpec((LA_HEADS, tm, LA_VDIM), lambda i: (0, i, 0)),
                   pl.BlockSpec((tm, LANES), lambda i: (i, 0))],
        out_shape=[jax.ShapeDtypeStruct((LA_HEADS, M_ALL, LA_DK), BF16),
                   jax.ShapeDtypeStruct((LA_HEADS, M_ALL, LA_DK), BF16),
                   jax.ShapeDtypeStruct((LA_HEADS, M_ALL, LA_VDIM), BF16),
                   jax.ShapeDtypeStruct((M_ALL, LANES), F32)],
        compiler_params=_params(("parallel",)),
        name="la_prep",
    )(h, w_small, gain_q, gain_kv, w_uq, w_ukv, cos, sin)


ATT_TQ = 256
ATT_ROWS = 1024


def _attn_kernel(q_ref, kl_ref, vl_ref, kc_ref, vc_ref, kca_ref, vca_ref, o_ref, *, n_lat_blocks):
    i = pl.program_id(1)

    @pl.when(i < n_lat_blocks)
    def _():
        for sub in range(ATT_ROWS // ATT_TQ):
            rows = slice(sub * ATT_TQ, (sub + 1) * ATT_TQ)
            q = q_ref[rows, :]
            s_l = _dot_nt(q, kl_ref[...])
            s_c = _dot_nt(q, kc_ref[...])
            mx = jnp.maximum(jnp.max(s_l, axis=1, keepdims=True), jnp.max(s_c, axis=1, keepdims=True))
            p_l = jnp.exp(s_l - mx)
            p_c = jnp.exp(s_c - mx)
            den = jnp.sum(p_l, axis=1, keepdims=True) + jnp.sum(p_c, axis=1, keepdims=True)
            o = _dot(p_l.astype(BF16), vl_ref[...]) + _dot(p_c.astype(BF16), vc_ref[...])
            o_ref[rows, :] = (o / den).astype(o_ref.dtype)

    @pl.when(i >= n_lat_blocks)
    def _():
        for sub in range(ATT_ROWS // CTX_LEN):
            rows = slice(sub * CTX_LEN, (sub + 1) * CTX_LEN)
            s = _dot_nt(q_ref[rows, :], kca_ref[rows, :])
            mx = jnp.max(s, axis=1, keepdims=True)
            p = jnp.exp(s - mx)
            den = jnp.sum(p, axis=1, keepdims=True)
            o = _dot(p.astype(BF16), vca_ref[rows, :])
            o_ref[rows, :] = (o / den).astype(o_ref.dtype)


def _attention(q, k, v, kv_group, need_ctx, name):
    assert M_CTX == ATT_ROWS and CTX_LEN == ATT_TQ
    n_heads, _, dk = q.shape
    dv = v.shape[2]
    n_lat_blocks = M_LAT // ATT_ROWS
    per_sample = SEQ // ATT_ROWS
    ctx0 = M_LAT // CTX_LEN
    n_blocks = n_lat_blocks + (1 if need_ctx else 0)

    def sample(i):
        return jnp.minimum(i // per_sample, BATCH - 1)

    def lat_spec(dim):
        return pl.BlockSpec((None, SEQ, dim), lambda h, i: (h // kv_group, sample(i), 0))

    def ctx_spec(dim):
        return pl.BlockSpec((None, CTX_LEN, dim), lambda h, i: (h // kv_group, ctx0 + sample(i), 0))

    def ctx_all_spec(dim):
        return pl.BlockSpec((None, M_CTX, dim), lambda h, i: (h // kv_group, n_lat_blocks, 0))

    return pl.pallas_call(
        functools.partial(_attn_kernel, n_lat_blocks=n_lat_blocks),
        grid=(n_heads, n_blocks),
        in_specs=[pl.BlockSpec((None, ATT_ROWS, dk), lambda h, i: (h, i, 0)),
                  lat_spec(dk), lat_spec(dv), ctx_spec(dk), ctx_spec(dv), ctx_all_spec(dk), ctx_all_spec(dv)],
        out_specs=pl.BlockSpec((ATT_ROWS, dv), lambda h, i: (i, h)),
        out_shape=jax.ShapeDtypeStruct((n_blocks * ATT_ROWS, n_heads * dv), BF16),
        compiler_params=_params(("parallel", "arbitrary")),
        name=name,
    )(q, k, v, k, v, k, v)


def _router_kernel(h_ref, w_ref, o_ref):
    logits = _dot_nt(w_ref[...], h_ref[...])
    mx = jnp.max(logits, axis=0, keepdims=True)
    e = jnp.exp(logits - mx)
    o_ref[...] = e / jnp.sum(e, axis=0, keepdims=True)


def _router(h, w_router_t, m_rows, tm=1024):
    d = h.shape[1]
    return pl.pallas_call(
        _router_kernel,
        grid=(m_rows // tm,),
        in_specs=[pl.BlockSpec((tm, d), lambda i: (i, 0)),
                  pl.BlockSpec((N_EXPERTS, d), lambda i: (0, 0))],
        out_specs=pl.BlockSpec((N_EXPERTS, tm), lambda i: (0, i)),
        out_shape=jax.ShapeDtypeStruct((N_EXPERTS, m_rows), F32),
        compiler_params=_params(("parallel",)),
        name="router",
    )(h, w_router_t)


PREFIX_BLOCK = 256


def _exclusive_prefix(x_bf, upper):
    n_tok = x_bf.shape[1]
    carry = jnp.zeros((x_bf.shape[0], 1), F32)
    parts = []
    for blk in range(n_tok // PREFIX_BLOCK):
        xb = x_bf[:, blk * PREFIX_BLOCK:(blk + 1) * PREFIX_BLOCK]
        parts.append(_dot(xb, upper) + carry)
        carry = carry + jnp.sum(xb.astype(F32), axis=1, keepdims=True)
    return parts[0] if len(parts) == 1 else jnp.concatenate(parts, axis=1)


def _bitonic_sort_descending(tiles):
    n_tiles = len(tiles)
    n = n_tiles * LANES
    lane = lax.broadcasted_iota(I32, tiles[0].shape, 1)
    k = 2
    while k <= n:
        j = k // 2
        while j >= 1:
            new_tiles = []
            for t in range(n_tiles):
                x = tiles[t]
                if j < LANES:
                    low = (lane & j) == 0
                    partner = jnp.where(low, pltpu.roll(x, LANES - j, 1), pltpu.roll(x, j, 1))
                else:
                    partner = tiles[t ^ (j // LANES)]
                big = jnp.maximum(x, partner)
                small = jnp.minimum(x, partner)
                up = ((t * LANES) & k) == 0
                if j >= LANES:
                    take_big = (((t * LANES) & j) == 0) == up
                    new_tiles.append(big if take_big else small)
                elif k >= LANES:
                    new_tiles.append(jnp.where(low, big, small) if up else jnp.where(low, small, big))
                else:
                    agree = (lane & j) * (k // j) == (lane & k)
                    new_tiles.append(jnp.where(agree, big, small))
            tiles = new_tiles
            j //= 2
        k *= 2
    return tiles


def _topk_kernel(aff_ref, rank_ref, *, cap):
    a = aff_ref[...]
    n_tok = a.shape[1]
    ordered = _bitonic_sort_descending([a[:, t * LANES:(t + 1) * LANES] for t in range(n_tok // LANES)])
    pos = cap - 1
    thr = ordered[pos // LANES][:, pos % LANES:pos % LANES + 1]
    gt = a > thr
    eq = a == thr
    need = cap - jnp.sum(jnp.where(gt, 1.0, 0.0), axis=1, keepdims=True)
    ri = lax.broadcasted_iota(I32, (PREFIX_BLOCK, PREFIX_BLOCK), 0)
    ci = lax.broadcasted_iota(I32, (PREFIX_BLOCK, PREFIX_BLOCK), 1)
    upper = jnp.where(ri < ci, 1.0, 0.0).astype(BF16)
    eq_before = _exclusive_prefix(jnp.where(eq, 1.0, 0.0).astype(BF16), upper)
    sel = gt | (eq & (eq_before < need))
    slot = _exclusive_prefix(jnp.where(sel, 1.0, 0.0).astype(BF16), upper)
    rank_ref[...] = jnp.where(sel, slot.astype(I32), -1)


def _topk(aff, n_tok, cap, blk0, n_sets):
    return pl.pallas_call(
        functools.partial(_topk_kernel, cap=cap),
        grid=(n_sets,),
        in_specs=[pl.BlockSpec((N_EXPERTS, n_tok), lambda s: (0, blk0 + s))],
        out_specs=pl.BlockSpec((N_EXPERTS, n_tok), lambda s: (0, s)),
        out_shape=jax.ShapeDtypeStruct((N_EXPERTS, n_sets * n_tok), I32),
        compiler_params=_params(("parallel",)),
        name=f"topk_{n_tok}",
    )(aff)


def _gather_kernel(rank_ref, aff_ref, h_ref, *rest, cap):
    xg_ref, val_ref = rest[-2:]
    r = rank_ref[...]
    n_tok = r.shape[1]
    slot = lax.broadcasted_iota(I32, (cap, n_tok), 0)
    hit = r == slot
    onehot = jnp.where(hit, 1.0, 0.0).astype(BF16)
    xg_ref[...] = _dot(onehot, h_ref[...]).astype(xg_ref.dtype)
    val_ref[...] = jnp.sum(jnp.where(hit, aff_ref[...], 0.0), axis=1, keepdims=True)


def _gather(rank3, aff3, h, n_tok, cap, blk0, row0, rows_total, prev):
    d = h.shape[1]
    rb0 = row0 // cap
    in_specs = [pl.BlockSpec((None, 1, n_tok), lambda b, e: (e, 0, b)),
                pl.BlockSpec((None, 1, n_tok), lambda b, e: (e, 0, blk0 + b)),
                pl.BlockSpec((n_tok, d), lambda b, e: (blk0 + b, 0))]
    args = [rank3, aff3, h]
    aliases = {}
    if prev is not None:
        in_specs += [pl.BlockSpec(memory_space=pl.ANY), pl.BlockSpec(memory_space=pl.ANY)]
        args += list(prev)
        aliases = {3: 0, 4: 1}
    return pl.pallas_call(
        functools.partial(_gather_kernel, cap=cap),
        grid=(BATCH, N_EXPERTS),
        in_specs=in_specs,
        out_specs=[pl.BlockSpec((None, cap, d), lambda b, e: (e, rb0 + b, 0)),
                   pl.BlockSpec((None, cap, 1), lambda b, e: (e, rb0 + b, 0))],
        out_shape=[jax.ShapeDtypeStruct((N_EXPERTS, rows_total, d), BF16),
                   jax.ShapeDtypeStruct((N_EXPERTS, rows_total, 1), F32)],
        input_output_aliases=aliases,
        compiler_params=_params(("parallel", "arbitrary")),
        name=f"moe_gather_{n_tok}",
    )(*args)


EXPERT_COLS = 512


def _expert_kernel(x_ref, w1_ref, w3_ref, w2_ref, val_ref, y_ref, acc_ref):
    f = pl.program_id(1)
    last = pl.num_programs(1) - 1
    col_slices = [slice(c0, c0 + EXPERT_COLS) for c0 in range(0, acc_ref.shape[1], EXPERT_COLS)]

    def hidden():
        x = x_ref[...]
        a = _dot(x, w1_ref[...].astype(BF16))
        u = _dot(x, w3_ref[...].astype(BF16))
        return (a * jax.nn.sigmoid(a) * u).astype(BF16)

    def down(hm, cols):
        return _dot(hm, w2_ref[:, cols].astype(BF16))

    @pl.when(f == 0)
    def _():
        hm = hidden()
        for cols in col_slices:
            acc_ref[:, cols] = down(hm, cols)

    @pl.when((f > 0) & (f < last))
    def _():
        hm = hidden()
        for cols in col_slices:
            acc_ref[:, cols] += down(hm, cols)

    @pl.when(f == last)
    def _():
        hm = hidden()
        val = val_ref[...]
        for cols in col_slices:
            y_ref[:, cols] = ((acc_ref[:, cols] + down(hm, cols)) * val).astype(y_ref.dtype)


def _experts(xg, vals, w_e1, w_e3, w_e2, layer, tf=256):
    n_exp, rows, d = xg.shape
    ff = w_e1.shape[3]
    assert ff // tf >= 2
    return pl.pallas_call(
        _expert_kernel,
        grid=(n_exp, ff // tf),
        in_specs=[pl.BlockSpec((None, rows, d), lambda e, f: (e, 0, 0)),
                  pl.BlockSpec((None, None, d, tf), lambda e, f: (layer, e, 0, f)),
                  pl.BlockSpec((None, None, d, tf), lambda e, f: (layer, e, 0, f)),
                  pl.BlockSpec((None, None, tf, d), lambda e, f: (layer, e, f, 0)),
                  pl.BlockSpec((None, rows, 1), lambda e, f: (e, 0, 0))],
        out_specs=pl.BlockSpec((None, rows, d), lambda e, f: (e, 0, 0)),
        out_shape=jax.ShapeDtypeStruct((n_exp, rows, d), BF16),
        scratch_shapes=[pltpu.VMEM((rows, d), F32)],
        compiler_params=_params(("parallel", "arbitrary")),
        name="moe_experts",
    )(xg, w_e1, w_e3, w_e2, vals)


COMBINE_LANES = 1024


def _combine_kernel(rank_ref, y_ref, x_ref, g_ref, *rest, cap):
    o_ref, lhs_ref = rest[-2:]
    n_slots = N_EXPERTS * cap
    chunk = min(COMBINE_LANES, n_slots)
    shift = cap.bit_length() - 1

    @pl.when(pl.program_id(2) == 0)
    def _():
        r = rank_ref[...].astype(F32).astype(BF16)
        tm = r.shape[0]
        for c0 in range(0, n_slots, chunk):
            lane = c0 + lax.broadcasted_iota(I32, (LANES, chunk), 1)
            row = lax.broadcasted_iota(I32, (LANES, chunk), 0)
            expand = jnp.where(lax.shift_right_logical(lane, shift) == row, 1.0, 0.0).astype(BF16)
            r_exp = _dot(r, expand)
            slot = (c0 + lax.broadcasted_iota(I32, (tm, chunk), 1)) & (cap - 1)
            lhs_ref[:, c0:c0 + chunk] = jnp.where(r_exp == slot.astype(F32), 1.0, 0.0).astype(BF16)

    y = y_ref[...]
    o_ref[...] = x_ref[...] + g_ref[...] * _dot(lhs_ref[...], y.reshape(n_slots, y.shape[2]))


def _combine(rank_t, y, x, gate_tab, out_prev, n_tok, cap, blk0, row0, tm, tn=1024):
    assert cap & (cap - 1) == 0
    d = x.shape[1]
    tpb = n_tok // tm
    xrow0 = blk0 * tpb
    rb0 = row0 // cap

    def grp(b):
        return b if n_tok == SEQ else BATCH

    in_specs = [pl.BlockSpec((tm, LANES), lambda b, i, j: (b * tpb + i, 0)),
                pl.BlockSpec((N_EXPERTS, cap, tn), lambda b, i, j: (0, rb0 + b, j)),
                pl.BlockSpec((tm, tn), lambda b, i, j: (xrow0 + b * tpb + i, j)),
                pl.BlockSpec((None, 1, tn), lambda b, i, j: (grp(b), 0, j))]
    args = [rank_t, y, x, gate_tab]
    aliases = {}
    if out_prev is not None:
        in_specs.append(pl.BlockSpec(memory_space=pl.ANY))
        args.append(out_prev)
        aliases = {4: 0}
    return pl.pallas_call(
        functools.partial(_combine_kernel, cap=cap),
        grid=(BATCH, tpb, d // tn),
        in_specs=in_specs,
        out_specs=pl.BlockSpec((tm, tn), lambda b, i, j: (xrow0 + b * tpb + i, j)),
        out_shape=jax.ShapeDtypeStruct(x.shape, F32),
        scratch_shapes=[pltpu.VMEM((tm, N_EXPERTS * cap), BF16)],
        input_output_aliases=aliases,
        compiler_params=_params(("parallel", "parallel", "arbitrary")),
        name=f"moe_combine_{n_tok}",
    )(*args)


def _expert_choice(h2, x, gate_tab, w_router_t, w_e1, w_e3, w_e2, layer, need_ctx):
    m_rows = h2.shape[0]
    aff = _router(h2, w_router_t, m_rows)
    aff3 = aff.reshape(N_EXPERTS, 1, m_rows)
    sets = [(SEQ, 0)]
    if need_ctx:
        sets.append((CTX_LEN, M_LAT // CTX_LEN))
    caps = [CAPACITY_FACTOR * n_tok // N_EXPERTS for n_tok, _ in sets]
    rows_total = sum(BATCH * cap for cap in caps)
    ranks, gathered, row0 = [], None, 0
    for (n_tok, blk0), cap in zip(sets, caps):
        rank = _topk(aff, n_tok, cap, blk0, BATCH)
        gathered = _gather(rank.reshape(N_EXPERTS, 1, BATCH * n_tok), aff3, h2, n_tok, cap, blk0, row0, rows_total,
                           gathered)
        ranks.append((rank, row0))
        row0 += BATCH * cap
    y = _experts(gathered[0], gathered[1], w_e1, w_e3, w_e2, layer)
    out = None
    for (n_tok, blk0), cap, (rank, row0) in zip(sets, caps, ranks):
        rank_t = jnp.pad(rank.T, ((0, 0), (0, LANES - N_EXPERTS)))
        out = _combine(rank_t, y, x, gate_tab, out, n_tok, cap, blk0, row0, tm=min(512, n_tok))
    return out


def _rope_tables(dim):
    rows = SEQ // GRID_W
    quarter = dim // 4
    inv = ROPE_BASE ** (-jnp.arange(quarter, dtype=F32) / quarter)
    row = jnp.broadcast_to(jnp.arange(rows, dtype=F32)[:, None], (rows, GRID_W)).reshape(-1)
    col = jnp.broadcast_to(jnp.arange(GRID_W, dtype=F32)[None, :], (rows, GRID_W)).reshape(-1)
    ang_r = row[:, None] * inv[None, :]
    ang_c = col[:, None] * inv[None, :]
    cos = jnp.concatenate([jnp.cos(ang_r), jnp.cos(ang_r), jnp.cos(ang_c), jnp.cos(ang_c)], axis=-1)
    sin = jnp.concatenate([-jnp.sin(ang_r), jnp.sin(ang_r), -jnp.sin(ang_c), jnp.sin(ang_c)], axis=-1)
    cos = jnp.tile(cos, (BATCH, LANES // dim))
    sin = jnp.tile(sin, (BATCH, LANES // dim))
    cos = jnp.concatenate([cos, jnp.ones((M_CTX, LANES), F32)], axis=0)
    sin = jnp.concatenate([sin, jnp.zeros((M_CTX, LANES), F32)], axis=0)
    return cos, sin


def _layer(x, mods, layer, w_in, ml_gate_bias, ml_norm, la_q_norm, la_kv_norm, la_w_uq, la_w_ukv,
           ga_q_norm, ga_k_norm, w_br_ml, w_br_la, w_br_ga, w_out, w_router, w_e1, w_e3, w_e2, ropes, need_ctx):
    d = D_MODEL
    m_rows = M_ALL if need_ctx else M_LAT
    sh1, sc1, g1, sh2, sc2, g2 = [mods[:N_GROUPS, k * d:(k + 1) * d].reshape(N_GROUPS, 1, d) for k in range(6)]
    (cos_la, sin_la), (cos_ga, sin_ga) = ropes

    w_ml = w_in[:, :OFF_MLG].astype(BF16)
    w_small = jnp.concatenate([w_in[:, OFF_CQ:OFF_GAQ], w_in[:, OFF_MLG:OFF_CQ],
                               jnp.zeros((d, SMALL_WIDTH - (OFF_GAQ - OFF_MLG)), F32)], axis=1).astype(BF16)
    w_ga = w_in[:, OFF_GAQ:OFF_GATE].astype(BF16)
    w_gate = w_in[:, OFF_GATE:].astype(BF16)
    w_uq = la_w_uq.reshape(LA_Q_RANK, LA_HEADS, LA_NOPE + LA_ROPE)
    w_uq = jnp.concatenate([w_uq[:, :, :LA_NOPE].reshape(LA_Q_RANK, LA_HEADS * LA_NOPE),
                            w_uq[:, :, LA_NOPE:].reshape(LA_Q_RANK, LA_HEADS * LA_ROPE)], axis=1).astype(BF16)

    h = _rms_mod(x, 1.0 + sc1, sh1, M_ALL)
    p_ml = _mm(h, w_ml, name="proj_ml")
    q_la, k_la, v_la, tail = _la_prep(h, w_small, la_q_norm.reshape(1, LA_Q_RANK), la_kv_norm.reshape(1, LA_KV_RANK),
                                      w_uq, la_w_ukv.astype(BF16), cos_la, sin_la)
    q_ga, k_ga, v_ga = _ga_prep(h, w_ga, ga_q_norm.reshape(1, GA_HEAD_DIM), ga_k_norm.reshape(1, GA_HEAD_DIM),
                                cos_ga, sin_ga)

    gates = (tail[:, TAIL_MLG:TAIL_MLG + ML_GATES] + ml_gate_bias[None, :]).reshape(M_ALL, 2, 2, ML_HEADS)
    gates = jnp.stack([gates[:, 0, 0], jax.nn.log_sigmoid(gates[:, 0, 1]),
                       gates[:, 1, 0], jax.nn.log_sigmoid(gates[:, 1, 1])], axis=-1)
    n_lat = SEQ // ML_CHUNK
    g_lat = gates[:M_LAT].reshape(BATCH, n_lat, ML_CHUNK, ML_HEADS, 4)
    g_ctx = gates[M_LAT:].reshape(BATCH, CTX_LEN, ML_HEADS, 4)
    y_ml_lat, y_ml_ctx = _mlstm(p_ml, jnp.transpose(g_lat, (0, 3, 1, 4, 2)), jnp.transpose(g_lat, (0, 3, 1, 2, 4)),
                                jnp.transpose(g_ctx, (0, 2, 3, 1)), jnp.transpose(g_ctx, (0, 2, 1, 3)),
                                ml_norm.reshape(1, ML_WIDTH))

    y_la = _attention(q_la, k_la, v_la, 1, need_ctx, "attn_mla")
    y_ga = _attention(q_ga, k_ga, v_ga, GA_HEADS // GA_KV_HEADS, need_ctx, "attn_gqa")

    merged = _merge(h, y_ml_lat, y_ml_ctx, y_la, y_ga, w_gate, w_br_ml.astype(BF16), w_br_la.astype(BF16),
                    w_br_ga.astype(BF16), m_rows)
    x = _mm_residual(merged, w_out.astype(BF16), x, g1, m_rows, name="out_proj")

    h2 = _rms_mod(x, 1.0 + sc2, sh2, m_rows)
    return _expert_choice(h2, x, g2, w_router.T.astype(BF16), w_e1, w_e3, w_e2, layer, need_ctx)


def kernel(x, c, ctx, c_ctx, w_mod, b_mod, w_in, ml_gate_bias, ml_norm, la_q_norm, la_kv_norm, la_w_uq, la_w_ukv,
           ga_q_norm, ga_k_norm, w_br_ml, w_br_la, w_br_ga, w_out, w_router, w_e1, w_e3, w_e2, final_norm):
    d = D_MODEL
    xs = jnp.concatenate([x.reshape(M_LAT, d), ctx.reshape(M_CTX, d)], axis=0)
    cc = jnp.concatenate([c, c_ctx[None, :], jnp.zeros((MODS_ROWS - N_GROUPS, d), F32)], axis=0)
    mods = _mods(cc, w_mod, b_mod.reshape(DEPTH, 1, 6 * d))
    ropes = (_rope_tables(LA_ROPE), _rope_tables(GA_HEAD_DIM))
    for l in range(DEPTH):
        need_ctx = l < DEPTH - 1
        xs = _layer(xs, mods[l], l, w_in[l], ml_gate_bias[l], ml_norm[l], la_q_norm[l], la_kv_norm[l],
                    la_w_uq[l], la_w_ukv[l], ga_q_norm[l], ga_k_norm[l], w_br_ml[l], w_br_la[l], w_br_ga[l],
                    w_out[l], w_router[l], w_e1, w_e3, w_e2, ropes, need_ctx)
    out = _rms_gain(xs, final_norm.reshape(1, d), F32, name="final_norm")
    return out.reshape(BATCH, SEQ, d)
```

```python
import functools

import jax
import jax.numpy as jnp
from jax import lax
from jax.experimental import pallas as pl
from jax.experimental.pallas import tpu as pltpu

F32 = jnp.float32
BF16 = jnp.bfloat16
I32 = jnp.int32

D_MODEL = 2048
BATCH = 4
SEQ = 2048
DEPTH = 2
GRID_W = 64
CTX_LEN = 256
EPS = 1e-6
ROPE_BASE = 10000.0

ML_HEADS = 4
ML_HEAD_DIM = 256
ML_WIDTH = ML_HEADS * ML_HEAD_DIM
ML_GATES = 2 * 2 * ML_HEADS
ML_CHUNK = 256

LA_HEADS = 8
LA_NOPE = 128
LA_ROPE = 64
LA_VDIM = 128
LA_Q_RANK = 512
LA_KV_RANK = 256
LA_DK = 256

GA_HEADS = 8
GA_KV_HEADS = 2
GA_HEAD_DIM = 128
GA_WIDTH = GA_HEADS * GA_HEAD_DIM
GA_KV_WIDTH = GA_KV_HEADS * GA_HEAD_DIM

N_EXPERTS = 16
EXPERT_FF = 1024
CAPACITY_FACTOR = 2

M_LAT = BATCH * SEQ
M_CTX = BATCH * CTX_LEN
M_ALL = M_LAT + M_CTX
MODS_ROWS = 16
N_GROUPS = BATCH + 1

LANES = 128

OFF_MLG = 4 * ML_WIDTH
OFF_CQ = OFF_MLG + ML_GATES
OFF_CKV = OFF_CQ + LA_Q_RANK
OFF_KR = OFF_CKV + LA_KV_RANK
OFF_GAQ = OFF_KR + LA_ROPE
OFF_GAK = OFF_GAQ + GA_WIDTH
OFF_GAV = OFF_GAK + GA_KV_WIDTH
OFF_GATE = OFF_GAV + GA_KV_WIDTH
SM_CKV = LA_Q_RANK
SM_TAIL = SM_CKV + LA_KV_RANK
SMALL_WIDTH = SM_TAIL + LANES
TAIL_MLG = LA_ROPE

VMEM_LIMIT = 56 * 1024 * 1024


def _params(semantics, vmem=VMEM_LIMIT):
    return pltpu.CompilerParams(dimension_semantics=semantics, vmem_limit_bytes=vmem)


def _dot(a, b):
    return jnp.dot(a, b, preferred_element_type=F32)


def _dot_nt(a, b):
    return lax.dot_general(a, b, (((1,), (1,)), ((), ())), preferred_element_type=F32)


def _rms(x):
    return x * lax.rsqrt(jnp.mean(x * x, axis=-1, keepdims=True) + EPS)


def _group_of_block(tm):
    return lambda i: (i * tm) // SEQ


def _mods_kernel(c_ref, w_ref, b_ref, o_ref):
    c = c_ref[...]
    a = (c * jax.nn.sigmoid(c)).astype(BF16)
    part = _dot(a, w_ref[...].astype(BF16))

    @pl.when(pl.program_id(1) == 0)
    def _():
        o_ref[...] = part + b_ref[...]

    @pl.when(pl.program_id(1) > 0)
    def _():
        o_ref[...] += part


def _mods(cc, w_mod, b_mod, tk=256):
    depth, d, n = w_mod.shape
    rows = cc.shape[0]
    return pl.pallas_call(
        _mods_kernel,
        grid=(depth, d // tk),
        in_specs=[pl.BlockSpec((rows, tk), lambda l, k: (0, k)),
                  pl.BlockSpec((None, tk, n), lambda l, k: (l, k, 0)),
                  pl.BlockSpec((None, 1, n), lambda l, k: (l, 0, 0))],
        out_specs=pl.BlockSpec((None, rows, n), lambda l, k: (l, 0, 0)),
        out_shape=jax.ShapeDtypeStruct((depth, rows, n), F32),
        compiler_params=_params(("parallel", "arbitrary")),
        name="mods",
    )(cc, w_mod, b_mod)


def _rms_kernel(x_ref, scale_ref, shift_ref, o_ref):
    o_ref[...] = (_rms(x_ref[...]) * scale_ref[...] + shift_ref[...]).astype(o_ref.dtype)


def _rms_gain_kernel(x_ref, scale_ref, o_ref):
    o_ref[...] = (_rms(x_ref[...]) * scale_ref[...]).astype(o_ref.dtype)


def _rms_mod(x, scale_tab, shift_tab, m_rows, tm=512):
    d = x.shape[1]
    grp = _group_of_block(tm)
    return pl.pallas_call(
        _rms_kernel,
        grid=(m_rows // tm,),
        in_specs=[pl.BlockSpec((tm, d), lambda i: (i, 0)),
                  pl.BlockSpec((None, 1, d), lambda i: (grp(i), 0, 0)),
                  pl.BlockSpec((None, 1, d), lambda i: (grp(i), 0, 0))],
        out_specs=pl.BlockSpec((tm, d), lambda i: (i, 0)),
        out_shape=jax.ShapeDtypeStruct((m_rows, d), BF16),
        compiler_params=_params(("parallel",)),
        name="rms_mod",
    )(x, scale_tab, shift_tab)


def _rms_gain(x, gain, out_dtype, tm=512, name="rms_gain"):
    m_rows, width = x.shape
    return pl.pallas_call(
        _rms_gain_kernel,
        grid=(m_rows // tm,),
        in_specs=[pl.BlockSpec((tm, width), lambda i: (i, 0)),
                  pl.BlockSpec((1, width), lambda i: (0, 0))],
        out_specs=pl.BlockSpec((tm, width), lambda i: (i, 0)),
        out_shape=jax.ShapeDtypeStruct((m_rows, width), out_dtype),
        compiler_params=_params(("parallel",)),
        name=name,
    )(x, gain)


def _mm_kernel(a_ref, w_ref, o_ref):
    o_ref[...] = _dot(a_ref[...], w_ref[...]).astype(o_ref.dtype)


def _mm_res_kernel(a_ref, w_ref, x_ref, g_ref, o_ref):
    o_ref[...] = x_ref[...] + g_ref[...] * _dot(a_ref[...], w_ref[...])


def _mm(a, w, *, tm=1024, tn=1024, out_dtype=F32, name="mm"):
    m_rows, k = a.shape
    n_cols = w.shape[1]
    tm = min(tm, m_rows)
    tn = min(tn, n_cols)
    assert m_rows % tm == 0 and n_cols % tn == 0
    return pl.pallas_call(
        _mm_kernel,
        grid=(n_cols // tn, m_rows // tm),
        in_specs=[pl.BlockSpec((tm, k), lambda j, i: (i, 0)),
                  pl.BlockSpec((k, tn), lambda j, i: (0, j))],
        out_specs=pl.BlockSpec((tm, tn), lambda j, i: (i, j)),
        out_shape=jax.ShapeDtypeStruct((m_rows, n_cols), out_dtype),
        compiler_params=_params(("parallel", "arbitrary")),
        name=name,
    )(a, w)


def _mm_residual(a, w, x, gate_tab, m_rows, *, tm=1024, tn=1024, name="mm_res"):
    k = a.shape[1]
    n = w.shape[1]
    grp = _group_of_block(tm)
    return pl.pallas_call(
        _mm_res_kernel,
        grid=(n // tn, m_rows // tm),
        in_specs=[pl.BlockSpec((tm, k), lambda j, i: (i, 0)),
                  pl.BlockSpec((k, tn), lambda j, i: (0, j)),
                  pl.BlockSpec((tm, tn), lambda j, i: (i, j)),
                  pl.BlockSpec((None, 1, tn), lambda j, i: (grp(i), 0, j))],
        out_specs=pl.BlockSpec((tm, tn), lambda j, i: (i, j)),
        out_shape=jax.ShapeDtypeStruct((m_rows, n), F32),
        compiler_params=_params(("parallel", "arbitrary")),
        name=name,
    )(a, w, x, gate_tab)


def _merge_kernel(h_ref, y0l_ref, y0c_ref, y1_ref, y2_ref, wg0_ref, wg1_ref, wg2_ref, wb0_ref, wb1_ref, wb2_ref,
                  o_ref, *, n_lat_tiles):
    h = h_ref[...]
    y0 = jnp.where(pl.program_id(1) < n_lat_tiles, y0l_ref[...], y0c_ref[...])
    acc = jax.nn.sigmoid(_dot(h, wg0_ref[...])) * _dot(y0, wb0_ref[...])
    acc = acc + jax.nn.sigmoid(_dot(h, wg1_ref[...])) * _dot(y1_ref[...], wb1_ref[...])
    acc = acc + jax.nn.sigmoid(_dot(h, wg2_ref[...])) * _dot(y2_ref[...], wb2_ref[...])
    o_ref[...] = acc.astype(o_ref.dtype)


def _merge(h, y_ml_lat, y_ml_ctx, y_la, y_ga, w_gate, w_br_ml, w_br_la, w_br_ga, m_rows, tm=512, tn=512):
    d = D_MODEL
    nb = d // tn
    n_lat_tiles = M_LAT // tm
    y_spec = pl.BlockSpec((tm, ML_WIDTH), lambda j, i: (i, 0))
    wb_spec = pl.BlockSpec((ML_WIDTH, tn), lambda j, i: (0, j))
    return pl.pallas_call(
        functools.partial(_merge_kernel, n_lat_tiles=n_lat_tiles),
        grid=(nb, m_rows // tm),
        in_specs=[pl.BlockSpec((tm, d), lambda j, i: (i, 0)),
                  pl.BlockSpec((tm, ML_WIDTH), lambda j, i: (jnp.minimum(i, n_lat_tiles - 1), 0)),
                  pl.BlockSpec((tm, ML_WIDTH), lambda j, i: (jnp.maximum(i - n_lat_tiles, 0), 0)),
                  y_spec, y_spec,
                  pl.BlockSpec((d, tn), lambda j, i: (0, j)),
                  pl.BlockSpec((d, tn), lambda j, i: (0, nb + j)),
                  pl.BlockSpec((d, tn), lambda j, i: (0, 2 * nb + j)),
                  wb_spec, wb_spec, wb_spec],
        out_specs=pl.BlockSpec((tm, tn), lambda j, i: (i, j)),
        out_shape=jax.ShapeDtypeStruct((m_rows, d), BF16),
        compiler_params=_params(("parallel", "arbitrary")),
        name="merge",
    )(h, y_ml_lat, y_ml_ctx, y_la, y_ga, w_gate, w_gate, w_gate, w_br_ml, w_br_la, w_br_ga)


def _mlstm_kernel(ql_ref, kl_ref, vl_ref, ol_ref, qc_ref, kc_ref, vc_ref, oc_ref,
                  grl_ref, gcl_ref, grc_ref, gcc_ref, gain_ref, yl_ref, yc_ref,
                  qb_ref, kt_ref, vb_ref, s_ref, hsum_ref):
    length = ML_CHUNK
    hd = ML_HEAD_DIM
    n_lat = SEQ // length
    n_chunks = n_lat + 1
    scale = hd ** -0.5

    def load(c, ctx_ref, lat_ref):
        return ctx_ref[...] if c == 0 else lat_ref[(c - 1) * length:c * length, :]

    for c in range(n_chunks):
        qb = (load(c, qc_ref, ql_ref) * scale).astype(BF16)
        kt = jnp.transpose(load(c, kc_ref, kl_ref)).astype(BF16)
        qb_ref[c] = qb
        kt_ref[c] = kt
        vb_ref[c] = load(c, vc_ref, vl_ref).astype(BF16)
        s_ref[c] = _dot(qb, kt)

    ri = lax.broadcasted_iota(I32, (length, length), 0)
    ci = lax.broadcasted_iota(I32, (length, length), 1)
    for direction in range(2):
        reverse = direction == 1
        row_i = 2 * direction
        allowed = (ci >= ri) if reverse else (ci <= ri)
        allowed_t = (ri >= ci) if reverse else (ri <= ci)
        order = [0] + ([n_lat - j for j in range(n_lat)] if reverse else list(range(1, n_chunks)))
        m = jnp.zeros((1, 1), F32)
        ct = jnp.zeros((hd, hd), F32)
        n_vec = jnp.zeros((1, hd), F32)
        for c in order:
            g_rows = grc_ref[...] if c == 0 else grl_ref[c - 1]
            g_cols = gcc_ref[...] if c == 0 else gcl_ref[c - 1]
            li_r = g_rows[row_i:row_i + 1, :]
            lf_r = g_rows[row_i + 1:row_i + 2, :]
            li_c = g_cols[:, row_i:row_i + 1]
            lf_c = g_cols[:, row_i + 1:row_i + 2]
            cum_c = jnp.sum(jnp.where(allowed, lf_r, 0.0), axis=1, keepdims=True)
            cum_r = jnp.sum(jnp.where(allowed_t, lf_c, 0.0), axis=0, keepdims=True)
            total = jnp.sum(lf_r, axis=1, keepdims=True)
            dmat = jnp.where(allowed, cum_c - cum_r + li_r, -jnp.inf)
            m_loc = jnp.max(dmat, axis=1, keepdims=True)
            s_loc = s_ref[c] * jnp.exp(dmat - m_loc)
            intra = _dot(s_loc.astype(BF16), vb_ref[c])
            intra_sum = jnp.sum(s_loc, axis=1, keepdims=True)
            inter = cum_c + m
            m_row = jnp.maximum(inter, m_loc)
            shrink = jnp.exp(-jnp.abs(inter - m_loc))
            f_loc = jnp.where(m_loc >= inter, 1.0, shrink)
            w_inter = jnp.where(inter >= m_loc, 1.0, shrink)
            q32 = load(c, qc_ref, ql_ref) * scale
            num = intra * f_loc + w_inter * _dot(qb_ref[c], ct.astype(BF16))
            den = intra_sum * f_loc + w_inter * jnp.sum(q32 * n_vec, axis=1, keepdims=True)
            h_out = num / jnp.maximum(jnp.abs(den), jnp.exp(-m_row))
            rows = slice(c * length, (c + 1) * length)
            if reverse:
                hsum_ref[rows, :] += h_out
            else:
                hsum_ref[rows, :] = h_out
            if c != order[-1]:
                g_r = total - cum_r + li_r
                g_c = total - cum_c + li_c
                m_new = jnp.maximum(total + m, jnp.max(g_r, axis=1, keepdims=True))
                wk_c = jnp.exp(g_c - m_new)
                decay = jnp.exp(total + m - m_new)
                wv = (wk_c * load(c, vc_ref, vl_ref)).astype(BF16)
                ct = decay * ct + _dot(kt_ref[c], wv)
                n_vec = decay * n_vec + jnp.sum(wk_c * load(c, kc_ref, kl_ref), axis=0, keepdims=True)
                m = m_new

    gain = gain_ref[...]

    def finish(rows, o_pre):
        return (_rms(hsum_ref[rows, :]) * gain * jax.nn.sigmoid(o_pre)).astype(BF16)

    yc_ref[...] = finish(slice(0, CTX_LEN), oc_ref[...])
    yl_ref[...] = finish(slice(CTX_LEN, CTX_LEN + SEQ), ol_ref[...])


def _mlstm(p_ml, g_row_lat, g_col_lat, g_row_ctx, g_col_ctx, gain):
    assert CTX_LEN == ML_CHUNK
    hd = ML_HEAD_DIM
    length = ML_CHUNK
    n_lat = SEQ // length
    ctx0 = M_LAT // CTX_LEN

    def lat_spec(col):
        return pl.BlockSpec((SEQ, hd), lambda b, h: (b, col * ML_HEADS + h))

    def ctx_spec(col):
        return pl.BlockSpec((CTX_LEN, hd), lambda b, h: (ctx0 + b, col * ML_HEADS + h))

    return pl.pallas_call(
        _mlstm_kernel,
        grid=(BATCH, ML_HEADS),
        in_specs=[lat_spec(0), lat_spec(1), lat_spec(2), lat_spec(3),
                  ctx_spec(0), ctx_spec(1), ctx_spec(2), ctx_spec(3),
                  pl.BlockSpec((None, None, n_lat, 4, length), lambda b, h: (b, h, 0, 0, 0)),
                  pl.BlockSpec((None, None, n_lat, length, 4), lambda b, h: (b, h, 0, 0, 0)),
                  pl.BlockSpec((None, None, 4, length), lambda b, h: (b, h, 0, 0)),
                  pl.BlockSpec((None, None, length, 4), lambda b, h: (b, h, 0, 0)),
                  pl.BlockSpec((1, hd), lambda b, h: (0, h))],
        out_specs=[pl.BlockSpec((SEQ, hd), lambda b, h: (b, h)),
                   pl.BlockSpec((CTX_LEN, hd), lambda b, h: (b, h))],
        out_shape=[jax.ShapeDtypeStruct((M_LAT, ML_WIDTH), BF16),
                   jax.ShapeDtypeStruct((M_CTX, ML_WIDTH), BF16)],
        scratch_shapes=[pltpu.VMEM((1 + n_lat, length, hd), BF16),
                        pltpu.VMEM((1 + n_lat, hd, length), BF16),
                        pltpu.VMEM((1 + n_lat, length, hd), BF16),
                        pltpu.VMEM((1 + n_lat, length, length), F32),
                        pltpu.VMEM((CTX_LEN + SEQ, hd), F32)],
        compiler_params=_params(("parallel", "parallel")),
        name="mlstm",
    )(p_ml, p_ml, p_ml, p_ml, p_ml, p_ml, p_ml, p_ml, g_row_lat, g_col_lat, g_row_ctx, g_col_ctx, gain)


def _rope_partner(x, half):
    lane = lax.broadcasted_iota(I32, x.shape, 1)
    first = (lane & (2 * half - 1)) < half
    return jnp.where(first, pltpu.roll(x, LANES - half, 1), pltpu.roll(x, half, 1))


def _ga_prep_kernel(h_ref, w_ref, gq_ref, gk_ref, cos_ref, sin_ref, q_ref, k_ref, v_ref):
    p = _dot(h_ref[...], w_ref[...])
    cos = cos_ref[...]
    sin = sin_ref[...]
    dh = GA_HEAD_DIM

    def norm_rope(x, gain):
        y = _rms(x) * gain
        return y * cos + pltpu.roll(y, dh // 2, 1) * sin

    for h in range(GA_HEADS):
        q = norm_rope(p[:, h * dh:(h + 1) * dh], gq_ref[...])
        q_ref[h] = (q * dh ** -0.5).astype(BF16)
    for g in range(GA_KV_HEADS):
        k0 = GA_WIDTH + g * dh
        v0 = GA_WIDTH + GA_KV_WIDTH + g * dh
        k_ref[g] = norm_rope(p[:, k0:k0 + dh], gk_ref[...]).astype(BF16)
        v_ref[g] = p[:, v0:v0 + dh].astype(BF16)


def _ga_prep(h, w_ga, gain_q, gain_k, cos, sin, tm=512):
    d = h.shape[1]
    n = w_ga.shape[1]
    dh = GA_HEAD_DIM

    def head_spec(n_heads):
        return pl.BlockSpec((n_heads, tm, dh), lambda i: (0, i, 0))

    return pl.pallas_call(
        _ga_prep_kernel,
        grid=(M_ALL // tm,),
        in_specs=[pl.BlockSpec((tm, d), lambda i: (i, 0)),
                  pl.BlockSpec((d, n), lambda i: (0, 0)),
                  pl.BlockSpec((1, dh), lambda i: (0, 0)),
                  pl.BlockSpec((1, dh), lambda i: (0, 0)),
                  pl.BlockSpec((tm, dh), lambda i: (i, 0)),
                  pl.BlockSpec((tm, dh), lambda i: (i, 0))],
        out_specs=[head_spec(GA_HEADS), head_spec(GA_KV_HEADS), head_spec(GA_KV_HEADS)],
        out_shape=[jax.ShapeDtypeStruct((GA_HEADS, M_ALL, dh), BF16),
                   jax.ShapeDtypeStruct((GA_KV_HEADS, M_ALL, dh), BF16),
                   jax.ShapeDtypeStruct((GA_KV_HEADS, M_ALL, dh), BF16)],
        compiler_params=_params(("parallel",)),
        name="ga_prep",
    )(h, w_ga, gain_q, gain_k, cos, sin)


def _la_prep_kernel(h_ref, wsm_ref, gq_ref, gkv_ref, wuq_ref, wukv_ref, cos_ref, sin_ref,
                    q_ref, k_ref, v_ref, tail_ref):
    p = _dot(h_ref[...], wsm_ref[...])
    tail = p[:, SM_TAIL:SM_TAIL + LANES]
    tail_ref[...] = tail
    cq = (_rms(p[:, :LA_Q_RANK]) * gq_ref[...]).astype(BF16)
    ckv = (_rms(p[:, SM_CKV:SM_CKV + LA_KV_RANK]) * gkv_ref[...]).astype(BF16)
    qf = _dot(cq, wuq_ref[...])
    kvf = _dot(ckv, wukv_ref[...])
    cos = cos_ref[...]
    sin = sin_ref[...]
    scale = (LA_NOPE + LA_ROPE) ** -0.5
    low = lax.broadcasted_iota(I32, tail.shape, 1) < LA_ROPE

    def rope(x):
        return x * cos + _rope_partner(x, LA_ROPE // 4) * sin

    kr = rope(tail)
    kr2 = jnp.where(low, kr, pltpu.roll(kr, LA_ROPE, 1)).astype(BF16)
    heads_per_tile = LANES // LA_ROPE
    for j in range(LA_HEADS // heads_per_tile):
        c0 = LA_HEADS * LA_NOPE + j * LANES
        qr = rope(qf[:, c0:c0 + LANES]) * scale
        q_ref[2 * j, :, LA_NOPE:] = jnp.where(low, qr, 0.0).astype(BF16)
        q_ref[2 * j + 1, :, LA_NOPE:] = jnp.where(low, 0.0, qr).astype(BF16)
    for h in range(LA_HEADS):
        q_ref[h, :, :LA_NOPE] = (qf[:, h * LA_NOPE:(h + 1) * LA_NOPE] * scale).astype(BF16)
        kv0 = h * (LA_NOPE + LA_VDIM)
        k_ref[h, :, :LA_NOPE] = kvf[:, kv0:kv0 + LA_NOPE].astype(BF16)
        k_ref[h, :, LA_NOPE:] = kr2
        v_ref[h] = kvf[:, kv0 + LA_NOPE:kv0 + LA_NOPE + LA_VDIM].astype(BF16)


def _la_prep(h, w_small, gain_q, gain_kv, w_uq, w_ukv, cos, sin, tm=512):
    assert LA_HEADS % (LANES // LA_ROPE) == 0 and LA_NOPE == LANES and LA_DK == LA_NOPE + LANES
    d = h.shape[1]

    def full(a):
        return pl.BlockSpec(a.shape, lambda i: (0,) * a.ndim)

    return pl.pallas_call(
        _la_prep_kernel,
        grid=(M_ALL // tm,),
        in_specs=[pl.BlockSpec((tm, d), lambda i: (i, 0)), full(w_small), full(gain_q), full(gain_kv),
                  full(w_uq), full(w_ukv),
                  pl.BlockSpec((tm, LANES), lambda i: (i, 0)),
                  pl.BlockSpec((tm, LANES), lambda i: (i, 0))],
        out_specs=[pl.BlockSpec((LA_HEADS, tm, LA_DK), lambda i: (0, i, 0)),
                   pl.BlockSpec((LA_HEADS, tm, LA_DK), lambda i: (0, i, 0)),
                   pl.BlockSpec((LA_HEADS, tm, LA_VDIM), lambda i: (0, i, 0)),
                   pl.BlockSpec((tm, LANES), lambda i: (i, 0))],
        out_shape=[jax.ShapeDtypeStruct((LA_HEADS, M_ALL, LA_DK), BF16),
                   jax.ShapeDtypeStruct((LA_HEADS, M_ALL, LA_DK), BF16),
                   jax.ShapeDtypeStruct((LA_HEADS, M_ALL, LA_VDIM), BF16),
                   jax.ShapeDtypeStruct((M_ALL, LANES), F32)],
        compiler_params=_params(("parallel",)),
        name="la_prep",
    )(h, w_small, gain_q, gain_kv, w_uq, w_ukv, cos, sin)


ATT_TQ = 256
ATT_ROWS = 1024


def _attn_kernel(q_ref, kl_ref, vl_ref, kc_ref, vc_ref, kca_ref, vca_ref, o_ref, *, n_lat_blocks):
    i = pl.program_id(1)

    @pl.when(i < n_lat_blocks)
    def _():
        n_sub = ATT_ROWS // ATT_TQ

        def scores(sub):
            q = q_ref[sub * ATT_TQ:(sub + 1) * ATT_TQ, :]
            return _dot_nt(q, kl_ref[...]), _dot_nt(q, kc_ref[...])

        def softmax(s_l, s_c):
            mx = jnp.maximum(jnp.max(s_l, axis=1, keepdims=True), jnp.max(s_c, axis=1, keepdims=True))
            p_l = jnp.exp(s_l - mx)
            p_c = jnp.exp(s_c - mx)
            den = jnp.sum(p_l, axis=1, keepdims=True) + jnp.sum(p_c, axis=1, keepdims=True)
            return p_l.astype(BF16), p_c.astype(BF16), den

        def values(sub, p_l, p_c, den):
            o = _dot(p_l, vl_ref[...]) + _dot(p_c, vc_ref[...])
            o_ref[sub * ATT_TQ:(sub + 1) * ATT_TQ, :] = (o / den).astype(o_ref.dtype)

        pending_s = scores(0)
        for sub in range(n_sub):
            nxt = scores(sub + 1) if sub + 1 < n_sub else None
            values(sub, *softmax(*pending_s))
            pending_s = nxt

    @pl.when(i >= n_lat_blocks)
    def _():
        for sub in range(ATT_ROWS // CTX_LEN):
            rows = slice(sub * CTX_LEN, (sub + 1) * CTX_LEN)
            s = _dot_nt(q_ref[rows, :], kca_ref[rows, :])
            mx = jnp.max(s, axis=1, keepdims=True)
            p = jnp.exp(s - mx)
            den = jnp.sum(p, axis=1, keepdims=True)
            o = _dot(p.astype(BF16), vca_ref[rows, :])
            o_ref[rows, :] = (o / den).astype(o_ref.dtype)


def _attention(q, k, v, kv_group, need_ctx, name):
    assert M_CTX == ATT_ROWS and ATT_ROWS % ATT_TQ == 0
    n_heads, _, dk = q.shape
    dv = v.shape[2]
    n_lat_blocks = M_LAT // ATT_ROWS
    per_sample = SEQ // ATT_ROWS
    ctx0 = M_LAT // CTX_LEN
    n_blocks = n_lat_blocks + (1 if need_ctx else 0)

    def sample(i):
        return jnp.minimum(i // per_sample, BATCH - 1)

    def lat_spec(dim):
        return pl.BlockSpec((None, SEQ, dim), lambda h, i: (h // kv_group, sample(i), 0))

    def ctx_spec(dim):
        return pl.BlockSpec((None, CTX_LEN, dim), lambda h, i: (h // kv_group, ctx0 + sample(i), 0))

    def ctx_all_spec(dim):
        return pl.BlockSpec((None, M_CTX, dim), lambda h, i: (h // kv_group, n_lat_blocks, 0))

    return pl.pallas_call(
        functools.partial(_attn_kernel, n_lat_blocks=n_lat_blocks),
        grid=(n_heads, n_blocks),
        in_specs=[pl.BlockSpec((None, ATT_ROWS, dk), lambda h, i: (h, i, 0)),
                  lat_spec(dk), lat_spec(dv), ctx_spec(dk), ctx_spec(dv), ctx_all_spec(dk), ctx_all_spec(dv)],
        out_specs=pl.BlockSpec((ATT_ROWS, dv), lambda h, i: (i, h)),
        out_shape=jax.ShapeDtypeStruct((n_blocks * ATT_ROWS, n_heads * dv), BF16),
        compiler_params=_params(("parallel", "arbitrary")),
        name=name,
    )(q, k, v, k, v, k, v)


def _router_kernel(h_ref, w_ref, o_ref):
    logits = _dot_nt(w_ref[...], h_ref[...])
    mx = jnp.max(logits, axis=0, keepdims=True)
    e = jnp.exp(logits - mx)
    o_ref[...] = e / jnp.sum(e, axis=0, keepdims=True)


def _router(h, w_router_t, m_rows, tm=1024):
    d = h.shape[1]
    return pl.pallas_call(
        _router_kernel,
        grid=(m_rows // tm,),
        in_specs=[pl.BlockSpec((tm, d), lambda i: (i, 0)),
                  pl.BlockSpec((N_EXPERTS, d), lambda i: (0, 0))],
        out_specs=pl.BlockSpec((N_EXPERTS, tm), lambda i: (0, i)),
        out_shape=jax.ShapeDtypeStruct((N_EXPERTS, m_rows), F32),
        compiler_params=_params(("parallel",)),
        name="router",
    )(h, w_router_t)


PREFIX_BLOCK = 256


def _exclusive_prefix(x_bf, upper):
    n_tok = x_bf.shape[1]
    carry = jnp.zeros((x_bf.shape[0], 1), F32)
    parts = []
    for blk in range(n_tok // PREFIX_BLOCK):
        xb = x_bf[:, blk * PREFIX_BLOCK:(blk + 1) * PREFIX_BLOCK]
        parts.append(_dot(xb, upper) + carry)
        carry = carry + jnp.sum(xb.astype(F32), axis=1, keepdims=True)
    return parts[0] if len(parts) == 1 else jnp.concatenate(parts, axis=1)


def _bitonic_sort_descending(tiles):
    n_tiles = len(tiles)
    n = n_tiles * LANES
    lane = lax.broadcasted_iota(I32, tiles[0].shape, 1)
    k = 2
    while k <= n:
        j = k // 2
        while j >= 1:
            new_tiles = []
            for t in range(n_tiles):
                x = tiles[t]
                if j < LANES:
                    low = (lane & j) == 0
                    partner = jnp.where(low, pltpu.roll(x, LANES - j, 1), pltpu.roll(x, j, 1))
                else:
                    partner = tiles[t ^ (j // LANES)]
                big = jnp.maximum(x, partner)
                small = jnp.minimum(x, partner)
                up = ((t * LANES) & k) == 0
                if j >= LANES:
                    take_big = (((t * LANES) & j) == 0) == up
                    new_tiles.append(big if take_big else small)
                elif k >= LANES:
                    new_tiles.append(jnp.where(low, big, small) if up else jnp.where(low, small, big))
                else:
                    agree = (lane & j) * (k // j) == (lane & k)
                    new_tiles.append(jnp.where(agree, big, small))
            tiles = new_tiles
            j //= 2
        k *= 2
    return tiles


def _topk_kernel(aff_ref, rank_ref, *, cap):
    a = aff_ref[...]
    n_tok = a.shape[1]
    ordered = _bitonic_sort_descending([a[:, t * LANES:(t + 1) * LANES] for t in range(n_tok // LANES)])
    pos = cap - 1
    thr = ordered[pos // LANES][:, pos % LANES:pos % LANES + 1]
    gt = a > thr
    eq = a == thr
    need = cap - jnp.sum(jnp.where(gt, 1.0, 0.0), axis=1, keepdims=True)
    ri = lax.broadcasted_iota(I32, (PREFIX_BLOCK, PREFIX_BLOCK), 0)
    ci = lax.broadcasted_iota(I32, (PREFIX_BLOCK, PREFIX_BLOCK), 1)
    upper = jnp.where(ri < ci, 1.0, 0.0).astype(BF16)
    eq_before = _exclusive_prefix(jnp.where(eq, 1.0, 0.0).astype(BF16), upper)
    sel = gt | (eq & (eq_before < need))
    slot = _exclusive_prefix(jnp.where(sel, 1.0, 0.0).astype(BF16), upper)
    rank_ref[...] = jnp.where(sel, slot.astype(I32), -1)


def _topk(aff, n_tok, cap, blk0, n_sets):
    return pl.pallas_call(
        functools.partial(_topk_kernel, cap=cap),
        grid=(n_sets,),
        in_specs=[pl.BlockSpec((N_EXPERTS, n_tok), lambda s: (0, blk0 + s))],
        out_specs=pl.BlockSpec((N_EXPERTS, n_tok), lambda s: (0, s)),
        out_shape=jax.ShapeDtypeStruct((N_EXPERTS, n_sets * n_tok), I32),
        compiler_params=_params(("parallel",)),
        name=f"topk_{n_tok}",
    )(aff)


def _gather_kernel(rank_ref, aff_ref, h_ref, *rest, cap):
    xg_ref, val_ref = rest[-2:]
    r = rank_ref[...]
    n_tok = r.shape[1]
    slot = lax.broadcasted_iota(I32, (cap, n_tok), 0)
    hit = r == slot
    onehot = jnp.where(hit, 1.0, 0.0).astype(BF16)
    xg_ref[...] = _dot(onehot, h_ref[...]).astype(xg_ref.dtype)
    val_ref[...] = jnp.sum(jnp.where(hit, aff_ref[...], 0.0), axis=1, keepdims=True)


def _gather(rank3, aff3, h, n_tok, cap, blk0, row0, rows_total, prev):
    d = h.shape[1]
    rb0 = row0 // cap
    in_specs = [pl.BlockSpec((None, 1, n_tok), lambda b, e: (e, 0, b)),
                pl.BlockSpec((None, 1, n_tok), lambda b, e: (e, 0, blk0 + b)),
                pl.BlockSpec((n_tok, d), lambda b, e: (blk0 + b, 0))]
    args = [rank3, aff3, h]
    aliases = {}
    if prev is not None:
        in_specs += [pl.BlockSpec(memory_space=pl.ANY), pl.BlockSpec(memory_space=pl.ANY)]
        args += list(prev)
        aliases = {3: 0, 4: 1}
    return pl.pallas_call(
        functools.partial(_gather_kernel, cap=cap),
        grid=(BATCH, N_EXPERTS),
        in_specs=in_specs,
        out_specs=[pl.BlockSpec((None, cap, d), lambda b, e: (e, rb0 + b, 0)),
                   pl.BlockSpec((None, cap, 1), lambda b, e: (e, rb0 + b, 0))],
        out_shape=[jax.ShapeDtypeStruct((N_EXPERTS, rows_total, d), BF16),
                   jax.ShapeDtypeStruct((N_EXPERTS, rows_total, 1), F32)],
        input_output_aliases=aliases,
        compiler_params=_params(("parallel", "arbitrary")),
        name=f"moe_gather_{n_tok}",
    )(*args)


EXPERT_COLS = 512


def _expert_kernel(x_ref, w1_ref, w3_ref, w2_ref, val_ref, y_ref, acc_ref):
    f = pl.program_id(1)
    last = pl.num_programs(1) - 1
    col_slices = [slice(c0, c0 + EXPERT_COLS) for c0 in range(0, acc_ref.shape[1], EXPERT_COLS)]

    def hidden():
        x = x_ref[...]
        a = _dot(x, w1_ref[...].astype(BF16))
        u = _dot(x, w3_ref[...].astype(BF16))
        return (a * jax.nn.sigmoid(a) * u).astype(BF16)

    def down(hm, cols):
        return _dot(hm, w2_ref[:, cols].astype(BF16))

    @pl.when(f == 0)
    def _():
        hm = hidden()
        for cols in col_slices:
            acc_ref[:, cols] = down(hm, cols)

    @pl.when((f > 0) & (f < last))
    def _():
        hm = hidden()
        for cols in col_slices:
            acc_ref[:, cols] += down(hm, cols)

    @pl.when(f == last)
    def _():
        hm = hidden()
        val = val_ref[...]
        for cols in col_slices:
            y_ref[:, cols] = ((acc_ref[:, cols] + down(hm, cols)) * val).astype(y_ref.dtype)


def _experts(xg, vals, w_e1, w_e3, w_e2, layer, tf=256):
    n_exp, rows, d = xg.shape
    ff = w_e1.shape[3]
    assert ff // tf >= 2
    return pl.pallas_call(
        _expert_kernel,
        grid=(n_exp, ff // tf),
        in_specs=[pl.BlockSpec((None, rows, d), lambda e, f: (e, 0, 0)),
                  pl.BlockSpec((None, None, d, tf), lambda e, f: (layer, e, 0, f)),
                  pl.BlockSpec((None, None, d, tf), lambda e, f: (layer, e, 0, f)),
                  pl.BlockSpec((None, None, tf, d), lambda e, f: (layer, e, f, 0)),
                  pl.BlockSpec((None, rows, 1), lambda e, f: (e, 0, 0))],
        out_specs=pl.BlockSpec((None, rows, d), lambda e, f: (e, 0, 0)),
        out_shape=jax.ShapeDtypeStruct((n_exp, rows, d), BF16),
        scratch_shapes=[pltpu.VMEM((rows, d), F32)],
        compiler_params=_params(("parallel", "arbitrary")),
        name="moe_experts",
    )(xg, w_e1, w_e3, w_e2, vals)


COMBINE_LANES = 1024


def _combine_kernel(rank_ref, y_ref, x_ref, g_ref, *rest, cap):
    o_ref, lhs_ref = rest[-2:]
    n_slots = N_EXPERTS * cap
    chunk = min(COMBINE_LANES, n_slots)
    shift = cap.bit_length() - 1

    @pl.when(pl.program_id(2) == 0)
    def _():
        r = rank_ref[...].astype(F32).astype(BF16)
        tm = r.shape[0]
        for c0 in range(0, n_slots, chunk):
            lane = c0 + lax.broadcasted_iota(I32, (LANES, chunk), 1)
            row = lax.broadcasted_iota(I32, (LANES, chunk), 0)
            expand = jnp.where(lax.shift_right_logical(lane, shift) == row, 1.0, 0.0).astype(BF16)
            r_exp = _dot(r, expand)
            slot = (c0 + lax.broadcasted_iota(I32, (tm, chunk), 1)) & (cap - 1)
            lhs_ref[:, c0:c0 + chunk] = jnp.where(r_exp == slot.astype(F32), 1.0, 0.0).astype(BF16)

    y = y_ref[...]
    o_ref[...] = x_ref[...] + g_ref[...] * _dot(lhs_ref[...], y.reshape(n_slots, y.shape[2]))


def _combine(rank_t, y, x, gate_tab, out_prev, n_tok, cap, blk0, row0, tm, tn=1024):
    assert cap & (cap - 1) == 0
    d = x.shape[1]
    tpb = n_tok // tm
    xrow0 = blk0 * tpb
    rb0 = row0 // cap

    def grp(b):
        return b if n_tok == SEQ else BATCH

    in_specs = [pl.BlockSpec((tm, LANES), lambda b, i, j: (b * tpb + i, 0)),
                pl.BlockSpec((N_EXPERTS, cap, tn), lambda b, i, j: (0, rb0 + b, j)),
                pl.BlockSpec((tm, tn), lambda b, i, j: (xrow0 + b * tpb + i, j)),
                pl.BlockSpec((None, 1, tn), lambda b, i, j: (grp(b), 0, j))]
    args = [rank_t, y, x, gate_tab]
    aliases = {}
    if out_prev is not None:
        in_specs.append(pl.BlockSpec(memory_space=pl.ANY))
        args.append(out_prev)
        aliases = {4: 0}
    return pl.pallas_call(
        functools.partial(_combine_kernel, cap=cap),
        grid=(BATCH, tpb, d // tn),
        in_specs=in_specs,
        out_specs=pl.BlockSpec((tm, tn), lambda b, i, j: (xrow0 + b * tpb + i, j)),
        out_shape=jax.ShapeDtypeStruct(x.shape, F32),
        scratch_shapes=[pltpu.VMEM((tm, N_EXPERTS * cap), BF16)],
        input_output_aliases=aliases,
        compiler_params=_params(("parallel", "parallel", "arbitrary")),
        name=f"moe_combine_{n_tok}",
    )(*args)


def _expert_choice(h2, x, gate_tab, w_router_t, w_e1, w_e3, w_e2, layer, need_ctx):
    m_rows = h2.shape[0]
    aff = _router(h2, w_router_t, m_rows)
    aff3 = aff.reshape(N_EXPERTS, 1, m_rows)
    sets = [(SEQ, 0)]
    if need_ctx:
        sets.append((CTX_LEN, M_LAT // CTX_LEN))
    caps = [CAPACITY_FACTOR * n_tok // N_EXPERTS for n_tok, _ in sets]
    rows_total = sum(BATCH * cap for cap in caps)
    ranks, gathered, row0 = [], None, 0
    for (n_tok, blk0), cap in zip(sets, caps):
        rank = _topk(aff, n_tok, cap, blk0, BATCH)
        gathered = _gather(rank.reshape(N_EXPERTS, 1, BATCH * n_tok), aff3, h2, n_tok, cap, blk0, row0, rows_total,
                           gathered)
        ranks.append((rank, row0))
        row0 += BATCH * cap
    y = _experts(gathered[0], gathered[1], w_e1, w_e3, w_e2, layer)
    out = None
    for (n_tok, blk0), cap, (rank, row0) in zip(sets, caps, ranks):
        rank_t = jnp.pad(rank.T, ((0, 0), (0, LANES - N_EXPERTS)))
        out = _combine(rank_t, y, x, gate_tab, out, n_tok, cap, blk0, row0, tm=min(512, n_tok))
    return out


def _rope_tables(dim):
    rows = SEQ // GRID_W
    quarter = dim // 4
    inv = ROPE_BASE ** (-jnp.arange(quarter, dtype=F32) / quarter)
    row = jnp.broadcast_to(jnp.arange(rows, dtype=F32)[:, None], (rows, GRID_W)).reshape(-1)
    col = jnp.broadcast_to(jnp.arange(GRID_W, dtype=F32)[None, :], (rows, GRID_W)).reshape(-1)
    ang_r = row[:, None] * inv[None, :]
    ang_c = col[:, None] * inv[None, :]
    cos = jnp.concatenate([jnp.cos(ang_r), jnp.cos(ang_r), jnp.cos(ang_c), jnp.cos(ang_c)], axis=-1)
    sin = jnp.concatenate([-jnp.sin(ang_r), jnp.sin(ang_r), -jnp.sin(ang_c), jnp.sin(ang_c)], axis=-1)
    cos = jnp.tile(cos, (BATCH, LANES // dim))
    sin = jnp.tile(sin, (BATCH, LANES // dim))
    cos = jnp.concatenate([cos, jnp.ones((M_CTX, LANES), F32)], axis=0)
    sin = jnp.concatenate([sin, jnp.zeros((M_CTX, LANES), F32)], axis=0)
    return cos, sin


def _halves_major(a):
    quarter = a.shape[-1] // 4
    a4 = a.reshape(*a.shape[:-1], 2, 2, quarter)
    return jnp.swapaxes(a4, -3, -2).reshape(a.shape)


def _layer(x, mods, layer, w_in, ml_gate_bias, ml_norm, la_q_norm, la_kv_norm, la_w_uq, la_w_ukv,
           ga_q_norm, ga_k_norm, w_br_ml, w_br_la, w_br_ga, w_out, w_router, w_e1, w_e3, w_e2, ropes, need_ctx):
    d = D_MODEL
    m_rows = M_ALL if need_ctx else M_LAT
    sh1, sc1, g1, sh2, sc2, g2 = [mods[:N_GROUPS, k * d:(k + 1) * d].reshape(N_GROUPS, 1, d) for k in range(6)]
    (cos_la, sin_la), (cos_ga, sin_ga) = ropes

    w_ml = w_in[:, :OFF_MLG].astype(BF16)
    w_small = jnp.concatenate([w_in[:, OFF_CQ:OFF_GAQ], w_in[:, OFF_MLG:OFF_CQ],
                               jnp.zeros((d, SMALL_WIDTH - (OFF_GAQ - OFF_MLG)), F32)], axis=1).astype(BF16)
    n_qk = GA_HEADS + GA_KV_HEADS
    w_ga = jnp.concatenate([_halves_major(w_in[:, OFF_GAQ:OFF_GAV].reshape(d, n_qk, GA_HEAD_DIM)).reshape(d, -1),
                            w_in[:, OFF_GAV:OFF_GATE]], axis=1).astype(BF16)
    w_gate = w_in[:, OFF_GATE:].astype(BF16)
    w_uq = la_w_uq.reshape(LA_Q_RANK, LA_HEADS, LA_NOPE + LA_ROPE)
    w_uq = jnp.concatenate([w_uq[:, :, :LA_NOPE].reshape(LA_Q_RANK, LA_HEADS * LA_NOPE),
                            w_uq[:, :, LA_NOPE:].reshape(LA_Q_RANK, LA_HEADS * LA_ROPE)], axis=1).astype(BF16)

    h = _rms_mod(x, 1.0 + sc1, sh1, M_ALL)
    p_ml = _mm(h, w_ml, name="proj_ml")
    q_la, k_la, v_la, tail = _la_prep(h, w_small, la_q_norm.reshape(1, LA_Q_RANK), la_kv_norm.reshape(1, LA_KV_RANK),
                                      w_uq, la_w_ukv.astype(BF16), cos_la, sin_la)
    q_ga, k_ga, v_ga = _ga_prep(h, w_ga, _halves_major(ga_q_norm.reshape(1, GA_HEAD_DIM)),
                                _halves_major(ga_k_norm.reshape(1, GA_HEAD_DIM)), cos_ga, sin_ga)

    gates = (tail[:, TAIL_MLG:TAIL_MLG + ML_GATES] + ml_gate_bias[None, :]).reshape(M_ALL, 2, 2, ML_HEADS)
    gates = jnp.stack([gates[:, 0, 0], jax.nn.log_sigmoid(gates[:, 0, 1]),
                       gates[:, 1, 0], jax.nn.log_sigmoid(gates[:, 1, 1])], axis=-1)
    n_lat = SEQ // ML_CHUNK
    g_lat = gates[:M_LAT].reshape(BATCH, n_lat, ML_CHUNK, ML_HEADS, 4)
    g_ctx = gates[M_LAT:].reshape(BATCH, CTX_LEN, ML_HEADS, 4)
    y_ml_lat, y_ml_ctx = _mlstm(p_ml, jnp.transpose(g_lat, (0, 3, 1, 4, 2)), jnp.transpose(g_lat, (0, 3, 1, 2, 4)),
                                jnp.transpose(g_ctx, (0, 2, 3, 1)), jnp.transpose(g_ctx, (0, 2, 1, 3)),
                                ml_norm.reshape(1, ML_WIDTH))

    y_la = _attention(q_la, k_la, v_la, 1, need_ctx, "attn_mla")
    y_ga = _attention(q_ga, k_ga, v_ga, GA_HEADS // GA_KV_HEADS, need_ctx, "attn_gqa")

    merged = _merge(h, y_ml_lat, y_ml_ctx, y_la, y_ga, w_gate, w_br_ml.astype(BF16), w_br_la.astype(BF16),
                    w_br_ga.astype(BF16), m_rows)
    x = _mm_residual(merged, w_out.astype(BF16), x, g1, m_rows, name="out_proj")

    h2 = _rms_mod(x, 1.0 + sc2, sh2, m_rows)
    return _expert_choice(h2, x, g2, w_router.T.astype(BF16), w_e1, w_e3, w_e2, layer, need_ctx)


def kernel(x, c, ctx, c_ctx, w_mod, b_mod, w_in, ml_gate_bias, ml_norm, la_q_norm, la_kv_norm, la_w_uq, la_w_ukv,
           ga_q_norm, ga_k_norm, w_br_ml, w_br_la, w_br_ga, w_out, w_router, w_e1, w_e3, w_e2, final_norm):
    d = D_MODEL
    xs = jnp.concatenate([x.reshape(M_LAT, d), ctx.reshape(M_CTX, d)], axis=0)
    cc = jnp.concatenate([c, c_ctx[None, :], jnp.zeros((MODS_ROWS - N_GROUPS, d), F32)], axis=0)
    mods = _mods(cc, w_mod, b_mod.reshape(DEPTH, 1, 6 * d))
    ropes = (_rope_tables(LA_ROPE), tuple(_halves_major(t) for t in _rope_tables(GA_HEAD_DIM)))
    for l in range(DEPTH):
        need_ctx = l < DEPTH - 1
        xs = _layer(xs, mods[l], l, w_in[l], ml_gate_bias[l], ml_norm[l], la_q_norm[l], la_kv_norm[l],
                    la_w_uq[l], la_w_ukv[l], ga_q_norm[l], ga_k_norm[l], w_br_ml[l], w_br_la[l], w_br_ga[l],
                    w_out[l], w_router[l], w_e1, w_e3, w_e2, ropes, need_ctx)
    out = _rms_gain(xs, final_norm.reshape(1, d), F32, name="final_norm")
    return out.reshape(BATCH, SEQ, d)
```

```python
import functools

import jax
import jax.numpy as jnp
from jax import lax
from jax.experimental import pallas as pl
from jax.experimental.pallas import tpu as pltpu

F32 = jnp.float32
BF16 = jnp.bfloat16
I32 = jnp.int32

D_MODEL = 2048
BATCH = 4
SEQ = 2048
DEPTH = 2
GRID_W = 64
CTX_LEN = 256
EPS = 1e-6
ROPE_BASE = 10000.0

ML_HEADS = 4
ML_HEAD_DIM = 256
ML_WIDTH = ML_HEADS * ML_HEAD_DIM
ML_GATES = 2 * 2 * ML_HEADS
ML_CHUNK = 256

LA_HEADS = 8
LA_NOPE = 128
LA_ROPE = 64
LA_VDIM = 128
LA_Q_RANK = 512
LA_KV_RANK = 256
LA_DK = 256

GA_HEADS = 8
GA_KV_HEADS = 2
GA_HEAD_DIM = 128
GA_WIDTH = GA_HEADS * GA_HEAD_DIM
GA_KV_WIDTH = GA_KV_HEADS * GA_HEAD_DIM

N_EXPERTS = 16
EXPERT_FF = 1024
CAPACITY_FACTOR = 2

M_LAT = BATCH * SEQ
M_CTX = BATCH * CTX_LEN
M_ALL = M_LAT + M_CTX
MODS_ROWS = 16
N_GROUPS = BATCH + 1

LANES = 128

OFF_MLG = 4 * ML_WIDTH
OFF_CQ = OFF_MLG + ML_GATES
OFF_CKV = OFF_CQ + LA_Q_RANK
OFF_KR = OFF_CKV + LA_KV_RANK
OFF_GAQ = OFF_KR + LA_ROPE
OFF_GAK = OFF_GAQ + GA_WIDTH
OFF_GAV = OFF_GAK + GA_KV_WIDTH
OFF_GATE = OFF_GAV + GA_KV_WIDTH
SM_CKV = LA_Q_RANK
SM_TAIL = SM_CKV + LA_KV_RANK
SMALL_WIDTH = SM_TAIL + LANES
TAIL_MLG = LA_ROPE

VMEM_LIMIT = 56 * 1024 * 1024


def _params(semantics, vmem=VMEM_LIMIT):
    return pltpu.CompilerParams(dimension_semantics=semantics, vmem_limit_bytes=vmem)


def _dot(a, b):
    return jnp.dot(a, b, preferred_element_type=F32)


def _dot_nt(a, b):
    return lax.dot_general(a, b, (((1,), (1,)), ((), ())), preferred_element_type=F32)


def _rms(x):
    return x * lax.rsqrt(jnp.mean(x * x, axis=-1, keepdims=True) + EPS)


def _group_of_block(tm):
    return lambda i: (i * tm) // SEQ


def _mods_kernel(c_ref, w_ref, b_ref, o_ref):
    c = c_ref[...]
    a = (c * jax.nn.sigmoid(c)).astype(BF16)
    part = _dot(a, w_ref[...].astype(BF16))

    @pl.when(pl.program_id(1) == 0)
    def _():
        o_ref[...] = part + b_ref[...]

    @pl.when(pl.program_id(1) > 0)
    def _():
        o_ref[...] += part


def _mods(cc, w_mod, b_mod, tk=256):
    depth, d, n = w_mod.shape
    rows = cc.shape[0]
    return pl.pallas_call(
        _mods_kernel,
        grid=(depth, d // tk),
        in_specs=[pl.BlockSpec((rows, tk), lambda l, k: (0, k)),
                  pl.BlockSpec((None, tk, n), lambda l, k: (l, k, 0)),
                  pl.BlockSpec((None, 1, n), lambda l, k: (l, 0, 0))],
        out_specs=pl.BlockSpec((None, rows, n), lambda l, k: (l, 0, 0)),
        out_shape=jax.ShapeDtypeStruct((depth, rows, n), F32),
        compiler_params=_params(("parallel", "arbitrary")),
        name="mods",
    )(cc, w_mod, b_mod)


def _row_sources(x_lat, x_ctx, tm, n_grid_axes=1):
    d = x_lat.shape[1]
    n_lat_tiles = M_LAT // tm

    def spec(fn):
        return pl.BlockSpec((tm, d), (lambda i: (fn(i), 0)) if n_grid_axes == 1 else (lambda j, i: (fn(i), 0)))

    if x_ctx is None:
        return [spec(lambda i: i)], [x_lat], lambda i, refs: refs[0][...]
    specs = [spec(lambda i: jnp.minimum(i, n_lat_tiles - 1)), spec(lambda i: jnp.maximum(i - n_lat_tiles, 0))]
    return specs, [x_lat, x_ctx], lambda i, refs: jnp.where(i < n_lat_tiles, refs[0][...], refs[1][...])


def _rms_kernel(*refs, pick):
    scale_ref, shift_ref, o_ref = refs[-3:]
    x = pick(pl.program_id(0), refs[:-3])
    o_ref[...] = (_rms(x) * scale_ref[...] + shift_ref[...]).astype(o_ref.dtype)


def _rms_gain_kernel(x_ref, scale_ref, o_ref):
    o_ref[...] = (_rms(x_ref[...]) * scale_ref[...]).astype(o_ref.dtype)


def _rms_mod(x_lat, x_ctx, scale_tab, shift_tab, m_rows, tm=512):
    d = x_lat.shape[1]
    grp = _group_of_block(tm)
    x_specs, x_args, pick = _row_sources(x_lat, x_ctx, tm)
    return pl.pallas_call(
        functools.partial(_rms_kernel, pick=pick),
        grid=(m_rows // tm,),
        in_specs=x_specs + [pl.BlockSpec((None, 1, d), lambda i: (grp(i), 0, 0)),
                            pl.BlockSpec((None, 1, d), lambda i: (grp(i), 0, 0))],
        out_specs=pl.BlockSpec((tm, d), lambda i: (i, 0)),
        out_shape=jax.ShapeDtypeStruct((m_rows, d), BF16),
        compiler_params=_params(("parallel",)),
        name="rms_mod",
    )(*x_args, scale_tab, shift_tab)


def _rms_gain(x, gain, out_dtype, tm=512, name="rms_gain"):
    m_rows, width = x.shape
    return pl.pallas_call(
        _rms_gain_kernel,
        grid=(m_rows // tm,),
        in_specs=[pl.BlockSpec((tm, width), lambda i: (i, 0)),
                  pl.BlockSpec((1, width), lambda i: (0, 0))],
        out_specs=pl.BlockSpec((tm, width), lambda i: (i, 0)),
        out_shape=jax.ShapeDtypeStruct((m_rows, width), out_dtype),
        compiler_params=_params(("parallel",)),
        name=name,
    )(x, gain)


def _mm_kernel(a_ref, wt_ref, o_ref):
    o_ref[...] = _dot_nt(a_ref[...], wt_ref[...]).astype(o_ref.dtype)


def _mm(a, w_t, *, tm=1024, tn=1024, out_dtype=F32, name="mm"):
    m_rows, k = a.shape
    n_cols = w_t.shape[0]
    tm = min(tm, m_rows)
    tn = min(tn, n_cols)
    assert m_rows % tm == 0 and n_cols % tn == 0
    return pl.pallas_call(
        _mm_kernel,
        grid=(n_cols // tn, m_rows // tm),
        in_specs=[pl.BlockSpec((tm, k), lambda j, i: (i, 0)),
                  pl.BlockSpec((tn, k), lambda j, i: (j, 0))],
        out_specs=pl.BlockSpec((tm, tn), lambda j, i: (i, j)),
        out_shape=jax.ShapeDtypeStruct((m_rows, n_cols), out_dtype),
        compiler_params=_params(("parallel", "arbitrary")),
        name=name,
    )(a, w_t)


def _out_proj_kernel(a_ref, w_ref, *refs, pick):
    g_ref, scale_ref, shift_ref, x_out_ref, h_out_ref = refs[-5:]
    x = pick(pl.program_id(0), refs[:-5])
    x_new = x + g_ref[...] * _dot(a_ref[...], w_ref[...])
    x_out_ref[...] = x_new
    h_out_ref[...] = (_rms(x_new) * scale_ref[...] + shift_ref[...]).astype(h_out_ref.dtype)


def _out_proj(a, w, x_lat, x_ctx, gate_tab, scale_tab, shift_tab, m_rows, tm=512):
    k = a.shape[1]
    d = w.shape[1]
    grp = _group_of_block(tm)
    x_specs, x_args, pick = _row_sources(x_lat, x_ctx, tm)
    tab_spec = pl.BlockSpec((None, 1, d), lambda i: (grp(i), 0, 0))
    return pl.pallas_call(
        functools.partial(_out_proj_kernel, pick=pick),
        grid=(m_rows // tm,),
        in_specs=[pl.BlockSpec((tm, k), lambda i: (i, 0)),
                  pl.BlockSpec((k, d), lambda i: (0, 0))] + x_specs + [tab_spec, tab_spec, tab_spec],
        out_specs=[pl.BlockSpec((tm, d), lambda i: (i, 0)), pl.BlockSpec((tm, d), lambda i: (i, 0))],
        out_shape=[jax.ShapeDtypeStruct((m_rows, d), F32), jax.ShapeDtypeStruct((m_rows, d), BF16)],
        compiler_params=_params(("parallel",)),
        name="out_proj",
    )(a, w, *x_args, gate_tab, scale_tab, shift_tab)


def _merge_kernel(h_ref, y0l_ref, y0c_ref, y1_ref, y2_ref, wg0_ref, wg1_ref, wg2_ref, wb0_ref, wb1_ref, wb2_ref,
                  o_ref, *, n_lat_tiles):
    h = h_ref[...]
    y0 = jnp.where(pl.program_id(1) < n_lat_tiles, y0l_ref[...], y0c_ref[...])
    acc = jax.nn.sigmoid(_dot_nt(h, wg0_ref[...])) * _dot(y0, wb0_ref[...])
    acc = acc + jax.nn.sigmoid(_dot_nt(h, wg1_ref[...])) * _dot(y1_ref[...], wb1_ref[...])
    acc = acc + jax.nn.sigmoid(_dot_nt(h, wg2_ref[...])) * _dot(y2_ref[...], wb2_ref[...])
    o_ref[...] = acc.astype(o_ref.dtype)


def _merge(h, y_ml_lat, y_ml_ctx, y_la, y_ga, wt_gate, w_br_ml, w_br_la, w_br_ga, m_rows, tm=512, tn=512):
    d = D_MODEL
    nb = d // tn
    n_lat_tiles = M_LAT // tm
    y_spec = pl.BlockSpec((tm, ML_WIDTH), lambda j, i: (i, 0))
    wb_spec = pl.BlockSpec((ML_WIDTH, tn), lambda j, i: (0, j))
    return pl.pallas_call(
        functools.partial(_merge_kernel, n_lat_tiles=n_lat_tiles),
        grid=(nb, m_rows // tm),
        in_specs=[pl.BlockSpec((tm, d), lambda j, i: (i, 0)),
                  pl.BlockSpec((tm, ML_WIDTH), lambda j, i: (jnp.minimum(i, n_lat_tiles - 1), 0)),
                  pl.BlockSpec((tm, ML_WIDTH), lambda j, i: (jnp.maximum(i - n_lat_tiles, 0), 0)),
                  y_spec, y_spec,
                  pl.BlockSpec((tn, d), lambda j, i: (j, 0)),
                  pl.BlockSpec((tn, d), lambda j, i: (nb + j, 0)),
                  pl.BlockSpec((tn, d), lambda j, i: (2 * nb + j, 0)),
                  wb_spec, wb_spec, wb_spec],
        out_specs=pl.BlockSpec((tm, tn), lambda j, i: (i, j)),
        out_shape=jax.ShapeDtypeStruct((m_rows, d), BF16),
        compiler_params=_params(("parallel", "arbitrary")),
        name="merge",
    )(h, y_ml_lat, y_ml_ctx, y_la, y_ga, wt_gate, wt_gate, wt_gate, w_br_ml, w_br_la, w_br_ga)


def _mlstm_kernel(ql_ref, kl_ref, vl_ref, ol_ref, qc_ref, kc_ref, vc_ref, oc_ref,
                  grl_ref, gcl_ref, grc_ref, gcc_ref, gain_ref, yl_ref, yc_ref,
                  qb_ref, kt_ref, vb_ref, s_ref, hsum_ref):
    length = ML_CHUNK
    hd = ML_HEAD_DIM
    n_lat = SEQ // length
    n_chunks = n_lat + 1
    scale = hd ** -0.5

    def load(c, ctx_ref, lat_ref):
        return ctx_ref[...] if c == 0 else lat_ref[(c - 1) * length:c * length, :]

    for c in range(n_chunks):
        qb = (load(c, qc_ref, ql_ref) * scale).astype(BF16)
        kt = jnp.transpose(load(c, kc_ref, kl_ref)).astype(BF16)
        qb_ref[c] = qb
        kt_ref[c] = kt
        vb_ref[c] = load(c, vc_ref, vl_ref).astype(BF16)
        s_ref[c] = _dot(qb, kt)

    ri = lax.broadcasted_iota(I32, (length, length), 0)
    ci = lax.broadcasted_iota(I32, (length, length), 1)
    for direction in range(2):
        reverse = direction == 1
        row_i = 2 * direction
        allowed = (ci >= ri) if reverse else (ci <= ri)
        allowed_t = (ri >= ci) if reverse else (ri <= ci)
        order = [0] + ([n_lat - j for j in range(n_lat)] if reverse else list(range(1, n_chunks)))
        m = jnp.zeros((1, 1), F32)
        ct = jnp.zeros((hd, hd), F32)
        n_vec = jnp.zeros((1, hd), F32)
        for c in order:
            g_rows = grc_ref[...] if c == 0 else grl_ref[c - 1]
            g_cols = gcc_ref[...] if c == 0 else gcl_ref[c - 1]
            li_r = g_rows[row_i:row_i + 1, :]
            lf_r = g_rows[row_i + 1:row_i + 2, :]
            li_c = g_cols[:, row_i:row_i + 1]
            lf_c = g_cols[:, row_i + 1:row_i + 2]
            cum_c = jnp.sum(jnp.where(allowed, lf_r, 0.0), axis=1, keepdims=True)
            cum_r = jnp.sum(jnp.where(allowed_t, lf_c, 0.0), axis=0, keepdims=True)
            total = jnp.sum(lf_r, axis=1, keepdims=True)
            dmat = jnp.where(allowed, cum_c - cum_r + li_r, -jnp.inf)
            m_loc = jnp.max(dmat, axis=1, keepdims=True)
            s_loc = s_ref[c] * jnp.exp(dmat - m_loc)
            intra = _dot(s_loc.astype(BF16), vb_ref[c])
            intra_sum = jnp.sum(s_loc, axis=1, keepdims=True)
            inter = cum_c + m
            m_row = jnp.maximum(inter, m_loc)
            shrink = jnp.exp(-jnp.abs(inter - m_loc))
            f_loc = jnp.where(m_loc >= inter, 1.0, shrink)
            w_inter = jnp.where(inter >= m_loc, 1.0, shrink)
            q32 = load(c, qc_ref, ql_ref) * scale
            num = intra * f_loc + w_inter * _dot(qb_ref[c], ct.astype(BF16))
            den = intra_sum * f_loc + w_inter * jnp.sum(q32 * n_vec, axis=1, keepdims=True)
            h_out = num / jnp.maximum(jnp.abs(den), jnp.exp(-m_row))
            rows = slice(c * length, (c + 1) * length)
            if reverse:
                hsum_ref[rows, :] += h_out
            else:
                hsum_ref[rows, :] = h_out
            if c != order[-1]:
                g_r = total - cum_r + li_r
                g_c = total - cum_c + li_c
                m_new = jnp.maximum(total + m, jnp.max(g_r, axis=1, keepdims=True))
                wk_c = jnp.exp(g_c - m_new)
                decay = jnp.exp(total + m - m_new)
                wv = (wk_c * load(c, vc_ref, vl_ref)).astype(BF16)
                ct = decay * ct + _dot(kt_ref[c], wv)
                n_vec = decay * n_vec + jnp.sum(wk_c * load(c, kc_ref, kl_ref), axis=0, keepdims=True)
                m = m_new

    gain = gain_ref[...]

    def finish(rows, o_pre):
        return (_rms(hsum_ref[rows, :]) * gain * jax.nn.sigmoid(o_pre)).astype(BF16)

    yc_ref[...] = finish(slice(0, CTX_LEN), oc_ref[...])
    yl_ref[...] = finish(slice(CTX_LEN, CTX_LEN + SEQ), ol_ref[...])


def _mlstm(p_ml, g_row_lat, g_col_lat, g_row_ctx, g_col_ctx, gain):
    assert CTX_LEN == ML_CHUNK
    hd = ML_HEAD_DIM
    length = ML_CHUNK
    n_lat = SEQ // length
    ctx0 = M_LAT // CTX_LEN

    def lat_spec(col):
        return pl.BlockSpec((SEQ, hd), lambda b, h: (b, col * ML_HEADS + h))

    def ctx_spec(col):
        return pl.BlockSpec((CTX_LEN, hd), lambda b, h: (ctx0 + b, col * ML_HEADS + h))

    return pl.pallas_call(
        _mlstm_kernel,
        grid=(BATCH, ML_HEADS),
        in_specs=[lat_spec(0), lat_spec(1), lat_spec(2), lat_spec(3),
                  ctx_spec(0), ctx_spec(1), ctx_spec(2), ctx_spec(3),
                  pl.BlockSpec((None, None, n_lat, 4, length), lambda b, h: (b, h, 0, 0, 0)),
                  pl.BlockSpec((None, None, n_lat, length, 4), lambda b, h: (b, h, 0, 0, 0)),
                  pl.BlockSpec((None, None, 4, length), lambda b, h: (b, h, 0, 0)),
                  pl.BlockSpec((None, None, length, 4), lambda b, h: (b, h, 0, 0)),
                  pl.BlockSpec((1, hd), lambda b, h: (0, h))],
        out_specs=[pl.BlockSpec((SEQ, hd), lambda b, h: (b, h)),
                   pl.BlockSpec((CTX_LEN, hd), lambda b, h: (b, h))],
        out_shape=[jax.ShapeDtypeStruct((M_LAT, ML_WIDTH), BF16),
                   jax.ShapeDtypeStruct((M_CTX, ML_WIDTH), BF16)],
        scratch_shapes=[pltpu.VMEM((1 + n_lat, length, hd), BF16),
                        pltpu.VMEM((1 + n_lat, hd, length), BF16),
                        pltpu.VMEM((1 + n_lat, length, hd), BF16),
                        pltpu.VMEM((1 + n_lat, length, length), F32),
                        pltpu.VMEM((CTX_LEN + SEQ, hd), F32)],
        compiler_params=_params(("parallel", "parallel")),
        name="mlstm",
    )(p_ml, p_ml, p_ml, p_ml, p_ml, p_ml, p_ml, p_ml, g_row_lat, g_col_lat, g_row_ctx, g_col_ctx, gain)


def _rope_partner(x, half):
    lane = lax.broadcasted_iota(I32, x.shape, 1)
    first = (lane & (2 * half - 1)) < half
    return jnp.where(first, pltpu.roll(x, LANES - half, 1), pltpu.roll(x, half, 1))


GA_PIECE = 512


def _ga_prep_kernel(h_ref, wa_ref, wb_ref, wc_ref, gq_ref, gk_ref, cos_ref, sin_ref, q_ref, k_ref, v_ref):
    h = h_ref[...]
    cos = cos_ref[...]
    sin = sin_ref[...]
    dh = GA_HEAD_DIM
    per_piece = GA_PIECE // dh

    def norm_rope(x, gain):
        y = _rms(x) * gain
        return y * cos + pltpu.roll(y, dh // 2, 1) * sin

    for piece, w_ref in enumerate((wa_ref, wb_ref)):
        p = _dot_nt(h, w_ref[...])
        for j in range(per_piece):
            q = norm_rope(p[:, j * dh:(j + 1) * dh], gq_ref[...])
            q_ref[piece * per_piece + j] = (q * dh ** -0.5).astype(BF16)
    p = _dot_nt(h, wc_ref[...])
    for g in range(GA_KV_HEADS):
        k_ref[g] = norm_rope(p[:, g * dh:(g + 1) * dh], gk_ref[...]).astype(BF16)
        v_ref[g] = p[:, GA_KV_WIDTH + g * dh:GA_KV_WIDTH + (g + 1) * dh].astype(BF16)


def _ga_prep(h, wt_ga, gain_q, gain_k, cos, sin, tm=512):
    assert GA_WIDTH == 2 * GA_PIECE and 2 * GA_KV_WIDTH == GA_PIECE
    d = h.shape[1]
    dh = GA_HEAD_DIM

    def head_spec(n_heads):
        return pl.BlockSpec((n_heads, tm, dh), lambda i: (0, i, 0))

    def piece_spec(j):
        return pl.BlockSpec((GA_PIECE, d), lambda i: (j, 0))

    return pl.pallas_call(
        _ga_prep_kernel,
        grid=(M_ALL // tm,),
        in_specs=[pl.BlockSpec((tm, d), lambda i: (i, 0)), piece_spec(0), piece_spec(1), piece_spec(2),
                  pl.BlockSpec((1, dh), lambda i: (0, 0)),
                  pl.BlockSpec((1, dh), lambda i: (0, 0)),
                  pl.BlockSpec((tm, dh), lambda i: (i, 0)),
                  pl.BlockSpec((tm, dh), lambda i: (i, 0))],
        out_specs=[head_spec(GA_HEADS), head_spec(GA_KV_HEADS), head_spec(GA_KV_HEADS)],
        out_shape=[jax.ShapeDtypeStruct((GA_HEADS, M_ALL, dh), BF16),
                   jax.ShapeDtypeStruct((GA_KV_HEADS, M_ALL, dh), BF16),
                   jax.ShapeDtypeStruct((GA_KV_HEADS, M_ALL, dh), BF16)],
        compiler_params=_params(("parallel",)),
        name="ga_prep",
    )(h, wt_ga, wt_ga, wt_ga, gain_q, gain_k, cos, sin)


def _la_prep_kernel(h_ref, wsm_ref, gq_ref, gkv_ref, wuq_ref, wukv_ref, cos_ref, sin_ref,
                    q_ref, k_ref, v_ref, tail_ref):
    p = _dot_nt(h_ref[...], wsm_ref[...])
    tail = p[:, SM_TAIL:SM_TAIL + LANES]
    tail_ref[...] = tail
    cq = (_rms(p[:, :LA_Q_RANK]) * gq_ref[...]).astype(BF16)
    ckv = (_rms(p[:, SM_CKV:SM_CKV + LA_KV_RANK]) * gkv_ref[...]).astype(BF16)
    qf = _dot(cq, wuq_ref[...])
    kvf = _dot(ckv, wukv_ref[...])
    cos = cos_ref[...]
    sin = sin_ref[...]
    scale = (LA_NOPE + LA_ROPE) ** -0.5
    low = lax.broadcasted_iota(I32, tail.shape, 1) < LA_ROPE

    def rope(x):
        return x * cos + _rope_partner(x, LA_ROPE // 4) * sin

    kr = rope(tail)
    kr2 = jnp.where(low, kr, pltpu.roll(kr, LA_ROPE, 1)).astype(BF16)
    heads_per_tile = LANES // LA_ROPE
    for j in range(LA_HEADS // heads_per_tile):
        c0 = LA_HEADS * LA_NOPE + j * LANES
        qr = rope(qf[:, c0:c0 + LANES]) * scale
        q_ref[2 * j, :, LA_NOPE:] = jnp.where(low, qr, 0.0).astype(BF16)
        q_ref[2 * j + 1, :, LA_NOPE:] = jnp.where(low, 0.0, qr).astype(BF16)
    for h in range(LA_HEADS):
        q_ref[h, :, :LA_NOPE] = (qf[:, h * LA_NOPE:(h + 1) * LA_NOPE] * scale).astype(BF16)
        kv0 = h * (LA_NOPE + LA_VDIM)
        k_ref[h, :, :LA_NOPE] = kvf[:, kv0:kv0 + LA_NOPE].astype(BF16)
        k_ref[h, :, LA_NOPE:] = kr2
        v_ref[h] = kvf[:, kv0 + LA_NOPE:kv0 + LA_NOPE + LA_VDIM].astype(BF16)


def _la_prep(h, wt_small, gain_q, gain_kv, w_uq, w_ukv, cos, sin, tm=512):
    assert LA_HEADS % (LANES // LA_ROPE) == 0 and LA_NOPE == LANES and LA_DK == LA_NOPE + LANES
    d = h.shape[1]

    def full(a):
        return pl.BlockSpec(a.shape, lambda i: (0,) * a.ndim)

    return pl.pallas_call(
        _la_prep_kernel,
        grid=(M_ALL // tm,),
        in_specs=[pl.BlockSpec((tm, d), lambda i: (i, 0)), full(wt_small), full(gain_q), full(gain_kv),
                  full(w_uq), full(w_ukv),
                  pl.BlockSpec((tm, LANES), lambda i: (i, 0)),
                  pl.BlockSpec((tm, LANES), lambda i: (i, 0))],
        out_specs=[pl.BlockSpec((LA_HEADS, tm, LA_DK), lambda i: (0, i, 0)),
                   pl.BlockSpec((LA_HEADS, tm, LA_DK), lambda i: (0, i, 0)),
                   pl.BlockSpec((LA_HEADS, tm, LA_VDIM), lambda i: (0, i, 0)),
                   pl.BlockSpec((tm, LANES), lambda i: (i, 0))],
        out_shape=[jax.ShapeDtypeStruct((LA_HEADS, M_ALL, LA_DK), BF16),
                   jax.ShapeDtypeStruct((LA_HEADS, M_ALL, LA_DK), BF16),
                   jax.ShapeDtypeStruct((LA_HEADS, M_ALL, LA_VDIM), BF16),
                   jax.ShapeDtypeStruct((M_ALL, LANES), F32)],
        compiler_params=_params(("parallel",)),
        name="la_prep",
    )(h, wt_small, gain_q, gain_kv, w_uq, w_ukv, cos, sin)


ATT_TQ = 256
ATT_ROWS = 1024


def _attn_kernel(q_ref, kl_ref, vl_ref, kc_ref, vc_ref, kca_ref, vca_ref, o_ref, *, n_lat_blocks):
    i = pl.program_id(1)

    @pl.when(i < n_lat_blocks)
    def _():
        n_sub = ATT_ROWS // ATT_TQ

        def scores(sub):
            q = q_ref[sub * ATT_TQ:(sub + 1) * ATT_TQ, :]
            return _dot_nt(q, kl_ref[...]), _dot_nt(q, kc_ref[...])

        def softmax(s_l, s_c):
            mx = jnp.maximum(jnp.max(s_l, axis=1, keepdims=True), jnp.max(s_c, axis=1, keepdims=True))
            p_l = jnp.exp(s_l - mx)
            p_c = jnp.exp(s_c - mx)
            den = jnp.sum(p_l, axis=1, keepdims=True) + jnp.sum(p_c, axis=1, keepdims=True)
            return p_l.astype(BF16), p_c.astype(BF16), den

        def values(sub, p_l, p_c, den):
            o = _dot(p_l, vl_ref[...]) + _dot(p_c, vc_ref[...])
            o_ref[sub * ATT_TQ:(sub + 1) * ATT_TQ, :] = (o / den).astype(o_ref.dtype)

        pending_s = scores(0)
        for sub in range(n_sub):
            nxt = scores(sub + 1) if sub + 1 < n_sub else None
            values(sub, *softmax(*pending_s))
            pending_s = nxt

    @pl.when(i >= n_lat_blocks)
    def _():
        for sub in range(ATT_ROWS // CTX_LEN):
            rows = slice(sub * CTX_LEN, (sub + 1) * CTX_LEN)
            s = _dot_nt(q_ref[rows, :], kca_ref[rows, :])
            mx = jnp.max(s, axis=1, keepdims=True)
            p = jnp.exp(s - mx)
            den = jnp.sum(p, axis=1, keepdims=True)
            o = _dot(p.astype(BF16), vca_ref[rows, :])
            o_ref[rows, :] = (o / den).astype(o_ref.dtype)


def _attention(q, k, v, kv_group, need_ctx, name):
    assert M_CTX == ATT_ROWS and ATT_ROWS % ATT_TQ == 0
    n_heads, _, dk = q.shape
    dv = v.shape[2]
    n_lat_blocks = M_LAT // ATT_ROWS
    per_sample = SEQ // ATT_ROWS
    ctx0 = M_LAT // CTX_LEN
    n_blocks = n_lat_blocks + (1 if need_ctx else 0)

    def sample(i):
        return jnp.minimum(i // per_sample, BATCH - 1)

    def lat_spec(dim):
        return pl.BlockSpec((None, SEQ, dim), lambda h, i: (h // kv_group, sample(i), 0))

    def ctx_spec(dim):
        return pl.BlockSpec((None, CTX_LEN, dim), lambda h, i: (h // kv_group, ctx0 + sample(i), 0))

    def ctx_all_spec(dim):
        return pl.BlockSpec((None, M_CTX, dim), lambda h, i: (h // kv_group, n_lat_blocks, 0))

    return pl.pallas_call(
        functools.partial(_attn_kernel, n_lat_blocks=n_lat_blocks),
        grid=(n_heads, n_blocks),
        in_specs=[pl.BlockSpec((None, ATT_ROWS, dk), lambda h, i: (h, i, 0)),
                  lat_spec(dk), lat_spec(dv), ctx_spec(dk), ctx_spec(dv), ctx_all_spec(dk), ctx_all_spec(dv)],
        out_specs=pl.BlockSpec((ATT_ROWS, dv), lambda h, i: (i, h)),
        out_shape=jax.ShapeDtypeStruct((n_blocks * ATT_ROWS, n_heads * dv), BF16),
        compiler_params=_params(("parallel", "arbitrary")),
        name=name,
    )(q, k, v, k, v, k, v)


def _router_kernel(h_ref, w_ref, o_ref):
    logits = _dot_nt(w_ref[...], h_ref[...])
    mx = jnp.max(logits, axis=0, keepdims=True)
    e = jnp.exp(logits - mx)
    o_ref[...] = e / jnp.sum(e, axis=0, keepdims=True)


def _router(h, w_router_t, m_rows, tm=1024):
    d = h.shape[1]
    return pl.pallas_call(
        _router_kernel,
        grid=(m_rows // tm,),
        in_specs=[pl.BlockSpec((tm, d), lambda i: (i, 0)),
                  pl.BlockSpec((N_EXPERTS, d), lambda i: (0, 0))],
        out_specs=pl.BlockSpec((N_EXPERTS, tm), lambda i: (0, i)),
        out_shape=jax.ShapeDtypeStruct((N_EXPERTS, m_rows), F32),
        compiler_params=_params(("parallel",)),
        name="router",
    )(h, w_router_t)


PREFIX_BLOCK = 256


def _exclusive_prefix(x_bf, upper):
    n_tok = x_bf.shape[1]
    carry = jnp.zeros((x_bf.shape[0], 1), F32)
    parts = []
    for blk in range(n_tok // PREFIX_BLOCK):
        xb = x_bf[:, blk * PREFIX_BLOCK:(blk + 1) * PREFIX_BLOCK]
        parts.append(_dot(xb, upper) + carry)
        carry = carry + jnp.sum(xb.astype(F32), axis=1, keepdims=True)
    return parts[0] if len(parts) == 1 else jnp.concatenate(parts, axis=1)


def _bitonic_sort_descending(tiles):
    n_tiles = len(tiles)
    n = n_tiles * LANES
    lane = lax.broadcasted_iota(I32, tiles[0].shape, 1)
    k = 2
    while k <= n:
        j = k // 2
        while j >= 1:
            new_tiles = []
            for t in range(n_tiles):
                x = tiles[t]
                if j < LANES:
                    low = (lane & j) == 0
                    partner = jnp.where(low, pltpu.roll(x, LANES - j, 1), pltpu.roll(x, j, 1))
                else:
                    partner = tiles[t ^ (j // LANES)]
                big = jnp.maximum(x, partner)
                small = jnp.minimum(x, partner)
                up = ((t * LANES) & k) == 0
                if j >= LANES:
                    take_big = (((t * LANES) & j) == 0) == up
                    new_tiles.append(big if take_big else small)
                elif k >= LANES:
                    new_tiles.append(jnp.where(low, big, small) if up else jnp.where(low, small, big))
                else:
                    agree = (lane & j) * (k // j) == (lane & k)
                    new_tiles.append(jnp.where(agree, big, small))
            tiles = new_tiles
            j //= 2
        k *= 2
    return tiles


def _topk_kernel(aff_ref, rank_ref, *, cap):
    a = aff_ref[...]
    n_tok = a.shape[1]
    ordered = _bitonic_sort_descending([a[:, t * LANES:(t + 1) * LANES] for t in range(n_tok // LANES)])
    pos = cap - 1
    thr = ordered[pos // LANES][:, pos % LANES:pos % LANES + 1]
    gt = a > thr
    eq = a == thr
    need = cap - jnp.sum(jnp.where(gt, 1.0, 0.0), axis=1, keepdims=True)
    ri = lax.broadcasted_iota(I32, (PREFIX_BLOCK, PREFIX_BLOCK), 0)
    ci = lax.broadcasted_iota(I32, (PREFIX_BLOCK, PREFIX_BLOCK), 1)
    upper = jnp.where(ri < ci, 1.0, 0.0).astype(BF16)
    eq_before = _exclusive_prefix(jnp.where(eq, 1.0, 0.0).astype(BF16), upper)
    sel = gt | (eq & (eq_before < need))
    slot = _exclusive_prefix(jnp.where(sel, 1.0, 0.0).astype(BF16), upper)
    rank_ref[...] = jnp.where(sel, slot.astype(I32), -1)


def _topk(aff, n_tok, cap, blk0, n_sets):
    return pl.pallas_call(
        functools.partial(_topk_kernel, cap=cap),
        grid=(n_sets,),
        in_specs=[pl.BlockSpec((N_EXPERTS, n_tok), lambda s: (0, blk0 + s))],
        out_specs=pl.BlockSpec((N_EXPERTS, n_tok), lambda s: (0, s)),
        out_shape=jax.ShapeDtypeStruct((N_EXPERTS, n_sets * n_tok), I32),
        compiler_params=_params(("parallel",)),
        name=f"topk_{n_tok}",
    )(aff)


def _gather_kernel(rank_ref, aff_ref, h_ref, *rest, cap):
    xg_ref, val_ref = rest[-2:]
    for e in range(rank_ref.shape[0]):
        r = rank_ref[e]
        n_tok = r.shape[1]
        slot = lax.broadcasted_iota(I32, (cap, n_tok), 0)
        hit = r == slot
        onehot = jnp.where(hit, 1.0, 0.0).astype(BF16)
        xg_ref[e] = _dot(onehot, h_ref[...]).astype(xg_ref.dtype)
        val_ref[e] = jnp.sum(jnp.where(hit, aff_ref[e], 0.0), axis=1, keepdims=True)


def _gather(rank3, aff3, h, n_tok, cap, blk0, row0, rows_total, prev, experts_per_step):
    d = h.shape[1]
    rb0 = row0 // cap
    eg = experts_per_step
    in_specs = [pl.BlockSpec((eg, 1, n_tok), lambda b, e: (e, 0, b)),
                pl.BlockSpec((eg, 1, n_tok), lambda b, e: (e, 0, blk0 + b)),
                pl.BlockSpec((n_tok, d), lambda b, e: (blk0 + b, 0))]
    args = [rank3, aff3, h]
    aliases = {}
    if prev is not None:
        in_specs += [pl.BlockSpec(memory_space=pl.ANY), pl.BlockSpec(memory_space=pl.ANY)]
        args += list(prev)
        aliases = {3: 0, 4: 1}
    return pl.pallas_call(
        functools.partial(_gather_kernel, cap=cap),
        grid=(BATCH, N_EXPERTS // eg),
        in_specs=in_specs,
        out_specs=[pl.BlockSpec((eg, cap, d), lambda b, e: (e, rb0 + b, 0)),
                   pl.BlockSpec((eg, cap, 1), lambda b, e: (e, rb0 + b, 0))],
        out_shape=[jax.ShapeDtypeStruct((N_EXPERTS, rows_total, d), BF16),
                   jax.ShapeDtypeStruct((N_EXPERTS, rows_total, 1), F32)],
        input_output_aliases=aliases,
        compiler_params=_params(("parallel", "arbitrary")),
        name=f"moe_gather_{n_tok}",
    )(*args)


EXPERT_COLS = 512


def _expert_kernel(x_ref, w1_ref, w3_ref, w2_ref, val_ref, y_ref, acc_ref):
    f = pl.program_id(1)
    last = pl.num_programs(1) - 1
    col_slices = [slice(c0, c0 + EXPERT_COLS) for c0 in range(0, acc_ref.shape[1], EXPERT_COLS)]

    def hidden():
        x = x_ref[...]
        a = _dot(x, w1_ref[...].astype(BF16))
        u = _dot(x, w3_ref[...].astype(BF16))
        return (a * jax.nn.sigmoid(a) * u).astype(BF16)

    def down(hm, cols):
        return _dot(hm, w2_ref[:, cols].astype(BF16))

    @pl.when(f == 0)
    def _():
        hm = hidden()
        for cols in col_slices:
            acc_ref[:, cols] = down(hm, cols)

    @pl.when((f > 0) & (f < last))
    def _():
        hm = hidden()
        for cols in col_slices:
            acc_ref[:, cols] += down(hm, cols)

    @pl.when(f == last)
    def _():
        hm = hidden()
        val = val_ref[...]
        for cols in col_slices:
            y_ref[:, cols] = ((acc_ref[:, cols] + down(hm, cols)) * val).astype(y_ref.dtype)


def _experts(xg, vals, w_e1, w_e3, w_e2, layer, tf=256):
    n_exp, rows, d = xg.shape
    ff = w_e1.shape[3]
    assert ff // tf >= 2
    return pl.pallas_call(
        _expert_kernel,
        grid=(n_exp, ff // tf),
        in_specs=[pl.BlockSpec((None, rows, d), lambda e, f: (e, 0, 0)),
                  pl.BlockSpec((None, None, d, tf), lambda e, f: (layer, e, 0, f)),
                  pl.BlockSpec((None, None, d, tf), lambda e, f: (layer, e, 0, f)),
                  pl.BlockSpec((None, None, tf, d), lambda e, f: (layer, e, f, 0)),
                  pl.BlockSpec((None, rows, 1), lambda e, f: (e, 0, 0))],
        out_specs=pl.BlockSpec((None, rows, d), lambda e, f: (e, 0, 0)),
        out_shape=jax.ShapeDtypeStruct((n_exp, rows, d), BF16),
        scratch_shapes=[pltpu.VMEM((rows, d), F32)],
        compiler_params=_params(("parallel", "arbitrary")),
        name="moe_experts",
    )(xg, w_e1, w_e3, w_e2, vals)


COMBINE_LANES = 1024


def _combine_kernel(rank_ref, y_ref, x_ref, g_ref, *rest, cap):
    o_ref, lhs_ref = rest[-2:]
    n_slots = N_EXPERTS * cap
    chunk = min(COMBINE_LANES, n_slots)
    shift = cap.bit_length() - 1

    @pl.when(pl.program_id(2) == 0)
    def _():
        r = rank_ref[...].astype(F32).astype(BF16)
        tm = r.shape[0]
        for c0 in range(0, n_slots, chunk):
            lane = c0 + lax.broadcasted_iota(I32, (LANES, chunk), 1)
            row = lax.broadcasted_iota(I32, (LANES, chunk), 0)
            expand = jnp.where(lax.shift_right_logical(lane, shift) == row, 1.0, 0.0).astype(BF16)
            r_exp = _dot(r, expand)
            slot = (c0 + lax.broadcasted_iota(I32, (tm, chunk), 1)) & (cap - 1)
            lhs_ref[:, c0:c0 + chunk] = jnp.where(r_exp == slot.astype(F32), 1.0, 0.0).astype(BF16)

    y = y_ref[...]
    o_ref[...] = x_ref[...] + g_ref[...] * _dot(lhs_ref[...], y.reshape(n_slots, y.shape[2]))


def _combine(rank_t, y, x, gate_tab, out_prev, n_tok, cap, blk0, row0, tm, tn=1024):
    assert cap & (cap - 1) == 0
    d = x.shape[1]
    tpb = n_tok // tm
    xrow0 = blk0 * tpb
    rb0 = row0 // cap

    def grp(b):
        return b if n_tok == SEQ else BATCH

    in_specs = [pl.BlockSpec((tm, LANES), lambda b, i, j: (b * tpb + i, 0)),
                pl.BlockSpec((N_EXPERTS, cap, tn), lambda b, i, j: (0, rb0 + b, j)),
                pl.BlockSpec((tm, tn), lambda b, i, j: (xrow0 + b * tpb + i, j)),
                pl.BlockSpec((None, 1, tn), lambda b, i, j: (grp(b), 0, j))]
    args = [rank_t, y, x, gate_tab]
    aliases = {}
    if out_prev is not None:
        in_specs.append(pl.BlockSpec(memory_space=pl.ANY))
        args.append(out_prev)
        aliases = {4: 0}
    return pl.pallas_call(
        functools.partial(_combine_kernel, cap=cap),
        grid=(BATCH, tpb, d // tn),
        in_specs=in_specs,
        out_specs=pl.BlockSpec((tm, tn), lambda b, i, j: (xrow0 + b * tpb + i, j)),
        out_shape=jax.ShapeDtypeStruct(x.shape, F32),
        scratch_shapes=[pltpu.VMEM((tm, N_EXPERTS * cap), BF16)],
        input_output_aliases=aliases,
        compiler_params=_params(("parallel", "parallel", "arbitrary")),
        name=f"moe_combine_{n_tok}",
    )(*args)


def _expert_choice(h2, x, gate_tab, w_router_t, w_e1, w_e3, w_e2, layer, need_ctx):
    m_rows = h2.shape[0]
    aff = _router(h2, w_router_t, m_rows)
    aff3 = aff.reshape(N_EXPERTS, 1, m_rows)
    sets = [(SEQ, 0)]
    if need_ctx:
        sets.append((CTX_LEN, M_LAT // CTX_LEN))
    caps = [CAPACITY_FACTOR * n_tok // N_EXPERTS for n_tok, _ in sets]
    rows_total = sum(BATCH * cap for cap in caps)
    ranks, gathered, row0 = [], None, 0
    for (n_tok, blk0), cap in zip(sets, caps):
        rank = _topk(aff, n_tok, cap, blk0, BATCH)
        gathered = _gather(rank.reshape(N_EXPERTS, 1, BATCH * n_tok), aff3, h2, n_tok, cap, blk0, row0, rows_total,
                           gathered, experts_per_step=1 if n_tok == SEQ else N_EXPERTS)
        ranks.append((rank, row0))
        row0 += BATCH * cap
    y = _experts(gathered[0], gathered[1], w_e1, w_e3, w_e2, layer)
    out = None
    for (n_tok, blk0), cap, (rank, row0) in zip(sets, caps, ranks):
        rank_t = jnp.pad(rank.T, ((0, 0), (0, LANES - N_EXPERTS)))
        out = _combine(rank_t, y, x, gate_tab, out, n_tok, cap, blk0, row0, tm=min(512, n_tok))
    return out


def _rope_tables(dim):
    rows = SEQ // GRID_W
    quarter = dim // 4
    inv = ROPE_BASE ** (-jnp.arange(quarter, dtype=F32) / quarter)
    row = jnp.broadcast_to(jnp.arange(rows, dtype=F32)[:, None], (rows, GRID_W)).reshape(-1)
    col = jnp.broadcast_to(jnp.arange(GRID_W, dtype=F32)[None, :], (rows, GRID_W)).reshape(-1)
    ang_r = row[:, None] * inv[None, :]
    ang_c = col[:, None] * inv[None, :]
    cos = jnp.concatenate([jnp.cos(ang_r), jnp.cos(ang_r), jnp.cos(ang_c), jnp.cos(ang_c)], axis=-1)
    sin = jnp.concatenate([-jnp.sin(ang_r), jnp.sin(ang_r), -jnp.sin(ang_c), jnp.sin(ang_c)], axis=-1)
    cos = jnp.tile(cos, (BATCH, LANES // dim))
    sin = jnp.tile(sin, (BATCH, LANES // dim))
    cos = jnp.concatenate([cos, jnp.ones((M_CTX, LANES), F32)], axis=0)
    sin = jnp.concatenate([sin, jnp.zeros((M_CTX, LANES), F32)], axis=0)
    return cos, sin


def _halves_major(a, axis=-1):
    axis = axis % a.ndim
    quarter = a.shape[axis] // 4
    a4 = a.reshape(*a.shape[:axis], 2, 2, quarter, *a.shape[axis + 1:])
    return jnp.swapaxes(a4, axis, axis + 1).reshape(a.shape)


def _layer(x_lat, x_ctx, mods, layer, wt_in, ml_gate_bias, ml_norm, la_q_norm, la_kv_norm, la_w_uq, la_w_ukv,
           ga_q_norm, ga_k_norm, w_br_ml, w_br_la, w_br_ga, w_out, w_router, w_e1, w_e3, w_e2, ropes, need_ctx):
    d = D_MODEL
    m_rows = M_ALL if need_ctx else M_LAT
    sh1, sc1, g1, sh2, sc2, g2 = [mods[:N_GROUPS, k * d:(k + 1) * d].reshape(N_GROUPS, 1, d) for k in range(6)]
    (cos_la, sin_la), (cos_ga, sin_ga) = ropes

    n_qk = GA_HEADS + GA_KV_HEADS
    wt_ml = wt_in[:OFF_MLG].astype(BF16)
    wt_small = jnp.concatenate([wt_in[OFF_CQ:OFF_GAQ], wt_in[OFF_MLG:OFF_CQ],
                                jnp.zeros((SMALL_WIDTH - (OFF_GAQ - OFF_MLG), d), F32)], axis=0).astype(BF16)
    wt_gaqk = _halves_major(wt_in[OFF_GAQ:OFF_GAV].reshape(n_qk, GA_HEAD_DIM, d), axis=1).reshape(-1, d)
    wt_ga = jnp.concatenate([wt_gaqk, wt_in[OFF_GAV:OFF_GATE]], axis=0).astype(BF16)
    wt_gate = wt_in[OFF_GATE:].astype(BF16)
    w_uq = la_w_uq.reshape(LA_Q_RANK, LA_HEADS, LA_NOPE + LA_ROPE)
    w_uq = jnp.concatenate([w_uq[:, :, :LA_NOPE].reshape(LA_Q_RANK, LA_HEADS * LA_NOPE),
                            w_uq[:, :, LA_NOPE:].reshape(LA_Q_RANK, LA_HEADS * LA_ROPE)], axis=1).astype(BF16)

    h = _rms_mod(x_lat, x_ctx, 1.0 + sc1, sh1, M_ALL)
    p_ml = _mm(h, wt_ml, name="proj_ml")
    q_la, k_la, v_la, tail = _la_prep(h, wt_small, la_q_norm.reshape(1, LA_Q_RANK),
                                      la_kv_norm.reshape(1, LA_KV_RANK), w_uq, la_w_ukv.astype(BF16), cos_la, sin_la)
    q_ga, k_ga, v_ga = _ga_prep(h, wt_ga, _halves_major(ga_q_norm.reshape(1, GA_HEAD_DIM)),
                                _halves_major(ga_k_norm.reshape(1, GA_HEAD_DIM)), cos_ga, sin_ga)

    gates = (tail[:, TAIL_MLG:TAIL_MLG + ML_GATES] + ml_gate_bias[None, :]).reshape(M_ALL, 2, 2, ML_HEADS)
    gates = jnp.stack([gates[:, 0, 0], jax.nn.log_sigmoid(gates[:, 0, 1]),
                       gates[:, 1, 0], jax.nn.log_sigmoid(gates[:, 1, 1])], axis=-1)
    n_lat = SEQ // ML_CHUNK
    g_lat = gates[:M_LAT].reshape(BATCH, n_lat, ML_CHUNK, ML_HEADS, 4)
    g_ctx = gates[M_LAT:].reshape(BATCH, CTX_LEN, ML_HEADS, 4)
    y_ml_lat, y_ml_ctx = _mlstm(p_ml, jnp.transpose(g_lat, (0, 3, 1, 4, 2)), jnp.transpose(g_lat, (0, 3, 1, 2, 4)),
                                jnp.transpose(g_ctx, (0, 2, 3, 1)), jnp.transpose(g_ctx, (0, 2, 1, 3)),
                                ml_norm.reshape(1, ML_WIDTH))

    y_la = _attention(q_la, k_la, v_la, 1, need_ctx, "attn_mla")
    y_ga = _attention(q_ga, k_ga, v_ga, GA_HEADS // GA_KV_HEADS, need_ctx, "attn_gqa")

    merged = _merge(h, y_ml_lat, y_ml_ctx, y_la, y_ga, wt_gate, w_br_ml.astype(BF16), w_br_la.astype(BF16),
                    w_br_ga.astype(BF16), m_rows)
    x, h2 = _out_proj(merged, w_out.astype(BF16), x_lat, x_ctx if need_ctx else None, g1, 1.0 + sc2, sh2, m_rows)
    return _expert_choice(h2, x, g2, w_router.T.astype(BF16), w_e1, w_e3, w_e2, layer, need_ctx)


def kernel(x, c, ctx, c_ctx, w_mod, b_mod, w_in, ml_gate_bias, ml_norm, la_q_norm, la_kv_norm, la_w_uq, la_w_ukv,
           ga_q_norm, ga_k_norm, w_br_ml, w_br_la, w_br_ga, w_out, w_router, w_e1, w_e3, w_e2, final_norm):
    d = D_MODEL
    cc = jnp.concatenate([c, c_ctx[None, :], jnp.zeros((MODS_ROWS - N_GROUPS, d), F32)], axis=0)
    mods = _mods(cc, w_mod, b_mod.reshape(DEPTH, 1, 6 * d))
    ropes = (_rope_tables(LA_ROPE), tuple(_halves_major(t) for t in _rope_tables(GA_HEAD_DIM)))
    xs, xs_ctx = x.reshape(M_LAT, d), ctx.reshape(M_CTX, d)
    wt_in = jnp.swapaxes(w_in, 1, 2)
    for l in range(DEPTH):
        need_ctx = l < DEPTH - 1
        xs = _layer(xs, xs_ctx, mods[l], l, wt_in[l], ml_gate_bias[l], ml_norm[l], la_q_norm[l], la_kv_norm[l],
                    la_w_uq[l], la_w_ukv[l], ga_q_norm[l], ga_k_norm[l], w_br_ml[l], w_br_la[l], w_br_ga[l],
                    w_out[l], w_router[l], w_e1, w_e3, w_e2, ropes, need_ctx)
        xs_ctx = None
    out = _rms_gain(xs, final_norm.reshape(1, d), F32, name="final_norm")
    return out.reshape(BATCH, SEQ, d)
```

```python
import functools

import jax
import jax.numpy as jnp
from jax import lax
from jax.experimental import pallas as pl
from jax.experimental.pallas import tpu as pltpu

F32 = jnp.float32
BF16 = jnp.bfloat16
I32 = jnp.int32

D_MODEL = 2048
BATCH = 4
SEQ = 2048
DEPTH = 2
GRID_W = 64
CTX_LEN = 256
EPS = 1e-6
ROPE_BASE = 10000.0

ML_HEADS = 4
ML_HEAD_DIM = 256
ML_WIDTH = ML_HEADS * ML_HEAD_DIM
ML_GATES = 2 * 2 * ML_HEADS
ML_CHUNK = 256

LA_HEADS = 8
LA_NOPE = 128
LA_ROPE = 64
LA_VDIM = 128
LA_Q_RANK = 512
LA_KV_RANK = 256
LA_DK = 256

GA_HEADS = 8
GA_KV_HEADS = 2
GA_HEAD_DIM = 128
GA_WIDTH = GA_HEADS * GA_HEAD_DIM
GA_KV_WIDTH = GA_KV_HEADS * GA_HEAD_DIM

N_EXPERTS = 16
EXPERT_FF = 1024
CAPACITY_FACTOR = 2

M_LAT = BATCH * SEQ
M_CTX = BATCH * CTX_LEN
M_ALL = M_LAT + M_CTX
MODS_ROWS = 16
N_GROUPS = BATCH + 1

LANES = 128
SUBLANES = 8

OFF_MLG = 4 * ML_WIDTH
OFF_CQ = OFF_MLG + ML_GATES
OFF_CKV = OFF_CQ + LA_Q_RANK
OFF_KR = OFF_CKV + LA_KV_RANK
OFF_GAQ = OFF_KR + LA_ROPE
OFF_GAK = OFF_GAQ + GA_WIDTH
OFF_GAV = OFF_GAK + GA_KV_WIDTH
OFF_GATE = OFF_GAV + GA_KV_WIDTH
SM_CKV = LA_Q_RANK
SM_TAIL = SM_CKV + LA_KV_RANK
SMALL_WIDTH = SM_TAIL + LANES
TAIL_MLG = LA_ROPE

VMEM_LIMIT = 56 * 1024 * 1024


def _params(semantics, vmem=VMEM_LIMIT):
    return pltpu.CompilerParams(dimension_semantics=semantics, vmem_limit_bytes=vmem)


def _dot(a, b):
    return jnp.dot(a, b, preferred_element_type=F32)


def _dot_nt(a, b):
    return lax.dot_general(a, b, (((1,), (1,)), ((), ())), preferred_element_type=F32)


def _rms(x):
    return x * lax.rsqrt(jnp.mean(x * x, axis=-1, keepdims=True) + EPS)


def _group_of_block(tm):
    return lambda i: (i * tm) // SEQ


def _mods_kernel(c_ref, w_ref, b_ref, o_ref):
    c = c_ref[...]
    a = (c * jax.nn.sigmoid(c)).astype(BF16)
    part = _dot(a, w_ref[...].astype(BF16))

    @pl.when(pl.program_id(1) == 0)
    def _():
        o_ref[...] = part + b_ref[...]

    @pl.when(pl.program_id(1) > 0)
    def _():
        o_ref[...] += part


def _mods(cc, w_mod, b_mod, tk=256):
    depth, d, n = w_mod.shape
    rows = cc.shape[0]
    return pl.pallas_call(
        _mods_kernel,
        grid=(depth, d // tk),
        in_specs=[pl.BlockSpec((rows, tk), lambda l, k: (0, k)),
                  pl.BlockSpec((None, tk, n), lambda l, k: (l, k, 0)),
                  pl.BlockSpec((None, 1, n), lambda l, k: (l, 0, 0))],
        out_specs=pl.BlockSpec((None, rows, n), lambda l, k: (l, 0, 0)),
        out_shape=jax.ShapeDtypeStruct((depth, rows, n), F32),
        compiler_params=_params(("parallel", "arbitrary")),
        name="mods",
    )(cc, w_mod, b_mod)


def _row_sources(x_lat, x_ctx, tm, n_grid_axes=1):
    d = x_lat.shape[1]
    n_lat_tiles = M_LAT // tm

    def spec(fn):
        return pl.BlockSpec((tm, d), (lambda i: (fn(i), 0)) if n_grid_axes == 1 else (lambda j, i: (fn(i), 0)))

    if x_ctx is None:
        return [spec(lambda i: i)], [x_lat], lambda i, refs: refs[0][...]
    specs = [spec(lambda i: jnp.minimum(i, n_lat_tiles - 1)), spec(lambda i: jnp.maximum(i - n_lat_tiles, 0))]
    return specs, [x_lat, x_ctx], lambda i, refs: jnp.where(i < n_lat_tiles, refs[0][...], refs[1][...])


def _rms_kernel(*refs, pick):
    scale_ref, shift_ref, o_ref = refs[-3:]
    x = pick(pl.program_id(0), refs[:-3])
    o_ref[...] = (_rms(x) * scale_ref[...] + shift_ref[...]).astype(o_ref.dtype)


def _rms_gain_kernel(x_ref, scale_ref, o_ref):
    o_ref[...] = (_rms(x_ref[...]) * scale_ref[...]).astype(o_ref.dtype)


def _rms_mod(x_lat, x_ctx, scale_tab, shift_tab, m_rows, tm=512):
    d = x_lat.shape[1]
    grp = _group_of_block(tm)
    x_specs, x_args, pick = _row_sources(x_lat, x_ctx, tm)
    return pl.pallas_call(
        functools.partial(_rms_kernel, pick=pick),
        grid=(m_rows // tm,),
        in_specs=x_specs + [pl.BlockSpec((None, 1, d), lambda i: (grp(i), 0, 0)),
                            pl.BlockSpec((None, 1, d), lambda i: (grp(i), 0, 0))],
        out_specs=pl.BlockSpec((tm, d), lambda i: (i, 0)),
        out_shape=jax.ShapeDtypeStruct((m_rows, d), BF16),
        compiler_params=_params(("parallel",)),
        name="rms_mod",
    )(*x_args, scale_tab, shift_tab)


def _rms_gain(x, gain, out_dtype, tm=512, name="rms_gain"):
    m_rows, width = x.shape
    return pl.pallas_call(
        _rms_gain_kernel,
        grid=(m_rows // tm,),
        in_specs=[pl.BlockSpec((tm, width), lambda i: (i, 0)),
                  pl.BlockSpec((1, width), lambda i: (0, 0))],
        out_specs=pl.BlockSpec((tm, width), lambda i: (i, 0)),
        out_shape=jax.ShapeDtypeStruct((m_rows, width), out_dtype),
        compiler_params=_params(("parallel",)),
        name=name,
    )(x, gain)


def _mm_kernel(a_ref, wt_ref, o_ref, wbf_ref):
    @pl.when(pl.program_id(1) == 0)
    def _():
        wbf_ref[...] = wt_ref[...].astype(BF16)

    o_ref[...] = _dot_nt(a_ref[...], wbf_ref[...]).astype(o_ref.dtype)


def _mm(a, wt_all, row0, n_cols, *, tm=1024, tn=1024, out_dtype=F32, name="mm"):
    m_rows, k = a.shape
    assert m_rows % tm == 0 and n_cols % tn == 0
    return pl.pallas_call(
        _mm_kernel,
        grid=(n_cols // tn, m_rows // tm),
        in_specs=[pl.BlockSpec((tm, k), lambda j, i: (i, 0)),
                  pl.BlockSpec((pl.Element(tn), pl.Element(k)),
                               lambda j, i: (pl.multiple_of(row0 + j * tn, SUBLANES), 0),
                               pipeline_mode=pl.Buffered(1))],
        out_specs=pl.BlockSpec((tm, tn), lambda j, i: (i, j)),
        out_shape=jax.ShapeDtypeStruct((m_rows, n_cols), out_dtype),
        scratch_shapes=[pltpu.VMEM((tn, k), BF16)],
        compiler_params=_params(("parallel", "arbitrary")),
        name=name,
    )(a, wt_all)


def _out_proj_kernel(a_ref, w_ref, *refs, pick):
    g_ref, scale_ref, shift_ref, x_out_ref, h_out_ref = refs[-5:]
    x = pick(pl.program_id(0), refs[:-5])
    x_new = x + g_ref[...] * _dot(a_ref[...], w_ref[...])
    x_out_ref[...] = x_new
    h_out_ref[...] = (_rms(x_new) * scale_ref[...] + shift_ref[...]).astype(h_out_ref.dtype)


def _out_proj(a, w, x_lat, x_ctx, gate_tab, scale_tab, shift_tab, m_rows, tm=512):
    k = a.shape[1]
    d = w.shape[1]
    grp = _group_of_block(tm)
    x_specs, x_args, pick = _row_sources(x_lat, x_ctx, tm)
    tab_spec = pl.BlockSpec((None, 1, d), lambda i: (grp(i), 0, 0))
    return pl.pallas_call(
        functools.partial(_out_proj_kernel, pick=pick),
        grid=(m_rows // tm,),
        in_specs=[pl.BlockSpec((tm, k), lambda i: (i, 0)),
                  pl.BlockSpec((k, d), lambda i: (0, 0))] + x_specs + [tab_spec, tab_spec, tab_spec],
        out_specs=[pl.BlockSpec((tm, d), lambda i: (i, 0)), pl.BlockSpec((tm, d), lambda i: (i, 0))],
        out_shape=[jax.ShapeDtypeStruct((m_rows, d), F32), jax.ShapeDtypeStruct((m_rows, d), BF16)],
        compiler_params=_params(("parallel",)),
        name="out_proj",
    )(a, w, *x_args, gate_tab, scale_tab, shift_tab)


def _merge_kernel(h_ref, y0l_ref, y0c_ref, y1_ref, y2_ref, wg0_ref, wg1_ref, wg2_ref, wb0_ref, wb1_ref, wb2_ref,
                  o_ref, wg_bf_ref, wb_bf_ref, *, n_lat_tiles):
    @pl.when(pl.program_id(1) == 0)
    def _():
        for b, (wg_ref, wb_ref) in enumerate(((wg0_ref, wb0_ref), (wg1_ref, wb1_ref), (wg2_ref, wb2_ref))):
            wg_bf_ref[b] = wg_ref[...].astype(BF16)
            wb_bf_ref[b] = wb_ref[...].astype(BF16)

    h = h_ref[...]
    y0 = jnp.where(pl.program_id(1) < n_lat_tiles, y0l_ref[...], y0c_ref[...])
    acc = jax.nn.sigmoid(_dot_nt(h, wg_bf_ref[0])) * _dot(y0, wb_bf_ref[0])
    acc = acc + jax.nn.sigmoid(_dot_nt(h, wg_bf_ref[1])) * _dot(y1_ref[...], wb_bf_ref[1])
    acc = acc + jax.nn.sigmoid(_dot_nt(h, wg_bf_ref[2])) * _dot(y2_ref[...], wb_bf_ref[2])
    o_ref[...] = acc.astype(o_ref.dtype)


def _merge(h, y_ml_lat, y_ml_ctx, y_la, y_ga, wt_all, layer, gate_row0, w_br_ml, w_br_la, w_br_ga, m_rows,
           tm=512, tn=512):
    d = D_MODEL
    nb = d // tn
    n_lat_tiles = M_LAT // tm
    y_spec = pl.BlockSpec((tm, ML_WIDTH), lambda j, i: (i, 0))
    wb_spec = pl.BlockSpec((None, ML_WIDTH, tn), lambda j, i: (layer, 0, j), pipeline_mode=pl.Buffered(1))

    def wg_spec(branch):
        return pl.BlockSpec((pl.Element(tn), pl.Element(d)),
                            lambda j, i: (pl.multiple_of(gate_row0 + branch * d + j * tn, SUBLANES), 0),
                            pipeline_mode=pl.Buffered(1))

    return pl.pallas_call(
        functools.partial(_merge_kernel, n_lat_tiles=n_lat_tiles),
        grid=(nb, m_rows // tm),
        in_specs=[pl.BlockSpec((tm, d), lambda j, i: (i, 0)),
                  pl.BlockSpec((tm, ML_WIDTH), lambda j, i: (jnp.minimum(i, n_lat_tiles - 1), 0)),
                  pl.BlockSpec((tm, ML_WIDTH), lambda j, i: (jnp.maximum(i - n_lat_tiles, 0), 0)),
                  y_spec, y_spec, wg_spec(0), wg_spec(1), wg_spec(2), wb_spec, wb_spec, wb_spec],
        out_specs=pl.BlockSpec((tm, tn), lambda j, i: (i, j)),
        out_shape=jax.ShapeDtypeStruct((m_rows, d), BF16),
        scratch_shapes=[pltpu.VMEM((3, tn, d), BF16), pltpu.VMEM((3, ML_WIDTH, tn), BF16)],
        compiler_params=_params(("parallel", "arbitrary")),
        name="merge",
    )(h, y_ml_lat, y_ml_ctx, y_la, y_ga, wt_all, wt_all, wt_all, w_br_ml, w_br_la, w_br_ga)


def _mlstm_kernel(ql_ref, kl_ref, vl_ref, ol_ref, qc_ref, kc_ref, vc_ref, oc_ref,
                  grl_ref, gcl_ref, grc_ref, gcc_ref, gain_ref, yl_ref, yc_ref,
                  qb_ref, kt_ref, vb_ref, s_ref, hsum_ref):
    length = ML_CHUNK
    hd = ML_HEAD_DIM
    n_lat = SEQ // length
    n_chunks = n_lat + 1
    scale = hd ** -0.5

    def load(c, ctx_ref, lat_ref):
        return ctx_ref[...] if c == 0 else lat_ref[(c - 1) * length:c * length, :]

    for c in range(n_chunks):
        qb = (load(c, qc_ref, ql_ref) * scale).astype(BF16)
        kt = jnp.transpose(load(c, kc_ref, kl_ref)).astype(BF16)
        qb_ref[c] = qb
        kt_ref[c] = kt
        vb_ref[c] = load(c, vc_ref, vl_ref).astype(BF16)
        s_ref[c] = _dot(qb, kt)

    ri = lax.broadcasted_iota(I32, (length, length), 0)
    ci = lax.broadcasted_iota(I32, (length, length), 1)
    for direction in range(2):
        reverse = direction == 1
        row_i = 2 * direction
        allowed = (ci >= ri) if reverse else (ci <= ri)
        allowed_t = (ri >= ci) if reverse else (ri <= ci)
        order = [0] + ([n_lat - j for j in range(n_lat)] if reverse else list(range(1, n_chunks)))
        m = jnp.zeros((1, 1), F32)
        ct = jnp.zeros((hd, hd), F32)
        n_vec = jnp.zeros((1, hd), F32)
        for c in order:
            g_rows = grc_ref[...] if c == 0 else grl_ref[c - 1]
            g_cols = gcc_ref[...] if c == 0 else gcl_ref[c - 1]
            li_r = g_rows[row_i:row_i + 1, :]
            lf_r = g_rows[row_i + 1:row_i + 2, :]
            li_c = g_cols[:, row_i:row_i + 1]
            lf_c = g_cols[:, row_i + 1:row_i + 2]
            cum_c = jnp.sum(jnp.where(allowed, lf_r, 0.0), axis=1, keepdims=True)
            cum_r = jnp.sum(jnp.where(allowed_t, lf_c, 0.0), axis=0, keepdims=True)
            total = jnp.sum(lf_r, axis=1, keepdims=True)
            dmat = jnp.where(allowed, cum_c - cum_r + li_r, -jnp.inf)
            m_loc = jnp.max(dmat, axis=1, keepdims=True)
            s_loc = s_ref[c] * jnp.exp(dmat - m_loc)
            intra = _dot(s_loc.astype(BF16), vb_ref[c])
            intra_sum = jnp.sum(s_loc, axis=1, keepdims=True)
            inter = cum_c + m
            m_row = jnp.maximum(inter, m_loc)
            shrink = jnp.exp(-jnp.abs(inter - m_loc))
            f_loc = jnp.where(m_loc >= inter, 1.0, shrink)
            w_inter = jnp.where(inter >= m_loc, 1.0, shrink)
            q32 = load(c, qc_ref, ql_ref) * scale
            num = intra * f_loc + w_inter * _dot(qb_ref[c], ct.astype(BF16))
            den = intra_sum * f_loc + w_inter * jnp.sum(q32 * n_vec, axis=1, keepdims=True)
            h_out = num / jnp.maximum(jnp.abs(den), jnp.exp(-m_row))
            rows = slice(c * length, (c + 1) * length)
            if reverse:
                hsum_ref[rows, :] += h_out
            else:
                hsum_ref[rows, :] = h_out
            if c != order[-1]:
                g_r = total - cum_r + li_r
                g_c = total - cum_c + li_c
                m_new = jnp.maximum(total + m, jnp.max(g_r, axis=1, keepdims=True))
                wk_c = jnp.exp(g_c - m_new)
                decay = jnp.exp(total + m - m_new)
                wv = (wk_c * load(c, vc_ref, vl_ref)).astype(BF16)
                ct = decay * ct + _dot(kt_ref[c], wv)
                n_vec = decay * n_vec + jnp.sum(wk_c * load(c, kc_ref, kl_ref), axis=0, keepdims=True)
                m = m_new

    gain = gain_ref[...]

    def finish(rows, o_pre):
        return (_rms(hsum_ref[rows, :]) * gain * jax.nn.sigmoid(o_pre)).astype(BF16)

    yc_ref[...] = finish(slice(0, CTX_LEN), oc_ref[...])
    yl_ref[...] = finish(slice(CTX_LEN, CTX_LEN + SEQ), ol_ref[...])


def _mlstm(p_ml, g_row_lat, g_col_lat, g_row_ctx, g_col_ctx, gain):
    assert CTX_LEN == ML_CHUNK
    hd = ML_HEAD_DIM
    length = ML_CHUNK
    n_lat = SEQ // length
    ctx0 = M_LAT // CTX_LEN

    def lat_spec(col):
        return pl.BlockSpec((SEQ, hd), lambda b, h: (b, col * ML_HEADS + h))

    def ctx_spec(col):
        return pl.BlockSpec((CTX_LEN, hd), lambda b, h: (ctx0 + b, col * ML_HEADS + h))

    return pl.pallas_call(
        _mlstm_kernel,
        grid=(BATCH, ML_HEADS),
        in_specs=[lat_spec(0), lat_spec(1), lat_spec(2), lat_spec(3),
                  ctx_spec(0), ctx_spec(1), ctx_spec(2), ctx_spec(3),
                  pl.BlockSpec((None, None, n_lat, 4, length), lambda b, h: (b, h, 0, 0, 0)),
                  pl.BlockSpec((None, None, n_lat, length, 4), lambda b, h: (b, h, 0, 0, 0)),
                  pl.BlockSpec((None, None, 4, length), lambda b, h: (b, h, 0, 0)),
                  pl.BlockSpec((None, None, length, 4), lambda b, h: (b, h, 0, 0)),
                  pl.BlockSpec((1, hd), lambda b, h: (0, h))],
        out_specs=[pl.BlockSpec((SEQ, hd), lambda b, h: (b, h)),
                   pl.BlockSpec((CTX_LEN, hd), lambda b, h: (b, h))],
        out_shape=[jax.ShapeDtypeStruct((M_LAT, ML_WIDTH), BF16),
                   jax.ShapeDtypeStruct((M_CTX, ML_WIDTH), BF16)],
        scratch_shapes=[pltpu.VMEM((1 + n_lat, length, hd), BF16),
                        pltpu.VMEM((1 + n_lat, hd, length), BF16),
                        pltpu.VMEM((1 + n_lat, length, hd), BF16),
                        pltpu.VMEM((1 + n_lat, length, length), F32),
                        pltpu.VMEM((CTX_LEN + SEQ, hd), F32)],
        compiler_params=_params(("parallel", "parallel")),
        name="mlstm",
    )(p_ml, p_ml, p_ml, p_ml, p_ml, p_ml, p_ml, p_ml, g_row_lat, g_col_lat, g_row_ctx, g_col_ctx, gain)


def _rope_partner(x, half):
    lane = lax.broadcasted_iota(I32, x.shape, 1)
    first = (lane & (2 * half - 1)) < half
    return jnp.where(first, pltpu.roll(x, LANES - half, 1), pltpu.roll(x, half, 1))


GA_PIECE = 512


def _ga_prep_kernel(h_ref, wa_ref, wb_ref, wc_ref, gq_ref, gk_ref, cos_ref, sin_ref, q_ref, k_ref, v_ref):
    h = h_ref[...]
    cos = cos_ref[...]
    sin = sin_ref[...]
    dh = GA_HEAD_DIM
    per_piece = GA_PIECE // dh

    def norm_rope(x, gain):
        y = _rms(x) * gain
        return y * cos + pltpu.roll(y, dh // 2, 1) * sin

    for piece, w_ref in enumerate((wa_ref, wb_ref)):
        p = _dot_nt(h, w_ref[...])
        for j in range(per_piece):
            q = norm_rope(p[:, j * dh:(j + 1) * dh], gq_ref[...])
            q_ref[piece * per_piece + j] = (q * dh ** -0.5).astype(BF16)
    p = _dot_nt(h, wc_ref[...])
    for g in range(GA_KV_HEADS):
        k_ref[g] = norm_rope(p[:, g * dh:(g + 1) * dh], gk_ref[...]).astype(BF16)
        v_ref[g] = p[:, GA_KV_WIDTH + g * dh:GA_KV_WIDTH + (g + 1) * dh].astype(BF16)


def _ga_prep(h, wt_ga, gain_q, gain_k, cos, sin, tm=512):
    assert GA_WIDTH == 2 * GA_PIECE and 2 * GA_KV_WIDTH == GA_PIECE
    d = h.shape[1]
    dh = GA_HEAD_DIM

    def head_spec(n_heads):
        return pl.BlockSpec((n_heads, tm, dh), lambda i: (0, i, 0))

    def piece_spec(j):
        return pl.BlockSpec((GA_PIECE, d), lambda i: (j, 0))

    return pl.pallas_call(
        _ga_prep_kernel,
        grid=(M_ALL // tm,),
        in_specs=[pl.BlockSpec((tm, d), lambda i: (i, 0)), piece_spec(0), piece_spec(1), piece_spec(2),
                  pl.BlockSpec((1, dh), lambda i: (0, 0)),
                  pl.BlockSpec((1, dh), lambda i: (0, 0)),
                  pl.BlockSpec((tm, dh), lambda i: (i, 0)),
                  pl.BlockSpec((tm, dh), lambda i: (i, 0))],
        out_specs=[head_spec(GA_HEADS), head_spec(GA_KV_HEADS), head_spec(GA_KV_HEADS)],
        out_shape=[jax.ShapeDtypeStruct((GA_HEADS, M_ALL, dh), BF16),
                   jax.ShapeDtypeStruct((GA_KV_HEADS, M_ALL, dh), BF16),
                   jax.ShapeDtypeStruct((GA_KV_HEADS, M_ALL, dh), BF16)],
        compiler_params=_params(("parallel",)),
        name="ga_prep",
    )(h, wt_ga, wt_ga, wt_ga, gain_q, gain_k, cos, sin)


def _la_prep_kernel(h_ref, wsm_ref, gq_ref, gkv_ref, wuq_ref, wukv_ref, cos_ref, sin_ref,
                    q_ref, k_ref, v_ref, tail_ref):
    p = _dot_nt(h_ref[...], wsm_ref[...])
    tail = p[:, SM_TAIL:SM_TAIL + LANES]
    tail_ref[...] = tail
    cq = (_rms(p[:, :LA_Q_RANK]) * gq_ref[...]).astype(BF16)
    ckv = (_rms(p[:, SM_CKV:SM_CKV + LA_KV_RANK]) * gkv_ref[...]).astype(BF16)
    qf = _dot(cq, wuq_ref[...])
    kvf = _dot(ckv, wukv_ref[...])
    cos = cos_ref[...]
    sin = sin_ref[...]
    scale = (LA_NOPE + LA_ROPE) ** -0.5
    low = lax.broadcasted_iota(I32, tail.shape, 1) < LA_ROPE

    def rope(x):
        return x * cos + _rope_partner(x, LA_ROPE // 4) * sin

    kr = rope(tail)
    kr2 = jnp.where(low, kr, pltpu.roll(kr, LA_ROPE, 1)).astype(BF16)
    heads_per_tile = LANES // LA_ROPE
    for j in range(LA_HEADS // heads_per_tile):
        c0 = LA_HEADS * LA_NOPE + j * LANES
        qr = rope(qf[:, c0:c0 + LANES]) * scale
        q_ref[2 * j, :, LA_NOPE:] = jnp.where(low, qr, 0.0).astype(BF16)
        q_ref[2 * j + 1, :, LA_NOPE:] = jnp.where(low, 0.0, qr).astype(BF16)
    for h in range(LA_HEADS):
        q_ref[h, :, :LA_NOPE] = (qf[:, h * LA_NOPE:(h + 1) * LA_NOPE] * scale).astype(BF16)
        kv0 = h * (LA_NOPE + LA_VDIM)
        k_ref[h, :, :LA_NOPE] = kvf[:, kv0:kv0 + LA_NOPE].astype(BF16)
        k_ref[h, :, LA_NOPE:] = kr2
        v_ref[h] = kvf[:, kv0 + LA_NOPE:kv0 + LA_NOPE + LA_VDIM].astype(BF16)


def _la_prep(h, wt_small, gain_q, gain_kv, w_uq, w_ukv, cos, sin, tm=512):
    assert LA_HEADS % (LANES // LA_ROPE) == 0 and LA_NOPE == LANES and LA_DK == LA_NOPE + LANES
    d = h.shape[1]

    def full(a):
        return pl.BlockSpec(a.shape, lambda i: (0,) * a.ndim)

    return pl.pallas_call(
        _la_prep_kernel,
        grid=(M_ALL // tm,),
        in_specs=[pl.BlockSpec((tm, d), lambda i: (i, 0)), full(wt_small), full(gain_q), full(gain_kv),
                  full(w_uq), full(w_ukv),
                  pl.BlockSpec((tm, LANES), lambda i: (i, 0)),
                  pl.BlockSpec((tm, LANES), lambda i: (i, 0))],
        out_specs=[pl.BlockSpec((LA_HEADS, tm, LA_DK), lambda i: (0, i, 0)),
                   pl.BlockSpec((LA_HEADS, tm, LA_DK), lambda i: (0, i, 0)),
                   pl.BlockSpec((LA_HEADS, tm, LA_VDIM), lambda i: (0, i, 0)),
                   pl.BlockSpec((tm, LANES), lambda i: (i, 0))],
        out_shape=[jax.ShapeDtypeStruct((LA_HEADS, M_ALL, LA_DK), BF16),
                   jax.ShapeDtypeStruct((LA_HEADS, M_ALL, LA_DK), BF16),
                   jax.ShapeDtypeStruct((LA_HEADS, M_ALL, LA_VDIM), BF16),
                   jax.ShapeDtypeStruct((M_ALL, LANES), F32)],
        compiler_params=_params(("parallel",)),
        name="la_prep",
    )(h, wt_small, gain_q, gain_kv, w_uq, w_ukv, cos, sin)


ATT_TQ = 256
ATT_ROWS = 1024


def _attn_kernel(q_ref, kl_ref, vl_ref, kc_ref, vc_ref, kca_ref, vca_ref, o_ref, *, n_lat_blocks):
    i = pl.program_id(1)

    @pl.when(i < n_lat_blocks)
    def _():
        n_sub = ATT_ROWS // ATT_TQ

        def scores(sub):
            q = q_ref[sub * ATT_TQ:(sub + 1) * ATT_TQ, :]
            return _dot_nt(q, kl_ref[...]), _dot_nt(q, kc_ref[...])

        def softmax(s_l, s_c):
            mx = jnp.maximum(jnp.max(s_l, axis=1, keepdims=True), jnp.max(s_c, axis=1, keepdims=True))
            p_l = jnp.exp(s_l - mx)
            p_c = jnp.exp(s_c - mx)
            den = jnp.sum(p_l, axis=1, keepdims=True) + jnp.sum(p_c, axis=1, keepdims=True)
            return p_l.astype(BF16), p_c.astype(BF16), den

        def values(sub, p_l, p_c, den):
            o = _dot(p_l, vl_ref[...]) + _dot(p_c, vc_ref[...])
            o_ref[sub * ATT_TQ:(sub + 1) * ATT_TQ, :] = (o / den).astype(o_ref.dtype)

        pending_s = scores(0)
        for sub in range(n_sub):
            nxt = scores(sub + 1) if sub + 1 < n_sub else None
            values(sub, *softmax(*pending_s))
            pending_s = nxt

    @pl.when(i >= n_lat_blocks)
    def _():
        for sub in range(ATT_ROWS // CTX_LEN):
            rows = slice(sub * CTX_LEN, (sub + 1) * CTX_LEN)
            s = _dot_nt(q_ref[rows, :], kca_ref[rows, :])
            mx = jnp.max(s, axis=1, keepdims=True)
            p = jnp.exp(s - mx)
            den = jnp.sum(p, axis=1, keepdims=True)
            o = _dot(p.astype(BF16), vca_ref[rows, :])
            o_ref[rows, :] = (o / den).astype(o_ref.dtype)


def _attention(q, k, v, kv_group, need_ctx, name):
    assert M_CTX == ATT_ROWS and ATT_ROWS % ATT_TQ == 0
    n_heads, _, dk = q.shape
    dv = v.shape[2]
    n_lat_blocks = M_LAT // ATT_ROWS
    per_sample = SEQ // ATT_ROWS
    ctx0 = M_LAT // CTX_LEN
    n_blocks = n_lat_blocks + (1 if need_ctx else 0)

    def sample(i):
        return jnp.minimum(i // per_sample, BATCH - 1)

    def lat_spec(dim):
        return pl.BlockSpec((None, SEQ, dim), lambda h, i: (h // kv_group, sample(i), 0))

    def ctx_spec(dim):
        return pl.BlockSpec((None, CTX_LEN, dim), lambda h, i: (h // kv_group, ctx0 + sample(i), 0))

    def ctx_all_spec(dim):
        return pl.BlockSpec((None, M_CTX, dim), lambda h, i: (h // kv_group, n_lat_blocks, 0))

    return pl.pallas_call(
        functools.partial(_attn_kernel, n_lat_blocks=n_lat_blocks),
        grid=(n_heads, n_blocks),
        in_specs=[pl.BlockSpec((None, ATT_ROWS, dk), lambda h, i: (h, i, 0)),
                  lat_spec(dk), lat_spec(dv), ctx_spec(dk), ctx_spec(dv), ctx_all_spec(dk), ctx_all_spec(dv)],
        out_specs=pl.BlockSpec((ATT_ROWS, dv), lambda h, i: (i, h)),
        out_shape=jax.ShapeDtypeStruct((n_blocks * ATT_ROWS, n_heads * dv), BF16),
        compiler_params=_params(("parallel", "arbitrary")),
        name=name,
    )(q, k, v, k, v, k, v)


def _router_kernel(h_ref, w_ref, o_ref):
    logits = _dot_nt(w_ref[...], h_ref[...])
    mx = jnp.max(logits, axis=0, keepdims=True)
    e = jnp.exp(logits - mx)
    o_ref[...] = e / jnp.sum(e, axis=0, keepdims=True)


def _router(h, w_router_t, m_rows, tm=1024):
    d = h.shape[1]
    return pl.pallas_call(
        _router_kernel,
        grid=(m_rows // tm,),
        in_specs=[pl.BlockSpec((tm, d), lambda i: (i, 0)),
                  pl.BlockSpec((N_EXPERTS, d), lambda i: (0, 0))],
        out_specs=pl.BlockSpec((N_EXPERTS, tm), lambda i: (0, i)),
        out_shape=jax.ShapeDtypeStruct((N_EXPERTS, m_rows), F32),
        compiler_params=_params(("parallel",)),
        name="router",
    )(h, w_router_t)


PREFIX_BLOCK = 256


def _exclusive_prefix(x_bf, upper):
    n_tok = x_bf.shape[1]
    carry = jnp.zeros((x_bf.shape[0], 1), F32)
    parts = []
    for blk in range(n_tok // PREFIX_BLOCK):
        xb = x_bf[:, blk * PREFIX_BLOCK:(blk + 1) * PREFIX_BLOCK]
        parts.append(_dot(xb, upper) + carry)
        carry = carry + jnp.sum(xb.astype(F32), axis=1, keepdims=True)
    return parts[0] if len(parts) == 1 else jnp.concatenate(parts, axis=1)


def _bitonic_sort_descending(tiles):
    n_tiles = len(tiles)
    n = n_tiles * LANES
    lane = lax.broadcasted_iota(I32, tiles[0].shape, 1)
    k = 2
    while k <= n:
        j = k // 2
        while j >= 1:
            new_tiles = []
            for t in range(n_tiles):
                x = tiles[t]
                if j < LANES:
                    low = (lane & j) == 0
                    partner = jnp.where(low, pltpu.roll(x, LANES - j, 1), pltpu.roll(x, j, 1))
                else:
                    partner = tiles[t ^ (j // LANES)]
                big = jnp.maximum(x, partner)
                small = jnp.minimum(x, partner)
                up = ((t * LANES) & k) == 0
                if j >= LANES:
                    take_big = (((t * LANES) & j) == 0) == up
                    new_tiles.append(big if take_big else small)
                elif k >= LANES:
                    new_tiles.append(jnp.where(low, big, small) if up else jnp.where(low, small, big))
                else:
                    agree = (lane & j) * (k // j) == (lane & k)
                    new_tiles.append(jnp.where(agree, big, small))
            tiles = new_tiles
            j //= 2
        k *= 2
    return tiles


def _topk_kernel(aff_ref, rank_ref, *, cap):
    a = aff_ref[...]
    n_tok = a.shape[1]
    ordered = _bitonic_sort_descending([a[:, t * LANES:(t + 1) * LANES] for t in range(n_tok // LANES)])
    pos = cap - 1
    thr = ordered[pos // LANES][:, pos % LANES:pos % LANES + 1]
    gt = a > thr
    eq = a == thr
    need = cap - jnp.sum(jnp.where(gt, 1.0, 0.0), axis=1, keepdims=True)
    ri = lax.broadcasted_iota(I32, (PREFIX_BLOCK, PREFIX_BLOCK), 0)
    ci = lax.broadcasted_iota(I32, (PREFIX_BLOCK, PREFIX_BLOCK), 1)
    upper = jnp.where(ri < ci, 1.0, 0.0).astype(BF16)
    eq_before = _exclusive_prefix(jnp.where(eq, 1.0, 0.0).astype(BF16), upper)
    sel = gt | (eq & (eq_before < need))
    slot = _exclusive_prefix(jnp.where(sel, 1.0, 0.0).astype(BF16), upper)
    rank_ref[...] = jnp.where(sel, slot.astype(I32), -1)


def _topk(aff, n_tok, cap, blk0, n_sets):
    return pl.pallas_call(
        functools.partial(_topk_kernel, cap=cap),
        grid=(n_sets,),
        in_specs=[pl.BlockSpec((N_EXPERTS, n_tok), lambda s: (0, blk0 + s))],
        out_specs=pl.BlockSpec((N_EXPERTS, n_tok), lambda s: (0, s)),
        out_shape=jax.ShapeDtypeStruct((N_EXPERTS, n_sets * n_tok), I32),
        compiler_params=_params(("parallel",)),
        name=f"topk_{n_tok}",
    )(aff)


def _gather_kernel(rank_ref, aff_ref, h_ref, *rest, cap):
    xg_ref, val_ref = rest[-2:]
    for e in range(rank_ref.shape[0]):
        r = rank_ref[e]
        n_tok = r.shape[1]
        slot = lax.broadcasted_iota(I32, (cap, n_tok), 0)
        hit = r == slot
        onehot = jnp.where(hit, 1.0, 0.0).astype(BF16)
        xg_ref[e] = _dot(onehot, h_ref[...]).astype(xg_ref.dtype)
        val_ref[e] = jnp.sum(jnp.where(hit, aff_ref[e], 0.0), axis=1, keepdims=True)


def _gather(rank3, aff3, h, n_tok, cap, blk0, row0, rows_total, prev, experts_per_step):
    d = h.shape[1]
    rb0 = row0 // cap
    eg = experts_per_step
    in_specs = [pl.BlockSpec((eg, 1, n_tok), lambda b, e: (e, 0, b)),
                pl.BlockSpec((eg, 1, n_tok), lambda b, e: (e, 0, blk0 + b)),
                pl.BlockSpec((n_tok, d), lambda b, e: (blk0 + b, 0))]
    args = [rank3, aff3, h]
    aliases = {}
    if prev is not None:
        in_specs += [pl.BlockSpec(memory_space=pl.ANY), pl.BlockSpec(memory_space=pl.ANY)]
        args += list(prev)
        aliases = {3: 0, 4: 1}
    return pl.pallas_call(
        functools.partial(_gather_kernel, cap=cap),
        grid=(BATCH, N_EXPERTS // eg),
        in_specs=in_specs,
        out_specs=[pl.BlockSpec((eg, cap, d), lambda b, e: (e, rb0 + b, 0)),
                   pl.BlockSpec((eg, cap, 1), lambda b, e: (e, rb0 + b, 0))],
        out_shape=[jax.ShapeDtypeStruct((N_EXPERTS, rows_total, d), BF16),
                   jax.ShapeDtypeStruct((N_EXPERTS, rows_total, 1), F32)],
        input_output_aliases=aliases,
        compiler_params=_params(("parallel", "arbitrary")),
        name=f"moe_gather_{n_tok}",
    )(*args)


EXPERT_COLS = 512


def _expert_kernel(x_ref, w1_ref, w3_ref, w2_ref, val_ref, y_ref, acc_ref):
    f = pl.program_id(1)
    last = pl.num_programs(1) - 1
    col_slices = [slice(c0, c0 + EXPERT_COLS) for c0 in range(0, acc_ref.shape[1], EXPERT_COLS)]

    def hidden():
        x = x_ref[...]
        a = _dot(x, w1_ref[...].astype(BF16))
        u = _dot(x, w3_ref[...].astype(BF16))
        return (a * jax.nn.sigmoid(a) * u).astype(BF16)

    def down(hm, cols):
        return _dot(hm, w2_ref[:, cols].astype(BF16))

    @pl.when(f == 0)
    def _():
        hm = hidden()
        for cols in col_slices:
            acc_ref[:, cols] = down(hm, cols)

    @pl.when((f > 0) & (f < last))
    def _():
        hm = hidden()
        for cols in col_slices:
            acc_ref[:, cols] += down(hm, cols)

    @pl.when(f == last)
    def _():
        hm = hidden()
        val = val_ref[...]
        for cols in col_slices:
            y_ref[:, cols] = ((acc_ref[:, cols] + down(hm, cols)) * val).astype(y_ref.dtype)


def _experts(xg, vals, w_e1, w_e3, w_e2, layer, tf=256):
    n_exp, rows, d = xg.shape
    ff = w_e1.shape[3]
    assert ff // tf >= 2
    return pl.pallas_call(
        _expert_kernel,
        grid=(n_exp, ff // tf),
        in_specs=[pl.BlockSpec((None, rows, d), lambda e, f: (e, 0, 0)),
                  pl.BlockSpec((None, None, d, tf), lambda e, f: (layer, e, 0, f)),
                  pl.BlockSpec((None, None, d, tf), lambda e, f: (layer, e, 0, f)),
                  pl.BlockSpec((None, None, tf, d), lambda e, f: (layer, e, f, 0)),
                  pl.BlockSpec((None, rows, 1), lambda e, f: (e, 0, 0))],
        out_specs=pl.BlockSpec((None, rows, d), lambda e, f: (e, 0, 0)),
        out_shape=jax.ShapeDtypeStruct((n_exp, rows, d), BF16),
        scratch_shapes=[pltpu.VMEM((rows, d), F32)],
        compiler_params=_params(("parallel", "arbitrary")),
        name="moe_experts",
    )(xg, w_e1, w_e3, w_e2, vals)


COMBINE_LANES = 1024


def _combine_kernel(rank_ref, y_ref, x_ref, g_ref, *rest, cap):
    o_ref, lhs_ref = rest[-2:]
    n_slots = N_EXPERTS * cap
    chunk = min(COMBINE_LANES, n_slots)
    shift = cap.bit_length() - 1

    @pl.when(pl.program_id(2) == 0)
    def _():
        r = rank_ref[...].astype(F32).astype(BF16)
        tm = r.shape[0]
        for c0 in range(0, n_slots, chunk):
            lane = c0 + lax.broadcasted_iota(I32, (LANES, chunk), 1)
            row = lax.broadcasted_iota(I32, (LANES, chunk), 0)
            expand = jnp.where(lax.shift_right_logical(lane, shift) == row, 1.0, 0.0).astype(BF16)
            r_exp = _dot(r, expand)
            slot = (c0 + lax.broadcasted_iota(I32, (tm, chunk), 1)) & (cap - 1)
            lhs_ref[:, c0:c0 + chunk] = jnp.where(r_exp == slot.astype(F32), 1.0, 0.0).astype(BF16)

    y = y_ref[...]
    o_ref[...] = x_ref[...] + g_ref[...] * _dot(lhs_ref[...], y.reshape(n_slots, y.shape[2]))


def _combine(rank_t, y, x, gate_tab, out_prev, n_tok, cap, blk0, row0, tm, tn=1024):
    assert cap & (cap - 1) == 0
    d = x.shape[1]
    tpb = n_tok // tm
    xrow0 = blk0 * tpb
    rb0 = row0 // cap

    def grp(b):
        return b if n_tok == SEQ else BATCH

    in_specs = [pl.BlockSpec((tm, LANES), lambda b, i, j: (b * tpb + i, 0)),
                pl.BlockSpec((N_EXPERTS, cap, tn), lambda b, i, j: (0, rb0 + b, j)),
                pl.BlockSpec((tm, tn), lambda b, i, j: (xrow0 + b * tpb + i, j)),
                pl.BlockSpec((None, 1, tn), lambda b, i, j: (grp(b), 0, j))]
    args = [rank_t, y, x, gate_tab]
    aliases = {}
    if out_prev is not None:
        in_specs.append(pl.BlockSpec(memory_space=pl.ANY))
        args.append(out_prev)
        aliases = {4: 0}
    return pl.pallas_call(
        functools.partial(_combine_kernel, cap=cap),
        grid=(BATCH, tpb, d // tn),
        in_specs=in_specs,
        out_specs=pl.BlockSpec((tm, tn), lambda b, i, j: (xrow0 + b * tpb + i, j)),
        out_shape=jax.ShapeDtypeStruct(x.shape, F32),
        scratch_shapes=[pltpu.VMEM((tm, N_EXPERTS * cap), BF16)],
        input_output_aliases=aliases,
        compiler_params=_params(("parallel", "parallel", "arbitrary")),
        name=f"moe_combine_{n_tok}",
    )(*args)


def _expert_choice(h2, x, gate_tab, w_router_t, w_e1, w_e3, w_e2, layer, need_ctx):
    m_rows = h2.shape[0]
    aff = _router(h2, w_router_t, m_rows)
    aff3 = aff.reshape(N_EXPERTS, 1, m_rows)
    sets = [(SEQ, 0)]
    if need_ctx:
        sets.append((CTX_LEN, M_LAT // CTX_LEN))
    caps = [CAPACITY_FACTOR * n_tok // N_EXPERTS for n_tok, _ in sets]
    rows_total = sum(BATCH * cap for cap in caps)
    ranks, gathered, row0 = [], None, 0
    for (n_tok, blk0), cap in zip(sets, caps):
        rank = _topk(aff, n_tok, cap, blk0, BATCH)
        gathered = _gather(rank.reshape(N_EXPERTS, 1, BATCH * n_tok), aff3, h2, n_tok, cap, blk0, row0, rows_total,
                           gathered, experts_per_step=1 if n_tok == SEQ else N_EXPERTS)
        ranks.append((rank, row0))
        row0 += BATCH * cap
    y = _experts(gathered[0], gathered[1], w_e1, w_e3, w_e2, layer)
    out = None
    for (n_tok, blk0), cap, (rank, row0) in zip(sets, caps, ranks):
        rank_t = jnp.pad(rank.T, ((0, 0), (0, LANES - N_EXPERTS)))
        out = _combine(rank_t, y, x, gate_tab, out, n_tok, cap, blk0, row0, tm=min(512, n_tok))
    return out


def _rope_tables(dim):
    rows = SEQ // GRID_W
    quarter = dim // 4
    inv = ROPE_BASE ** (-jnp.arange(quarter, dtype=F32) / quarter)
    row = jnp.broadcast_to(jnp.arange(rows, dtype=F32)[:, None], (rows, GRID_W)).reshape(-1)
    col = jnp.broadcast_to(jnp.arange(GRID_W, dtype=F32)[None, :], (rows, GRID_W)).reshape(-1)
    ang_r = row[:, None] * inv[None, :]
    ang_c = col[:, None] * inv[None, :]
    cos = jnp.concatenate([jnp.cos(ang_r), jnp.cos(ang_r), jnp.cos(ang_c), jnp.cos(ang_c)], axis=-1)
    sin = jnp.concatenate([-jnp.sin(ang_r), jnp.sin(ang_r), -jnp.sin(ang_c), jnp.sin(ang_c)], axis=-1)
    cos = jnp.tile(cos, (BATCH, LANES // dim))
    sin = jnp.tile(sin, (BATCH, LANES // dim))
    cos = jnp.concatenate([cos, jnp.ones((M_CTX, LANES), F32)], axis=0)
    sin = jnp.concatenate([sin, jnp.zeros((M_CTX, LANES), F32)], axis=0)
    return cos, sin


def _halves_major(a, axis=-1):
    axis = axis % a.ndim
    quarter = a.shape[axis] // 4
    a4 = a.reshape(*a.shape[:axis], 2, 2, quarter, *a.shape[axis + 1:])
    return jnp.swapaxes(a4, axis, axis + 1).reshape(a.shape)


def _layer(x_lat, x_ctx, mods, layer, wt_all, ml_gate_bias, ml_norm, la_q_norm, la_kv_norm, la_w_uq, la_w_ukv,
           ga_q_norm, ga_k_norm, w_br_ml, w_br_la, w_br_ga, w_out, w_router, w_e1, w_e3, w_e2, ropes, need_ctx):
    d = D_MODEL
    m_rows = M_ALL if need_ctx else M_LAT
    sh1, sc1, g1, sh2, sc2, g2 = [mods[:N_GROUPS, k * d:(k + 1) * d].reshape(N_GROUPS, 1, d) for k in range(6)]
    (cos_la, sin_la), (cos_ga, sin_ga) = ropes

    n_qk = GA_HEADS + GA_KV_HEADS
    wt_rows = wt_all.reshape(-1, d)
    layer_row0 = layer * wt_all.shape[1]
    wt_sm, wt_gaf = lax.optimization_barrier((wt_rows[layer_row0 + OFF_MLG:layer_row0 + OFF_GAQ],
                                              wt_rows[layer_row0 + OFF_GAQ:layer_row0 + OFF_GATE]))
    wt_small = jnp.concatenate([wt_sm[OFF_CQ - OFF_MLG:], wt_sm[:OFF_CQ - OFF_MLG],
                                jnp.zeros((SMALL_WIDTH - (OFF_GAQ - OFF_MLG), d), F32)], axis=0).astype(BF16)
    wt_gaqk = _halves_major(wt_gaf[:OFF_GAV - OFF_GAQ].reshape(n_qk, GA_HEAD_DIM, d), axis=1).reshape(-1, d)
    wt_ga = jnp.concatenate([wt_gaqk, wt_gaf[OFF_GAV - OFF_GAQ:]], axis=0).astype(BF16)
    w_uq = la_w_uq.reshape(LA_Q_RANK, LA_HEADS, LA_NOPE + LA_ROPE)
    w_uq = jnp.concatenate([w_uq[:, :, :LA_NOPE].reshape(LA_Q_RANK, LA_HEADS * LA_NOPE),
                            w_uq[:, :, LA_NOPE:].reshape(LA_Q_RANK, LA_HEADS * LA_ROPE)], axis=1).astype(BF16)

    h = _rms_mod(x_lat, x_ctx, 1.0 + sc1, sh1, M_ALL)
    p_ml = _mm(h, wt_rows, layer_row0, OFF_MLG, name="proj_ml")
    q_la, k_la, v_la, tail = _la_prep(h, wt_small, la_q_norm.reshape(1, LA_Q_RANK),
                                      la_kv_norm.reshape(1, LA_KV_RANK), w_uq, la_w_ukv.astype(BF16), cos_la, sin_la)
    q_ga, k_ga, v_ga = _ga_prep(h, wt_ga, _halves_major(ga_q_norm.reshape(1, GA_HEAD_DIM)),
                                _halves_major(ga_k_norm.reshape(1, GA_HEAD_DIM)), cos_ga, sin_ga)

    gates = (tail[:, TAIL_MLG:TAIL_MLG + ML_GATES] + ml_gate_bias[None, :]).reshape(M_ALL, 2, 2, ML_HEADS)
    gates = jnp.stack([gates[:, 0, 0], jax.nn.log_sigmoid(gates[:, 0, 1]),
                       gates[:, 1, 0], jax.nn.log_sigmoid(gates[:, 1, 1])], axis=-1)
    n_lat = SEQ // ML_CHUNK
    g_lat = gates[:M_LAT].reshape(BATCH, n_lat, ML_CHUNK, ML_HEADS, 4)
    g_ctx = gates[M_LAT:].reshape(BATCH, CTX_LEN, ML_HEADS, 4)
    y_ml_lat, y_ml_ctx = _mlstm(p_ml, jnp.transpose(g_lat, (0, 3, 1, 4, 2)), jnp.transpose(g_lat, (0, 3, 1, 2, 4)),
                                jnp.transpose(g_ctx, (0, 2, 3, 1)), jnp.transpose(g_ctx, (0, 2, 1, 3)),
                                ml_norm.reshape(1, ML_WIDTH))

    y_la = _attention(q_la, k_la, v_la, 1, need_ctx, "attn_mla")
    y_ga = _attention(q_ga, k_ga, v_ga, GA_HEADS // GA_KV_HEADS, need_ctx, "attn_gqa")

    merged = _merge(h, y_ml_lat, y_ml_ctx, y_la, y_ga, wt_rows, layer, layer_row0 + OFF_GATE, w_br_ml, w_br_la,
                    w_br_ga, m_rows)
    x, h2 = _out_proj(merged, w_out.astype(BF16), x_lat, x_ctx if need_ctx else None, g1, 1.0 + sc2, sh2, m_rows)
    return _expert_choice(h2, x, g2, w_router.T.astype(BF16), w_e1, w_e3, w_e2, layer, need_ctx)


def kernel(x, c, ctx, c_ctx, w_mod, b_mod, w_in, ml_gate_bias, ml_norm, la_q_norm, la_kv_norm, la_w_uq, la_w_ukv,
           ga_q_norm, ga_k_norm, w_br_ml, w_br_la, w_br_ga, w_out, w_router, w_e1, w_e3, w_e2, final_norm):
    d = D_MODEL
    cc = jnp.concatenate([c, c_ctx[None, :], jnp.zeros((MODS_ROWS - N_GROUPS, d), F32)], axis=0)
    mods = _mods(cc, w_mod, b_mod.reshape(DEPTH, 1, 6 * d))
    ropes = (_rope_tables(LA_ROPE), tuple(_halves_major(t) for t in _rope_tables(GA_HEAD_DIM)))
    xs, xs_ctx = x.reshape(M_LAT, d), ctx.reshape(M_CTX, d)
    wt_in = jnp.swapaxes(w_in, 1, 2)
    for l in range(DEPTH):
        need_ctx = l < DEPTH - 1
        xs = _layer(xs, xs_ctx, mods[l], l, wt_in, ml_gate_bias[l], ml_norm[l], la_q_norm[l], la_kv_norm[l],
                    la_w_uq[l], la_w_ukv[l], ga_q_norm[l], ga_k_norm[l], w_br_ml, w_br_la, w_br_ga,
                    w_out[l], w_router[l], w_e1, w_e3, w_e2, ropes, need_ctx)
        xs_ctx = None
    out = _rms_gain(xs, final_norm.reshape(1, d), F32, name="final_norm")
    return out.reshape(BATCH, SEQ, d)
```

```python
import functools

import jax
import jax.numpy as jnp
from jax import lax
from jax.experimental import pallas as pl
from jax.experimental.pallas import tpu as pltpu

F32 = jnp.float32
BF16 = jnp.bfloat16
I32 = jnp.int32

D_MODEL = 2048
BATCH = 4
SEQ = 2048
DEPTH = 2
GRID_W = 64
CTX_LEN = 256
EPS = 1e-6
ROPE_BASE = 10000.0

ML_HEADS = 4
ML_HEAD_DIM = 256
ML_WIDTH = ML_HEADS * ML_HEAD_DIM
ML_GATES = 2 * 2 * ML_HEADS
ML_CHUNK = 256

LA_HEADS = 8
LA_NOPE = 128
LA_ROPE = 64
LA_VDIM = 128
LA_Q_RANK = 512
LA_KV_RANK = 256
LA_DK = 256

GA_HEADS = 8
GA_KV_HEADS = 2
GA_HEAD_DIM = 128
GA_WIDTH = GA_HEADS * GA_HEAD_DIM
GA_KV_WIDTH = GA_KV_HEADS * GA_HEAD_DIM

N_EXPERTS = 16
EXPERT_FF = 1024
CAPACITY_FACTOR = 2

M_LAT = BATCH * SEQ
M_CTX = BATCH * CTX_LEN
M_ALL = M_LAT + M_CTX
MODS_ROWS = 16
N_GROUPS = BATCH + 1

LANES = 128
SUBLANES = 8

OFF_MLG = 4 * ML_WIDTH
OFF_CQ = OFF_MLG + ML_GATES
OFF_CKV = OFF_CQ + LA_Q_RANK
OFF_KR = OFF_CKV + LA_KV_RANK
OFF_GAQ = OFF_KR + LA_ROPE
OFF_GAK = OFF_GAQ + GA_WIDTH
OFF_GAV = OFF_GAK + GA_KV_WIDTH
OFF_GATE = OFF_GAV + GA_KV_WIDTH
SM_CKV = LA_Q_RANK
SM_TAIL = SM_CKV + LA_KV_RANK
SMALL_WIDTH = SM_TAIL + LANES
TAIL_MLG = LA_ROPE

VMEM_LIMIT = 56 * 1024 * 1024


def _params(semantics, vmem=VMEM_LIMIT):
    return pltpu.CompilerParams(dimension_semantics=semantics, vmem_limit_bytes=vmem)


def _dot(a, b):
    return jnp.dot(a, b, preferred_element_type=F32)


def _dot_nt(a, b):
    return lax.dot_general(a, b, (((1,), (1,)), ((), ())), preferred_element_type=F32)


def _rms(x):
    return x * lax.rsqrt(jnp.mean(x * x, axis=-1, keepdims=True) + EPS)


def _group_of_block(tm):
    return lambda i: (i * tm) // SEQ


def _mods_kernel(c_ref, w_ref, b_ref, o_ref):
    c = c_ref[...]
    a = (c * jax.nn.sigmoid(c)).astype(BF16)
    part = _dot(a, w_ref[...].astype(BF16))

    @pl.when(pl.program_id(1) == 0)
    def _():
        o_ref[...] = part + b_ref[...]

    @pl.when(pl.program_id(1) > 0)
    def _():
        o_ref[...] += part


def _mods(cc, w_mod, b_mod, tk=256):
    depth, d, n = w_mod.shape
    rows = cc.shape[0]
    return pl.pallas_call(
        _mods_kernel,
        grid=(depth, d // tk),
        in_specs=[pl.BlockSpec((rows, tk), lambda l, k: (0, k)),
                  pl.BlockSpec((None, tk, n), lambda l, k: (l, k, 0)),
                  pl.BlockSpec((None, 1, n), lambda l, k: (l, 0, 0))],
        out_specs=pl.BlockSpec((None, rows, n), lambda l, k: (l, 0, 0)),
        out_shape=jax.ShapeDtypeStruct((depth, rows, n), F32),
        compiler_params=_params(("parallel", "arbitrary")),
        name="mods",
    )(cc, w_mod, b_mod)


def _row_sources(x_lat, x_ctx, tm, n_grid_axes=1):
    d = x_lat.shape[1]
    n_lat_tiles = M_LAT // tm

    def spec(fn):
        return pl.BlockSpec((tm, d), (lambda i: (fn(i), 0)) if n_grid_axes == 1 else (lambda j, i: (fn(i), 0)))

    if x_ctx is None:
        return [spec(lambda i: i)], [x_lat], lambda i, refs: refs[0][...]
    specs = [spec(lambda i: jnp.minimum(i, n_lat_tiles - 1)), spec(lambda i: jnp.maximum(i - n_lat_tiles, 0))]
    return specs, [x_lat, x_ctx], lambda i, refs: jnp.where(i < n_lat_tiles, refs[0][...], refs[1][...])


def _rms_kernel(*refs, pick):
    scale_ref, shift_ref, o_ref = refs[-3:]
    x = pick(pl.program_id(0), refs[:-3])
    o_ref[...] = (_rms(x) * scale_ref[...] + shift_ref[...]).astype(o_ref.dtype)


def _rms_gain_kernel(x_ref, scale_ref, o_ref):
    o_ref[...] = (_rms(x_ref[...]) * scale_ref[...]).astype(o_ref.dtype)


def _rms_mod(x_lat, x_ctx, scale_tab, shift_tab, m_rows, tm=512):
    d = x_lat.shape[1]
    grp = _group_of_block(tm)
    x_specs, x_args, pick = _row_sources(x_lat, x_ctx, tm)
    return pl.pallas_call(
        functools.partial(_rms_kernel, pick=pick),
        grid=(m_rows // tm,),
        in_specs=x_specs + [pl.BlockSpec((None, 1, d), lambda i: (grp(i), 0, 0)),
                            pl.BlockSpec((None, 1, d), lambda i: (grp(i), 0, 0))],
        out_specs=pl.BlockSpec((tm, d), lambda i: (i, 0)),
        out_shape=jax.ShapeDtypeStruct((m_rows, d), BF16),
        compiler_params=_params(("parallel",)),
        name="rms_mod",
    )(*x_args, scale_tab, shift_tab)


def _rms_gain(x, gain, out_dtype, tm=512, name="rms_gain"):
    m_rows, width = x.shape
    return pl.pallas_call(
        _rms_gain_kernel,
        grid=(m_rows // tm,),
        in_specs=[pl.BlockSpec((tm, width), lambda i: (i, 0)),
                  pl.BlockSpec((1, width), lambda i: (0, 0))],
        out_specs=pl.BlockSpec((tm, width), lambda i: (i, 0)),
        out_shape=jax.ShapeDtypeStruct((m_rows, width), out_dtype),
        compiler_params=_params(("parallel",)),
        name=name,
    )(x, gain)


def _mm_kernel(a_ref, wt_ref, o_ref, wbf_ref):
    @pl.when(pl.program_id(1) == 0)
    def _():
        wbf_ref[...] = wt_ref[...].astype(BF16)

    o_ref[...] = _dot_nt(a_ref[...], wbf_ref[...]).astype(o_ref.dtype)


def _mm(a, wt_all, row0, n_cols, *, tm=1024, tn=1024, out_dtype=F32, name="mm"):
    m_rows, k = a.shape
    assert m_rows % tm == 0 and n_cols % tn == 0
    return pl.pallas_call(
        _mm_kernel,
        grid=(n_cols // tn, m_rows // tm),
        in_specs=[pl.BlockSpec((tm, k), lambda j, i: (i, 0)),
                  pl.BlockSpec((pl.Element(tn), pl.Element(k)),
                               lambda j, i: (pl.multiple_of(row0 + j * tn, SUBLANES), 0))],
        out_specs=pl.BlockSpec((tm, tn), lambda j, i: (i, j)),
        out_shape=jax.ShapeDtypeStruct((m_rows, n_cols), out_dtype),
        scratch_shapes=[pltpu.VMEM((tn, k), BF16)],
        compiler_params=_params(("parallel", "arbitrary")),
        name=name,
    )(a, wt_all)


def _out_proj_kernel(a_ref, w_ref, *refs, pick):
    g_ref, scale_ref, shift_ref, x_out_ref, h_out_ref = refs[-5:]
    x = pick(pl.program_id(0), refs[:-5])
    x_new = x + g_ref[...] * _dot(a_ref[...], w_ref[...])
    x_out_ref[...] = x_new
    h_out_ref[...] = (_rms(x_new) * scale_ref[...] + shift_ref[...]).astype(h_out_ref.dtype)


def _out_proj(a, w, x_lat, x_ctx, gate_tab, scale_tab, shift_tab, m_rows, tm=512):
    k = a.shape[1]
    d = w.shape[1]
    grp = _group_of_block(tm)
    x_specs, x_args, pick = _row_sources(x_lat, x_ctx, tm)
    tab_spec = pl.BlockSpec((None, 1, d), lambda i: (grp(i), 0, 0))
    return pl.pallas_call(
        functools.partial(_out_proj_kernel, pick=pick),
        grid=(m_rows // tm,),
        in_specs=[pl.BlockSpec((tm, k), lambda i: (i, 0)),
                  pl.BlockSpec((k, d), lambda i: (0, 0))] + x_specs + [tab_spec, tab_spec, tab_spec],
        out_specs=[pl.BlockSpec((tm, d), lambda i: (i, 0)), pl.BlockSpec((tm, d), lambda i: (i, 0))],
        out_shape=[jax.ShapeDtypeStruct((m_rows, d), F32), jax.ShapeDtypeStruct((m_rows, d), BF16)],
        compiler_params=_params(("parallel",)),
        name="out_proj",
    )(a, w, *x_args, gate_tab, scale_tab, shift_tab)


def _merge_kernel(h_ref, y0l_ref, y0c_ref, y1_ref, y2_ref, wg0_ref, wg1_ref, wg2_ref, wb0_ref, wb1_ref, wb2_ref,
                  o_ref, wg_bf_ref, wb_bf_ref, *, n_lat_tiles):
    @pl.when(pl.program_id(1) == 0)
    def _():
        for b, (wg_ref, wb_ref) in enumerate(((wg0_ref, wb0_ref), (wg1_ref, wb1_ref), (wg2_ref, wb2_ref))):
            wg_bf_ref[b] = wg_ref[...].astype(BF16)
            wb_bf_ref[b] = wb_ref[...].astype(BF16)

    h = h_ref[...]
    y0 = jnp.where(pl.program_id(1) < n_lat_tiles, y0l_ref[...], y0c_ref[...])
    acc = jax.nn.sigmoid(_dot_nt(h, wg_bf_ref[0])) * _dot(y0, wb_bf_ref[0])
    acc = acc + jax.nn.sigmoid(_dot_nt(h, wg_bf_ref[1])) * _dot(y1_ref[...], wb_bf_ref[1])
    acc = acc + jax.nn.sigmoid(_dot_nt(h, wg_bf_ref[2])) * _dot(y2_ref[...], wb_bf_ref[2])
    o_ref[...] = acc.astype(o_ref.dtype)


def _merge(h, y_ml_lat, y_ml_ctx, y_la, y_ga, wt_all, layer, gate_row0, w_br_ml, w_br_la, w_br_ga, m_rows,
           tm=512, tn=512):
    d = D_MODEL
    nb = d // tn
    n_lat_tiles = M_LAT // tm
    y_spec = pl.BlockSpec((tm, ML_WIDTH), lambda j, i: (i, 0))
    wb_spec = pl.BlockSpec((None, ML_WIDTH, tn), lambda j, i: (layer, 0, j), pipeline_mode=pl.Buffered(1))

    def wg_spec(branch):
        return pl.BlockSpec((pl.Element(tn), pl.Element(d)),
                            lambda j, i: (pl.multiple_of(gate_row0 + branch * d + j * tn, SUBLANES), 0),
                            pipeline_mode=pl.Buffered(1))

    return pl.pallas_call(
        functools.partial(_merge_kernel, n_lat_tiles=n_lat_tiles),
        grid=(nb, m_rows // tm),
        in_specs=[pl.BlockSpec((tm, d), lambda j, i: (i, 0)),
                  pl.BlockSpec((tm, ML_WIDTH), lambda j, i: (jnp.minimum(i, n_lat_tiles - 1), 0)),
                  pl.BlockSpec((tm, ML_WIDTH), lambda j, i: (jnp.maximum(i - n_lat_tiles, 0), 0)),
                  y_spec, y_spec, wg_spec(0), wg_spec(1), wg_spec(2), wb_spec, wb_spec, wb_spec],
        out_specs=pl.BlockSpec((tm, tn), lambda j, i: (i, j)),
        out_shape=jax.ShapeDtypeStruct((m_rows, d), BF16),
        scratch_shapes=[pltpu.VMEM((3, tn, d), BF16), pltpu.VMEM((3, ML_WIDTH, tn), BF16)],
        compiler_params=_params(("parallel", "arbitrary")),
        name="merge",
    )(h, y_ml_lat, y_ml_ctx, y_la, y_ga, wt_all, wt_all, wt_all, w_br_ml, w_br_la, w_br_ga)


def _mlstm_kernel(ql_ref, kl_ref, vl_ref, ol_ref, qc_ref, kc_ref, vc_ref, oc_ref,
                  grl_ref, gcl_ref, grc_ref, gcc_ref, gain_ref, yl_ref, yc_ref,
                  qb_ref, kt_ref, vb_ref, s_ref, hsum_ref):
    length = ML_CHUNK
    hd = ML_HEAD_DIM
    n_lat = SEQ // length
    n_chunks = n_lat + 1
    scale = hd ** -0.5

    def load(c, ctx_ref, lat_ref):
        return ctx_ref[...] if c == 0 else lat_ref[(c - 1) * length:c * length, :]

    for c in range(n_chunks):
        qb = (load(c, qc_ref, ql_ref) * scale).astype(BF16)
        kt = jnp.transpose(load(c, kc_ref, kl_ref)).astype(BF16)
        qb_ref[c] = qb
        kt_ref[c] = kt
        vb_ref[c] = load(c, vc_ref, vl_ref).astype(BF16)
        s_ref[c] = _dot(qb, kt)

    ri = lax.broadcasted_iota(I32, (length, length), 0)
    ci = lax.broadcasted_iota(I32, (length, length), 1)
    written = set()

    def chunk_step(direction, c, state, last):
        m, ct, n_vec = state
        reverse = direction == 1
        row_i = 2 * direction
        allowed = (ci >= ri) if reverse else (ci <= ri)
        allowed_t = (ri >= ci) if reverse else (ri <= ci)
        g_rows = grc_ref[...] if c == 0 else grl_ref[c - 1]
        g_cols = gcc_ref[...] if c == 0 else gcl_ref[c - 1]
        li_r = g_rows[row_i:row_i + 1, :]
        lf_r = g_rows[row_i + 1:row_i + 2, :]
        li_c = g_cols[:, row_i:row_i + 1]
        lf_c = g_cols[:, row_i + 1:row_i + 2]
        cum_c = jnp.sum(jnp.where(allowed, lf_r, 0.0), axis=1, keepdims=True)
        cum_r = jnp.sum(jnp.where(allowed_t, lf_c, 0.0), axis=0, keepdims=True)
        total = jnp.sum(lf_r, axis=1, keepdims=True)
        dmat = jnp.where(allowed, cum_c - cum_r + li_r, -jnp.inf)
        m_loc = jnp.max(dmat, axis=1, keepdims=True)
        s_loc = s_ref[c] * jnp.exp(dmat - m_loc)
        intra = _dot(s_loc.astype(BF16), vb_ref[c])
        intra_sum = jnp.sum(s_loc, axis=1, keepdims=True)
        inter = cum_c + m
        m_row = jnp.maximum(inter, m_loc)
        shrink = jnp.exp(-jnp.abs(inter - m_loc))
        f_loc = jnp.where(m_loc >= inter, 1.0, shrink)
        w_inter = jnp.where(inter >= m_loc, 1.0, shrink)
        q32 = load(c, qc_ref, ql_ref) * scale
        num = intra * f_loc + w_inter * _dot(qb_ref[c], ct.astype(BF16))
        den = intra_sum * f_loc + w_inter * jnp.sum(q32 * n_vec, axis=1, keepdims=True)
        h_out = num / jnp.maximum(jnp.abs(den), jnp.exp(-m_row))
        rows = slice(c * length, (c + 1) * length)
        if c in written:
            hsum_ref[rows, :] += h_out
        else:
            hsum_ref[rows, :] = h_out
            written.add(c)
        if last:
            return state
        g_r = total - cum_r + li_r
        g_c = total - cum_c + li_c
        m_new = jnp.maximum(total + m, jnp.max(g_r, axis=1, keepdims=True))
        wk_c = jnp.exp(g_c - m_new)
        decay = jnp.exp(total + m - m_new)
        wv = (wk_c * load(c, vc_ref, vl_ref)).astype(BF16)
        ct_new = decay * ct + _dot(kt_ref[c], wv)
        n_new = decay * n_vec + jnp.sum(wk_c * load(c, kc_ref, kl_ref), axis=0, keepdims=True)
        return m_new, ct_new, n_new

    orders = ([0] + list(range(1, n_chunks)), [0] + [n_lat - j for j in range(n_lat)])
    zero_state = (jnp.zeros((1, 1), F32), jnp.zeros((hd, hd), F32), jnp.zeros((1, hd), F32))
    states = [zero_state, zero_state]
    for pos in range(n_chunks):
        for direction in range(2):
            states[direction] = chunk_step(direction, orders[direction][pos], states[direction],
                                           last=pos == n_chunks - 1)

    gain = gain_ref[...]

    def finish(rows, o_pre):
        return (_rms(hsum_ref[rows, :]) * gain * jax.nn.sigmoid(o_pre)).astype(BF16)

    yc_ref[...] = finish(slice(0, CTX_LEN), oc_ref[...])
    yl_ref[...] = finish(slice(CTX_LEN, CTX_LEN + SEQ), ol_ref[...])


def _mlstm(p_ml, g_row_lat, g_col_lat, g_row_ctx, g_col_ctx, gain):
    assert CTX_LEN == ML_CHUNK
    hd = ML_HEAD_DIM
    length = ML_CHUNK
    n_lat = SEQ // length
    ctx0 = M_LAT // CTX_LEN

    def lat_spec(col):
        return pl.BlockSpec((SEQ, hd), lambda b, h: (b, col * ML_HEADS + h))

    def ctx_spec(col):
        return pl.BlockSpec((CTX_LEN, hd), lambda b, h: (ctx0 + b, col * ML_HEADS + h))

    return pl.pallas_call(
        _mlstm_kernel,
        grid=(BATCH, ML_HEADS),
        in_specs=[lat_spec(0), lat_spec(1), lat_spec(2), lat_spec(3),
                  ctx_spec(0), ctx_spec(1), ctx_spec(2), ctx_spec(3),
                  pl.BlockSpec((None, None, n_lat, 4, length), lambda b, h: (b, h, 0, 0, 0)),
                  pl.BlockSpec((None, None, n_lat, length, 4), lambda b, h: (b, h, 0, 0, 0)),
                  pl.BlockSpec((None, None, 4, length), lambda b, h: (b, h, 0, 0)),
                  pl.BlockSpec((None, None, length, 4), lambda b, h: (b, h, 0, 0)),
                  pl.BlockSpec((1, hd), lambda b, h: (0, h))],
        out_specs=[pl.BlockSpec((SEQ, hd), lambda b, h: (b, h)),
                   pl.BlockSpec((CTX_LEN, hd), lambda b, h: (b, h))],
        out_shape=[jax.ShapeDtypeStruct((M_LAT, ML_WIDTH), BF16),
                   jax.ShapeDtypeStruct((M_CTX, ML_WIDTH), BF16)],
        scratch_shapes=[pltpu.VMEM((1 + n_lat, length, hd), BF16),
                        pltpu.VMEM((1 + n_lat, hd, length), BF16),
                        pltpu.VMEM((1 + n_lat, length, hd), BF16),
                        pltpu.VMEM((1 + n_lat, length, length), F32),
                        pltpu.VMEM((CTX_LEN + SEQ, hd), F32)],
        compiler_params=_params(("parallel", "parallel")),
        name="mlstm",
    )(p_ml, p_ml, p_ml, p_ml, p_ml, p_ml, p_ml, p_ml, g_row_lat, g_col_lat, g_row_ctx, g_col_ctx, gain)


def _rope_partner(x, half):
    lane = lax.broadcasted_iota(I32, x.shape, 1)
    first = (lane & (2 * half - 1)) < half
    return jnp.where(first, pltpu.roll(x, LANES - half, 1), pltpu.roll(x, half, 1))


GA_PIECE = 512


def _ga_prep_kernel(h_ref, wa_ref, wb_ref, wc_ref, gq_ref, gk_ref, cos_ref, sin_ref, q_ref, k_ref, v_ref):
    h = h_ref[...]
    cos = cos_ref[...]
    sin = sin_ref[...]
    dh = GA_HEAD_DIM
    per_piece = GA_PIECE // dh

    def norm_rope(x, gain):
        y = _rms(x) * gain
        return y * cos + pltpu.roll(y, dh // 2, 1) * sin

    for piece, w_ref in enumerate((wa_ref, wb_ref)):
        p = _dot_nt(h, w_ref[...])
        for j in range(per_piece):
            q = norm_rope(p[:, j * dh:(j + 1) * dh], gq_ref[...])
            q_ref[piece * per_piece + j] = (q * dh ** -0.5).astype(BF16)
    p = _dot_nt(h, wc_ref[...])
    for g in range(GA_KV_HEADS):
        k_ref[g] = norm_rope(p[:, g * dh:(g + 1) * dh], gk_ref[...]).astype(BF16)
        v_ref[g] = p[:, GA_KV_WIDTH + g * dh:GA_KV_WIDTH + (g + 1) * dh].astype(BF16)


def _ga_prep(h, wt_ga, gain_q, gain_k, cos, sin, tm=512):
    assert GA_WIDTH == 2 * GA_PIECE and 2 * GA_KV_WIDTH == GA_PIECE
    d = h.shape[1]
    dh = GA_HEAD_DIM

    def head_spec(n_heads):
        return pl.BlockSpec((n_heads, tm, dh), lambda i: (0, i, 0))

    def piece_spec(j):
        return pl.BlockSpec((GA_PIECE, d), lambda i: (j, 0))

    return pl.pallas_call(
        _ga_prep_kernel,
        grid=(M_ALL // tm,),
        in_specs=[pl.BlockSpec((tm, d), lambda i: (i, 0)), piece_spec(0), piece_spec(1), piece_spec(2),
                  pl.BlockSpec((1, dh), lambda i: (0, 0)),
                  pl.BlockSpec((1, dh), lambda i: (0, 0)),
                  pl.BlockSpec((tm, dh), lambda i: (i, 0)),
                  pl.BlockSpec((tm, dh), lambda i: (i, 0))],
        out_specs=[head_spec(GA_HEADS), head_spec(GA_KV_HEADS), head_spec(GA_KV_HEADS)],
        out_shape=[jax.ShapeDtypeStruct((GA_HEADS, M_ALL, dh), BF16),
                   jax.ShapeDtypeStruct((GA_KV_HEADS, M_ALL, dh), BF16),
                   jax.ShapeDtypeStruct((GA_KV_HEADS, M_ALL, dh), BF16)],
        compiler_params=_params(("parallel",)),
        name="ga_prep",
    )(h, wt_ga, wt_ga, wt_ga, gain_q, gain_k, cos, sin)


def _la_prep_kernel(h_ref, wsm_ref, gq_ref, gkv_ref, wuq_ref, wukv_ref, cos_ref, sin_ref,
                    q_ref, k_ref, v_ref, tail_ref):
    p = _dot_nt(h_ref[...], wsm_ref[...])
    tail = p[:, SM_TAIL:SM_TAIL + LANES]
    tail_ref[...] = tail
    cq = (_rms(p[:, :LA_Q_RANK]) * gq_ref[...]).astype(BF16)
    ckv = (_rms(p[:, SM_CKV:SM_CKV + LA_KV_RANK]) * gkv_ref[...]).astype(BF16)
    qf = _dot(cq, wuq_ref[...])
    kvf = _dot(ckv, wukv_ref[...])
    cos = cos_ref[...]
    sin = sin_ref[...]
    scale = (LA_NOPE + LA_ROPE) ** -0.5
    low = lax.broadcasted_iota(I32, tail.shape, 1) < LA_ROPE

    def rope(x):
        return x * cos + _rope_partner(x, LA_ROPE // 4) * sin

    kr = rope(tail)
    kr2 = jnp.where(low, kr, pltpu.roll(kr, LA_ROPE, 1)).astype(BF16)
    heads_per_tile = LANES // LA_ROPE
    for j in range(LA_HEADS // heads_per_tile):
        c0 = LA_HEADS * LA_NOPE + j * LANES
        qr = rope(qf[:, c0:c0 + LANES]) * scale
        q_ref[2 * j, :, LA_NOPE:] = jnp.where(low, qr, 0.0).astype(BF16)
        q_ref[2 * j + 1, :, LA_NOPE:] = jnp.where(low, 0.0, qr).astype(BF16)
    for h in range(LA_HEADS):
        q_ref[h, :, :LA_NOPE] = (qf[:, h * LA_NOPE:(h + 1) * LA_NOPE] * scale).astype(BF16)
        kv0 = h * (LA_NOPE + LA_VDIM)
        k_ref[h, :, :LA_NOPE] = kvf[:, kv0:kv0 + LA_NOPE].astype(BF16)
        k_ref[h, :, LA_NOPE:] = kr2
        v_ref[h] = kvf[:, kv0 + LA_NOPE:kv0 + LA_NOPE + LA_VDIM].astype(BF16)


def _la_prep(h, wt_small, gain_q, gain_kv, w_uq, w_ukv, cos, sin, tm=512):
    assert LA_HEADS % (LANES // LA_ROPE) == 0 and LA_NOPE == LANES and LA_DK == LA_NOPE + LANES
    d = h.shape[1]

    def full(a):
        return pl.BlockSpec(a.shape, lambda i: (0,) * a.ndim)

    return pl.pallas_call(
        _la_prep_kernel,
        grid=(M_ALL // tm,),
        in_specs=[pl.BlockSpec((tm, d), lambda i: (i, 0)), full(wt_small), full(gain_q), full(gain_kv),
                  full(w_uq), full(w_ukv),
                  pl.BlockSpec((tm, LANES), lambda i: (i, 0)),
                  pl.BlockSpec((tm, LANES), lambda i: (i, 0))],
        out_specs=[pl.BlockSpec((LA_HEADS, tm, LA_DK), lambda i: (0, i, 0)),
                   pl.BlockSpec((LA_HEADS, tm, LA_DK), lambda i: (0, i, 0)),
                   pl.BlockSpec((LA_HEADS, tm, LA_VDIM), lambda i: (0, i, 0)),
                   pl.BlockSpec((tm, LANES), lambda i: (i, 0))],
        out_shape=[jax.ShapeDtypeStruct((LA_HEADS, M_ALL, LA_DK), BF16),
                   jax.ShapeDtypeStruct((LA_HEADS, M_ALL, LA_DK), BF16),
                   jax.ShapeDtypeStruct((LA_HEADS, M_ALL, LA_VDIM), BF16),
                   jax.ShapeDtypeStruct((M_ALL, LANES), F32)],
        compiler_params=_params(("parallel",)),
        name="la_prep",
    )(h, wt_small, gain_q, gain_kv, w_uq, w_ukv, cos, sin)


ATT_TQ = 256
ATT_ROWS = 1024


def _attn_kernel(q_ref, kl_ref, vl_ref, kc_ref, vc_ref, kca_ref, vca_ref, o_ref, *, n_lat_blocks):
    i = pl.program_id(1)

    @pl.when(i < n_lat_blocks)
    def _():
        n_sub = ATT_ROWS // ATT_TQ

        def scores(sub):
            q = q_ref[sub * ATT_TQ:(sub + 1) * ATT_TQ, :]
            return _dot_nt(q, kl_ref[...]), _dot_nt(q, kc_ref[...])

        def softmax(s_l, s_c):
            mx = jnp.maximum(jnp.max(s_l, axis=1, keepdims=True), jnp.max(s_c, axis=1, keepdims=True))
            p_l = jnp.exp(s_l - mx)
            p_c = jnp.exp(s_c - mx)
            den = jnp.sum(p_l, axis=1, keepdims=True) + jnp.sum(p_c, axis=1, keepdims=True)
            return p_l.astype(BF16), p_c.astype(BF16), den

        def values(sub, p_l, p_c, den):
            o = _dot(p_l, vl_ref[...]) + _dot(p_c, vc_ref[...])
            o_ref[sub * ATT_TQ:(sub + 1) * ATT_TQ, :] = (o / den).astype(o_ref.dtype)

        pending_s = scores(0)
        for sub in range(n_sub):
            nxt = scores(sub + 1) if sub + 1 < n_sub else None
            values(sub, *softmax(*pending_s))
            pending_s = nxt

    @pl.when(i >= n_lat_blocks)
    def _():
        for sub in range(ATT_ROWS // CTX_LEN):
            rows = slice(sub * CTX_LEN, (sub + 1) * CTX_LEN)
            s = _dot_nt(q_ref[rows, :], kca_ref[rows, :])
            mx = jnp.max(s, axis=1, keepdims=True)
            p = jnp.exp(s - mx)
            den = jnp.sum(p, axis=1, keepdims=True)
            o = _dot(p.astype(BF16), vca_ref[rows, :])
            o_ref[rows, :] = (o / den).astype(o_ref.dtype)


def _attention(q, k, v, kv_group, need_ctx, name):
    assert M_CTX == ATT_ROWS and ATT_ROWS % ATT_TQ == 0
    n_heads, _, dk = q.shape
    dv = v.shape[2]
    n_lat_blocks = M_LAT // ATT_ROWS
    per_sample = SEQ // ATT_ROWS
    ctx0 = M_LAT // CTX_LEN
    n_blocks = n_lat_blocks + (1 if need_ctx else 0)

    def sample(i):
        return jnp.minimum(i // per_sample, BATCH - 1)

    def lat_spec(dim):
        return pl.BlockSpec((None, SEQ, dim), lambda h, i: (h // kv_group, sample(i), 0))

    def ctx_spec(dim):
        return pl.BlockSpec((None, CTX_LEN, dim), lambda h, i: (h // kv_group, ctx0 + sample(i), 0))

    def ctx_all_spec(dim):
        return pl.BlockSpec((None, M_CTX, dim), lambda h, i: (h // kv_group, n_lat_blocks, 0))

    return pl.pallas_call(
        functools.partial(_attn_kernel, n_lat_blocks=n_lat_blocks),
        grid=(n_heads, n_blocks),
        in_specs=[pl.BlockSpec((None, ATT_ROWS, dk), lambda h, i: (h, i, 0)),
                  lat_spec(dk), lat_spec(dv), ctx_spec(dk), ctx_spec(dv), ctx_all_spec(dk), ctx_all_spec(dv)],
        out_specs=pl.BlockSpec((ATT_ROWS, dv), lambda h, i: (i, h)),
        out_shape=jax.ShapeDtypeStruct((n_blocks * ATT_ROWS, n_heads * dv), BF16),
        compiler_params=_params(("parallel", "arbitrary")),
        name=name,
    )(q, k, v, k, v, k, v)


def _router_kernel(h_ref, w_ref, o_ref):
    logits = _dot_nt(w_ref[...], h_ref[...])
    mx = jnp.max(logits, axis=0, keepdims=True)
    e = jnp.exp(logits - mx)
    o_ref[...] = e / jnp.sum(e, axis=0, keepdims=True)


def _router(h, w_router_t, m_rows, tm=1024):
    d = h.shape[1]
    return pl.pallas_call(
        _router_kernel,
        grid=(m_rows // tm,),
        in_specs=[pl.BlockSpec((tm, d), lambda i: (i, 0)),
                  pl.BlockSpec((N_EXPERTS, d), lambda i: (0, 0))],
        out_specs=pl.BlockSpec((N_EXPERTS, tm), lambda i: (0, i)),
        out_shape=jax.ShapeDtypeStruct((N_EXPERTS, m_rows), F32),
        compiler_params=_params(("parallel",)),
        name="router",
    )(h, w_router_t)


PREFIX_BLOCK = 256


def _exclusive_prefix(x_bf, upper):
    n_tok = x_bf.shape[1]
    carry = jnp.zeros((x_bf.shape[0], 1), F32)
    parts = []
    for blk in range(n_tok // PREFIX_BLOCK):
        xb = x_bf[:, blk * PREFIX_BLOCK:(blk + 1) * PREFIX_BLOCK]
        parts.append(_dot(xb, upper) + carry)
        carry = carry + jnp.sum(xb.astype(F32), axis=1, keepdims=True)
    return parts[0] if len(parts) == 1 else jnp.concatenate(parts, axis=1)


def _bitonic_sort_descending(tiles):
    n_tiles = len(tiles)
    n = n_tiles * LANES
    lane = lax.broadcasted_iota(I32, tiles[0].shape, 1)
    k = 2
    while k <= n:
        j = k // 2
        while j >= 1:
            new_tiles = []
            for t in range(n_tiles):
                x = tiles[t]
                if j < LANES:
                    low = (lane & j) == 0
                    partner = jnp.where(low, pltpu.roll(x, LANES - j, 1), pltpu.roll(x, j, 1))
                else:
                    partner = tiles[t ^ (j // LANES)]
                big = jnp.maximum(x, partner)
                small = jnp.minimum(x, partner)
                up = ((t * LANES) & k) == 0
                if j >= LANES:
                    take_big = (((t * LANES) & j) == 0) == up
                    new_tiles.append(big if take_big else small)
                elif k >= LANES:
                    new_tiles.append(jnp.where(low, big, small) if up else jnp.where(low, small, big))
                else:
                    agree = (lane & j) * (k // j) == (lane & k)
                    new_tiles.append(jnp.where(agree, big, small))
            tiles = new_tiles
            j //= 2
        k *= 2
    return tiles


def _topk_kernel(aff_ref, rank_ref, *, cap):
    a = aff_ref[...]
    n_tok = a.shape[1]
    ordered = _bitonic_sort_descending([a[:, t * LANES:(t + 1) * LANES] for t in range(n_tok // LANES)])
    pos = cap - 1
    thr = ordered[pos // LANES][:, pos % LANES:pos % LANES + 1]
    gt = a > thr
    eq = a == thr
    need = cap - jnp.sum(jnp.where(gt, 1.0, 0.0), axis=1, keepdims=True)
    ri = lax.broadcasted_iota(I32, (PREFIX_BLOCK, PREFIX_BLOCK), 0)
    ci = lax.broadcasted_iota(I32, (PREFIX_BLOCK, PREFIX_BLOCK), 1)
    upper = jnp.where(ri < ci, 1.0, 0.0).astype(BF16)
    eq_before = _exclusive_prefix(jnp.where(eq, 1.0, 0.0).astype(BF16), upper)
    sel = gt | (eq & (eq_before < need))
    slot = _exclusive_prefix(jnp.where(sel, 1.0, 0.0).astype(BF16), upper)
    rank_ref[...] = jnp.where(sel, slot.astype(I32), -1)


def _topk(aff, n_tok, cap, blk0, n_sets):
    return pl.pallas_call(
        functools.partial(_topk_kernel, cap=cap),
        grid=(n_sets,),
        in_specs=[pl.BlockSpec((N_EXPERTS, n_tok), lambda s: (0, blk0 + s))],
        out_specs=pl.BlockSpec((N_EXPERTS, n_tok), lambda s: (0, s)),
        out_shape=jax.ShapeDtypeStruct((N_EXPERTS, n_sets * n_tok), I32),
        compiler_params=_params(("parallel",)),
        name=f"topk_{n_tok}",
    )(aff)


def _gather_kernel(rank_ref, aff_ref, h_ref, *rest, cap):
    xg_ref, val_ref = rest[-2:]
    for e in range(rank_ref.shape[0]):
        r = rank_ref[e]
        n_tok = r.shape[1]
        slot = lax.broadcasted_iota(I32, (cap, n_tok), 0)
        hit = r == slot
        onehot = jnp.where(hit, 1.0, 0.0).astype(BF16)
        xg_ref[e] = _dot(onehot, h_ref[...]).astype(xg_ref.dtype)
        val_ref[e] = jnp.sum(jnp.where(hit, aff_ref[e], 0.0), axis=1, keepdims=True)


def _gather(rank3, aff3, h, n_tok, cap, blk0, row0, rows_total, prev, experts_per_step):
    d = h.shape[1]
    rb0 = row0 // cap
    eg = experts_per_step
    in_specs = [pl.BlockSpec((eg, 1, n_tok), lambda b, e: (e, 0, b)),
                pl.BlockSpec((eg, 1, n_tok), lambda b, e: (e, 0, blk0 + b)),
                pl.BlockSpec((n_tok, d), lambda b, e: (blk0 + b, 0))]
    args = [rank3, aff3, h]
    aliases = {}
    if prev is not None:
        in_specs += [pl.BlockSpec(memory_space=pl.ANY), pl.BlockSpec(memory_space=pl.ANY)]
        args += list(prev)
        aliases = {3: 0, 4: 1}
    return pl.pallas_call(
        functools.partial(_gather_kernel, cap=cap),
        grid=(BATCH, N_EXPERTS // eg),
        in_specs=in_specs,
        out_specs=[pl.BlockSpec((eg, cap, d), lambda b, e: (e, rb0 + b, 0)),
                   pl.BlockSpec((eg, cap, 1), lambda b, e: (e, rb0 + b, 0))],
        out_shape=[jax.ShapeDtypeStruct((N_EXPERTS, rows_total, d), BF16),
                   jax.ShapeDtypeStruct((N_EXPERTS, rows_total, 1), F32)],
        input_output_aliases=aliases,
        compiler_params=_params(("parallel", "arbitrary")),
        name=f"moe_gather_{n_tok}",
    )(*args)


EXPERT_COLS = 512


def _expert_kernel(x_ref, w1_ref, w3_ref, w2_ref, val_ref, y_ref, acc_ref):
    f = pl.program_id(1)
    last = pl.num_programs(1) - 1
    col_slices = [slice(c0, c0 + EXPERT_COLS) for c0 in range(0, acc_ref.shape[1], EXPERT_COLS)]

    def hidden():
        x = x_ref[...]
        a = _dot(x, w1_ref[...].astype(BF16))
        u = _dot(x, w3_ref[...].astype(BF16))
        return (a * jax.nn.sigmoid(a) * u).astype(BF16)

    def down(hm, cols):
        return _dot(hm, w2_ref[:, cols].astype(BF16))

    @pl.when(f == 0)
    def _():
        hm = hidden()
        for cols in col_slices:
            acc_ref[:, cols] = down(hm, cols)

    @pl.when((f > 0) & (f < last))
    def _():
        hm = hidden()
        for cols in col_slices:
            acc_ref[:, cols] += down(hm, cols)

    @pl.when(f == last)
    def _():
        hm = hidden()
        val = val_ref[...]
        for cols in col_slices:
            y_ref[:, cols] = ((acc_ref[:, cols] + down(hm, cols)) * val).astype(y_ref.dtype)


def _experts(xg, vals, w_e1, w_e3, w_e2, layer, tf=256):
    n_exp, rows, d = xg.shape
    ff = w_e1.shape[3]
    assert ff // tf >= 2
    return pl.pallas_call(
        _expert_kernel,
        grid=(n_exp, ff // tf),
        in_specs=[pl.BlockSpec((None, rows, d), lambda e, f: (e, 0, 0)),
                  pl.BlockSpec((None, None, d, tf), lambda e, f: (layer, e, 0, f)),
                  pl.BlockSpec((None, None, d, tf), lambda e, f: (layer, e, 0, f)),
                  pl.BlockSpec((None, None, tf, d), lambda e, f: (layer, e, f, 0)),
                  pl.BlockSpec((None, rows, 1), lambda e, f: (e, 0, 0))],
        out_specs=pl.BlockSpec((None, rows, d), lambda e, f: (e, 0, 0)),
        out_shape=jax.ShapeDtypeStruct((n_exp, rows, d), BF16),
        scratch_shapes=[pltpu.VMEM((rows, d), F32)],
        compiler_params=_params(("parallel", "arbitrary")),
        name="moe_experts",
    )(xg, w_e1, w_e3, w_e2, vals)


COMBINE_LANES = 1024


def _combine_kernel(rank_ref, y_ref, x_ref, g_ref, scale_ref, shift_ref, *rest, cap, keep_x):
    lhs_ref = rest[-1]
    outs = rest[-3:-1] if keep_x else rest[-2:-1]
    n_slots = N_EXPERTS * cap
    chunk = min(COMBINE_LANES, n_slots)
    shift = cap.bit_length() - 1

    r = rank_ref[...].astype(F32).astype(BF16)
    tm = r.shape[0]
    for c0 in range(0, n_slots, chunk):
        lane = c0 + lax.broadcasted_iota(I32, (LANES, chunk), 1)
        row = lax.broadcasted_iota(I32, (LANES, chunk), 0)
        expand = jnp.where(lax.shift_right_logical(lane, shift) == row, 1.0, 0.0).astype(BF16)
        r_exp = _dot(r, expand)
        slot = (c0 + lax.broadcasted_iota(I32, (tm, chunk), 1)) & (cap - 1)
        lhs_ref[:, c0:c0 + chunk] = jnp.where(r_exp == slot.astype(F32), 1.0, 0.0).astype(BF16)

    y = y_ref[...]
    x_new = x_ref[...] + g_ref[...] * _dot(lhs_ref[...], y.reshape(n_slots, y.shape[2]))
    if keep_x:
        outs[0][...] = x_new
    outs[-1][...] = (_rms(x_new) * scale_ref[...] + shift_ref[...]).astype(outs[-1].dtype)


def _combine(rank_t, y, x, gate_tab, scale_tab, shift_tab, prev, n_tok, cap, blk0, row0, tm, keep_x, norm_dtype):
    assert cap & (cap - 1) == 0
    d = x.shape[1]
    tpb = n_tok // tm
    xrow0 = blk0 * tpb
    rb0 = row0 // cap

    def grp(b):
        return b if n_tok == SEQ else BATCH

    def tab_spec():
        return pl.BlockSpec((None, 1, d), lambda b, i: (grp(b), 0, 0))

    def row_spec():
        return pl.BlockSpec((tm, d), lambda b, i: (xrow0 + b * tpb + i, 0))

    in_specs = [pl.BlockSpec((tm, LANES), lambda b, i: (b * tpb + i, 0)),
                pl.BlockSpec((N_EXPERTS, cap, d), lambda b, i: (0, rb0 + b, 0), pipeline_mode=pl.Buffered(1)),
                row_spec(), tab_spec(), tab_spec(), tab_spec()]
    args = [rank_t, y, x, gate_tab, scale_tab, shift_tab]
    out_shape = [jax.ShapeDtypeStruct(x.shape, norm_dtype)]
    if keep_x:
        out_shape = [jax.ShapeDtypeStruct(x.shape, F32)] + out_shape
    aliases = {}
    if prev is not None:
        in_specs += [pl.BlockSpec(memory_space=pl.ANY)] * len(prev)
        aliases = {len(args) + k: k for k in range(len(prev))}
        args += list(prev)
    return pl.pallas_call(
        functools.partial(_combine_kernel, cap=cap, keep_x=keep_x),
        grid=(BATCH, tpb),
        in_specs=in_specs,
        out_specs=[row_spec() for _ in out_shape],
        out_shape=out_shape,
        scratch_shapes=[pltpu.VMEM((tm, N_EXPERTS * cap), BF16)],
        input_output_aliases=aliases,
        compiler_params=_params(("parallel", "arbitrary")),
        name=f"moe_combine_{n_tok}",
    )(*args)


def _expert_choice(h2, x, gate_tab, next_scale, next_shift, w_router_t, w_e1, w_e3, w_e2, layer, need_ctx):
    m_rows = h2.shape[0]
    aff = _router(h2, w_router_t, m_rows)
    aff3 = aff.reshape(N_EXPERTS, 1, m_rows)
    sets = [(SEQ, 0)]
    if need_ctx:
        sets.append((CTX_LEN, M_LAT // CTX_LEN))
    caps = [CAPACITY_FACTOR * n_tok // N_EXPERTS for n_tok, _ in sets]
    rows_total = sum(BATCH * cap for cap in caps)
    ranks, gathered, row0 = [], None, 0
    for (n_tok, blk0), cap in zip(sets, caps):
        rank = _topk(aff, n_tok, cap, blk0, BATCH)
        gathered = _gather(rank.reshape(N_EXPERTS, 1, BATCH * n_tok), aff3, h2, n_tok, cap, blk0, row0, rows_total,
                           gathered, experts_per_step=1 if n_tok == SEQ else N_EXPERTS)
        ranks.append((rank, row0))
        row0 += BATCH * cap
    y = _experts(gathered[0], gathered[1], w_e1, w_e3, w_e2, layer)
    out = None
    for (n_tok, blk0), cap, (rank, row0) in zip(sets, caps, ranks):
        rank_t = jnp.pad(rank.T, ((0, 0), (0, LANES - N_EXPERTS)))
        out = _combine(rank_t, y, x, gate_tab, next_scale, next_shift, out, n_tok, cap, blk0, row0,
                       tm=min(512, n_tok), keep_x=need_ctx, norm_dtype=BF16 if need_ctx else F32)
    return out


def _rope_tables(dim):
    rows = SEQ // GRID_W
    quarter = dim // 4
    inv = ROPE_BASE ** (-jnp.arange(quarter, dtype=F32) / quarter)
    row = jnp.broadcast_to(jnp.arange(rows, dtype=F32)[:, None], (rows, GRID_W)).reshape(-1)
    col = jnp.broadcast_to(jnp.arange(GRID_W, dtype=F32)[None, :], (rows, GRID_W)).reshape(-1)
    ang_r = row[:, None] * inv[None, :]
    ang_c = col[:, None] * inv[None, :]
    cos = jnp.concatenate([jnp.cos(ang_r), jnp.cos(ang_r), jnp.cos(ang_c), jnp.cos(ang_c)], axis=-1)
    sin = jnp.concatenate([-jnp.sin(ang_r), jnp.sin(ang_r), -jnp.sin(ang_c), jnp.sin(ang_c)], axis=-1)
    cos = jnp.tile(cos, (BATCH, LANES // dim))
    sin = jnp.tile(sin, (BATCH, LANES // dim))
    cos = jnp.concatenate([cos, jnp.ones((M_CTX, LANES), F32)], axis=0)
    sin = jnp.concatenate([sin, jnp.zeros((M_CTX, LANES), F32)], axis=0)
    return cos, sin


def _halves_major(a, axis=-1):
    axis = axis % a.ndim
    quarter = a.shape[axis] // 4
    a4 = a.reshape(*a.shape[:axis], 2, 2, quarter, *a.shape[axis + 1:])
    return jnp.swapaxes(a4, axis, axis + 1).reshape(a.shape)


def _mod_tables(mods):
    d = D_MODEL
    return [mods[:N_GROUPS, k * d:(k + 1) * d].reshape(N_GROUPS, 1, d) for k in range(6)]


def _layer(x_lat, x_ctx, h_pre, mods, next_scale, next_shift, layer, wt_all, ml_gate_bias, ml_norm, la_q_norm,
           la_kv_norm, la_w_uq, la_w_ukv, ga_q_norm, ga_k_norm, w_br_ml, w_br_la, w_br_ga, w_out, w_router,
           w_e1, w_e3, w_e2, ropes, need_ctx):
    d = D_MODEL
    m_rows = M_ALL if need_ctx else M_LAT
    sh1, sc1, g1, sh2, sc2, g2 = _mod_tables(mods)
    (cos_la, sin_la), (cos_ga, sin_ga) = ropes

    n_qk = GA_HEADS + GA_KV_HEADS
    wt_rows = wt_all.reshape(-1, d)
    layer_row0 = layer * wt_all.shape[1]
    wt_sm, wt_gaf = lax.optimization_barrier((wt_rows[layer_row0 + OFF_MLG:layer_row0 + OFF_GAQ],
                                              wt_rows[layer_row0 + OFF_GAQ:layer_row0 + OFF_GATE]))
    wt_small = jnp.concatenate([wt_sm[OFF_CQ - OFF_MLG:], wt_sm[:OFF_CQ - OFF_MLG],
                                jnp.zeros((SMALL_WIDTH - (OFF_GAQ - OFF_MLG), d), F32)], axis=0).astype(BF16)
    wt_gaqk = _halves_major(wt_gaf[:OFF_GAV - OFF_GAQ].reshape(n_qk, GA_HEAD_DIM, d), axis=1).reshape(-1, d)
    wt_ga = jnp.concatenate([wt_gaqk, wt_gaf[OFF_GAV - OFF_GAQ:]], axis=0).astype(BF16)
    w_uq = la_w_uq.reshape(LA_Q_RANK, LA_HEADS, LA_NOPE + LA_ROPE)
    w_uq = jnp.concatenate([w_uq[:, :, :LA_NOPE].reshape(LA_Q_RANK, LA_HEADS * LA_NOPE),
                            w_uq[:, :, LA_NOPE:].reshape(LA_Q_RANK, LA_HEADS * LA_ROPE)], axis=1).astype(BF16)

    h = _rms_mod(x_lat, x_ctx, 1.0 + sc1, sh1, M_ALL) if h_pre is None else h_pre
    p_ml = _mm(h, wt_rows, layer_row0, OFF_MLG, name="proj_ml")
    q_la, k_la, v_la, tail = _la_prep(h, wt_small, la_q_norm.reshape(1, LA_Q_RANK),
                                      la_kv_norm.reshape(1, LA_KV_RANK), w_uq, la_w_ukv.astype(BF16), cos_la, sin_la)
    q_ga, k_ga, v_ga = _ga_prep(h, wt_ga, _halves_major(ga_q_norm.reshape(1, GA_HEAD_DIM)),
                                _halves_major(ga_k_norm.reshape(1, GA_HEAD_DIM)), cos_ga, sin_ga)

    gates = (tail[:, TAIL_MLG:TAIL_MLG + ML_GATES] + ml_gate_bias[None, :]).reshape(M_ALL, 2, 2, ML_HEADS)
    gates = jnp.stack([gates[:, 0, 0], jax.nn.log_sigmoid(gates[:, 0, 1]),
                       gates[:, 1, 0], jax.nn.log_sigmoid(gates[:, 1, 1])], axis=-1)
    n_lat = SEQ // ML_CHUNK
    g_lat = gates[:M_LAT].reshape(BATCH, n_lat, ML_CHUNK, ML_HEADS, 4)
    g_ctx = gates[M_LAT:].reshape(BATCH, CTX_LEN, ML_HEADS, 4)
    y_ml_lat, y_ml_ctx = _mlstm(p_ml, jnp.transpose(g_lat, (0, 3, 1, 4, 2)), jnp.transpose(g_lat, (0, 3, 1, 2, 4)),
                                jnp.transpose(g_ctx, (0, 2, 3, 1)), jnp.transpose(g_ctx, (0, 2, 1, 3)),
                                ml_norm.reshape(1, ML_WIDTH))

    y_la = _attention(q_la, k_la, v_la, 1, need_ctx, "attn_mla")
    y_ga = _attention(q_ga, k_ga, v_ga, GA_HEADS // GA_KV_HEADS, need_ctx, "attn_gqa")

    merged = _merge(h, y_ml_lat, y_ml_ctx, y_la, y_ga, wt_rows, layer, layer_row0 + OFF_GATE, w_br_ml, w_br_la,
                    w_br_ga, m_rows)
    x, h2 = _out_proj(merged, w_out.astype(BF16), x_lat, x_ctx if need_ctx else None, g1, 1.0 + sc2, sh2, m_rows)
    return _expert_choice(h2, x, g2, next_scale, next_shift, w_router.T.astype(BF16), w_e1, w_e3, w_e2, layer,
                          need_ctx)


def kernel(x, c, ctx, c_ctx, w_mod, b_mod, w_in, ml_gate_bias, ml_norm, la_q_norm, la_kv_norm, la_w_uq, la_w_ukv,
           ga_q_norm, ga_k_norm, w_br_ml, w_br_la, w_br_ga, w_out, w_router, w_e1, w_e3, w_e2, final_norm):
    d = D_MODEL
    cc = jnp.concatenate([c, c_ctx[None, :], jnp.zeros((MODS_ROWS - N_GROUPS, d), F32)], axis=0)
    mods = _mods(cc, w_mod, b_mod.reshape(DEPTH, 1, 6 * d))
    ropes = (_rope_tables(LA_ROPE), tuple(_halves_major(t) for t in _rope_tables(GA_HEAD_DIM)))
    xs, xs_ctx, h_pre = x.reshape(M_LAT, d), ctx.reshape(M_CTX, d), None
    wt_in = jnp.swapaxes(w_in, 1, 2)
    for l in range(DEPTH):
        need_ctx = l < DEPTH - 1
        if need_ctx:
            sh1_next, sc1_next = _mod_tables(mods[l + 1])[:2]
            next_scale, next_shift = 1.0 + sc1_next, sh1_next
        else:
            next_scale = jnp.broadcast_to(final_norm.reshape(1, 1, d), (N_GROUPS, 1, d))
            next_shift = jnp.zeros((N_GROUPS, 1, d), F32)
        out = _layer(xs, xs_ctx, h_pre, mods[l], next_scale, next_shift, l, wt_in, ml_gate_bias[l], ml_norm[l],
                     la_q_norm[l], la_kv_norm[l], la_w_uq[l], la_w_ukv[l], ga_q_norm[l], ga_k_norm[l],
                     w_br_ml, w_br_la, w_br_ga, w_out[l], w_router[l], w_e1, w_e3, w_e2, ropes, need_ctx)
        if need_ctx:
            xs, h_pre = out
            xs_ctx = None
    return out[0].reshape(BATCH, SEQ, d)
```

```python
import functools

import jax
import jax.numpy as jnp
from jax import lax
from jax.experimental import pallas as pl
from jax.experimental.pallas import tpu as pltpu

F32 = jnp.float32
BF16 = jnp.bfloat16
I32 = jnp.int32

D_MODEL = 2048
BATCH = 4
SEQ = 2048
DEPTH = 2
GRID_W = 64
CTX_LEN = 256
EPS = 1e-6
ROPE_BASE = 10000.0

ML_HEADS = 4
ML_HEAD_DIM = 256
ML_WIDTH = ML_HEADS * ML_HEAD_DIM
ML_GATES = 2 * 2 * ML_HEADS
ML_CHUNK = 256

LA_HEADS = 8
LA_NOPE = 128
LA_ROPE = 64
LA_VDIM = 128
LA_Q_RANK = 512
LA_KV_RANK = 256
LA_DK = 256

GA_HEADS = 8
GA_KV_HEADS = 2
GA_HEAD_DIM = 128
GA_WIDTH = GA_HEADS * GA_HEAD_DIM
GA_KV_WIDTH = GA_KV_HEADS * GA_HEAD_DIM

N_EXPERTS = 16
CAPACITY_FACTOR = 2

M_LAT = BATCH * SEQ
M_CTX = BATCH * CTX_LEN
M_ALL = M_LAT + M_CTX
MODS_ROWS = 16
N_GROUPS = BATCH + 1

LANES = 128
SUBLANES = 8

OFF_MLG = 4 * ML_WIDTH
OFF_CQ = OFF_MLG + ML_GATES
OFF_CKV = OFF_CQ + LA_Q_RANK
OFF_KR = OFF_CKV + LA_KV_RANK
OFF_GAQ = OFF_KR + LA_ROPE
OFF_GAK = OFF_GAQ + GA_WIDTH
OFF_GAV = OFF_GAK + GA_KV_WIDTH
OFF_GATE = OFF_GAV + GA_KV_WIDTH
SM_CKV = LA_Q_RANK
SM_TAIL = SM_CKV + LA_KV_RANK
SMALL_WIDTH = SM_TAIL + LANES
TAIL_MLG = LA_ROPE

VMEM_LIMIT = 56 * 1024 * 1024


def _params(semantics, vmem=VMEM_LIMIT):
    return pltpu.CompilerParams(dimension_semantics=semantics, vmem_limit_bytes=vmem)


def _dot(a, b):
    return jnp.dot(a, b, preferred_element_type=F32)


def _dot_nt(a, b):
    return lax.dot_general(a, b, (((1,), (1,)), ((), ())), preferred_element_type=F32)


def _rms(x):
    return x * lax.rsqrt(jnp.mean(x * x, axis=-1, keepdims=True) + EPS)


def _group_of_block(tm):
    return lambda i: (i * tm) // SEQ


def _mods_kernel(c_ref, w_ref, b_ref, o_ref):
    c = c_ref[...]
    a = (c * jax.nn.sigmoid(c)).astype(BF16)
    part = _dot(a, w_ref[...].astype(BF16))

    @pl.when(pl.program_id(1) == 0)
    def _():
        o_ref[...] = part + b_ref[...]

    @pl.when(pl.program_id(1) > 0)
    def _():
        o_ref[...] += part


def _mods(cc, w_mod, b_mod, tk=256):
    depth, d, n = w_mod.shape
    rows = cc.shape[0]
    return pl.pallas_call(
        _mods_kernel,
        grid=(depth, d // tk),
        in_specs=[pl.BlockSpec((rows, tk), lambda l, k: (0, k)),
                  pl.BlockSpec((None, tk, n), lambda l, k: (l, k, 0)),
                  pl.BlockSpec((None, 1, n), lambda l, k: (l, 0, 0))],
        out_specs=pl.BlockSpec((None, rows, n), lambda l, k: (l, 0, 0)),
        out_shape=jax.ShapeDtypeStruct((depth, rows, n), F32),
        compiler_params=_params(("parallel", "arbitrary")),
        name="mods",
    )(cc, w_mod, b_mod)


def _row_sources(x_lat, x_ctx, tm, n_grid_axes=1):
    d = x_lat.shape[1]
    n_lat_tiles = M_LAT // tm

    def spec(fn):
        return pl.BlockSpec((tm, d), (lambda i: (fn(i), 0)) if n_grid_axes == 1 else (lambda j, i: (fn(i), 0)))

    if x_ctx is None:
        return [spec(lambda i: i)], [x_lat], lambda i, refs: refs[0][...]
    specs = [spec(lambda i: jnp.minimum(i, n_lat_tiles - 1)), spec(lambda i: jnp.maximum(i - n_lat_tiles, 0))]
    return specs, [x_lat, x_ctx], lambda i, refs: jnp.where(i < n_lat_tiles, refs[0][...], refs[1][...])


def _rms_kernel(*refs, pick):
    scale_ref, shift_ref, o_ref = refs[-3:]
    x = pick(pl.program_id(0), refs[:-3])
    o_ref[...] = (_rms(x) * scale_ref[...] + shift_ref[...]).astype(o_ref.dtype)


def _rms_mod(x_lat, x_ctx, scale_tab, shift_tab, m_rows, tm=512):
    d = x_lat.shape[1]
    grp = _group_of_block(tm)
    x_specs, x_args, pick = _row_sources(x_lat, x_ctx, tm)
    return pl.pallas_call(
        functools.partial(_rms_kernel, pick=pick),
        grid=(m_rows // tm,),
        in_specs=x_specs + [pl.BlockSpec((None, 1, d), lambda i: (grp(i), 0, 0)),
                            pl.BlockSpec((None, 1, d), lambda i: (grp(i), 0, 0))],
        out_specs=pl.BlockSpec((tm, d), lambda i: (i, 0)),
        out_shape=jax.ShapeDtypeStruct((m_rows, d), BF16),
        compiler_params=_params(("parallel",)),
        name="rms_mod",
    )(*x_args, scale_tab, shift_tab)


def _mm_kernel(a_ref, wt_ref, o_ref, wbf_ref):
    @pl.when(pl.program_id(1) == 0)
    def _():
        wbf_ref[...] = wt_ref[...].astype(BF16)

    o_ref[...] = _dot_nt(a_ref[...], wbf_ref[...]).astype(o_ref.dtype)


def _mm(a, wt_all, row0, n_cols, *, tm=1024, tn=1024, out_dtype=F32, name="mm"):
    m_rows, k = a.shape
    assert m_rows % tm == 0 and n_cols % tn == 0
    return pl.pallas_call(
        _mm_kernel,
        grid=(n_cols // tn, m_rows // tm),
        in_specs=[pl.BlockSpec((tm, k), lambda j, i: (i, 0)),
                  pl.BlockSpec((pl.Element(tn), pl.Element(k)),
                               lambda j, i: (pl.multiple_of(row0 + j * tn, SUBLANES), 0))],
        out_specs=pl.BlockSpec((tm, tn), lambda j, i: (i, j)),
        out_shape=jax.ShapeDtypeStruct((m_rows, n_cols), out_dtype),
        scratch_shapes=[pltpu.VMEM((tn, k), BF16)],
        compiler_params=_params(("parallel", "arbitrary")),
        name=name,
    )(a, wt_all)


def _out_proj_kernel(a_ref, w_ref, wr_ref, *refs, pick):
    g_ref, scale_ref, shift_ref, x_out_ref, h_out_ref, aff_ref = refs[-6:]
    x = pick(pl.program_id(0), refs[:-6])
    x_new = x + g_ref[...] * _dot(a_ref[...], w_ref[...])
    x_out_ref[...] = x_new
    h2 = (_rms(x_new) * scale_ref[...] + shift_ref[...]).astype(h_out_ref.dtype)
    h_out_ref[...] = h2
    logits = _dot_nt(wr_ref[...], h2)
    e = jnp.exp(logits - jnp.max(logits, axis=0, keepdims=True))
    aff_ref[...] = e / jnp.sum(e, axis=0, keepdims=True)


def _out_proj(a, w, w_router_t, x_lat, x_ctx, gate_tab, scale_tab, shift_tab, m_rows, tm=512):
    k = a.shape[1]
    d = w.shape[1]
    grp = _group_of_block(tm)
    x_specs, x_args, pick = _row_sources(x_lat, x_ctx, tm)
    tab_spec = pl.BlockSpec((None, 1, d), lambda i: (grp(i), 0, 0))
    return pl.pallas_call(
        functools.partial(_out_proj_kernel, pick=pick),
        grid=(m_rows // tm,),
        in_specs=[pl.BlockSpec((tm, k), lambda i: (i, 0)),
                  pl.BlockSpec((k, d), lambda i: (0, 0)),
                  pl.BlockSpec((N_EXPERTS, d), lambda i: (0, 0))] + x_specs + [tab_spec, tab_spec, tab_spec],
        out_specs=[pl.BlockSpec((tm, d), lambda i: (i, 0)), pl.BlockSpec((tm, d), lambda i: (i, 0)),
                   pl.BlockSpec((N_EXPERTS, tm), lambda i: (0, i))],
        out_shape=[jax.ShapeDtypeStruct((m_rows, d), F32), jax.ShapeDtypeStruct((m_rows, d), BF16),
                   jax.ShapeDtypeStruct((N_EXPERTS, m_rows), F32)],
        compiler_params=_params(("parallel",)),
        name="out_proj",
    )(a, w, w_router_t, *x_args, gate_tab, scale_tab, shift_tab)


def _merge_kernel(h_ref, y0l_ref, y0c_ref, y1_ref, y2_ref, wg0_ref, wg1_ref, wg2_ref, wb0_ref, wb1_ref, wb2_ref,
                  o_ref, wg_bf_ref, wb_bf_ref, *, n_lat_tiles):
    @pl.when(pl.program_id(1) == 0)
    def _():
        for b, (wg_ref, wb_ref) in enumerate(((wg0_ref, wb0_ref), (wg1_ref, wb1_ref), (wg2_ref, wb2_ref))):
            wg_bf_ref[b] = wg_ref[...].astype(BF16)
            wb_bf_ref[b] = wb_ref[...].astype(BF16)

    h = h_ref[...]
    y0 = jnp.where(pl.program_id(1) < n_lat_tiles, y0l_ref[...], y0c_ref[...])
    acc = jax.nn.sigmoid(_dot_nt(h, wg_bf_ref[0])) * _dot(y0, wb_bf_ref[0])
    acc = acc + jax.nn.sigmoid(_dot_nt(h, wg_bf_ref[1])) * _dot(y1_ref[...], wb_bf_ref[1])
    acc = acc + jax.nn.sigmoid(_dot_nt(h, wg_bf_ref[2])) * _dot(y2_ref[...], wb_bf_ref[2])
    o_ref[...] = acc.astype(o_ref.dtype)


def _merge(h, y_ml_lat, y_ml_ctx, y_la, y_ga, wt_all, layer, gate_row0, w_br_ml, w_br_la, w_br_ga, m_rows,
           tm=512, tn=512):
    d = D_MODEL
    nb = d // tn
    n_lat_tiles = M_LAT // tm
    y_spec = pl.BlockSpec((tm, ML_WIDTH), lambda j, i: (i, 0))
    wb_spec = pl.BlockSpec((None, ML_WIDTH, tn), lambda j, i: (layer, 0, j), pipeline_mode=pl.Buffered(1))

    def wg_spec(branch):
        return pl.BlockSpec((pl.Element(tn), pl.Element(d)),
                            lambda j, i: (pl.multiple_of(gate_row0 + branch * d + j * tn, SUBLANES), 0),
                            pipeline_mode=pl.Buffered(1))

    return pl.pallas_call(
        functools.partial(_merge_kernel, n_lat_tiles=n_lat_tiles),
        grid=(nb, m_rows // tm),
        in_specs=[pl.BlockSpec((tm, d), lambda j, i: (i, 0)),
                  pl.BlockSpec((tm, ML_WIDTH), lambda j, i: (jnp.minimum(i, n_lat_tiles - 1), 0)),
                  pl.BlockSpec((tm, ML_WIDTH), lambda j, i: (jnp.maximum(i - n_lat_tiles, 0), 0)),
                  y_spec, y_spec, wg_spec(0), wg_spec(1), wg_spec(2), wb_spec, wb_spec, wb_spec],
        out_specs=pl.BlockSpec((tm, tn), lambda j, i: (i, j)),
        out_shape=jax.ShapeDtypeStruct((m_rows, d), BF16),
        scratch_shapes=[pltpu.VMEM((3, tn, d), BF16), pltpu.VMEM((3, ML_WIDTH, tn), BF16)],
        compiler_params=_params(("parallel", "arbitrary")),
        name="merge",
    )(h, y_ml_lat, y_ml_ctx, y_la, y_ga, wt_all, wt_all, wt_all, w_br_ml, w_br_la, w_br_ga)


def _mlstm_kernel(ql_ref, kl_ref, vl_ref, ol_ref, qc_ref, kc_ref, vc_ref, oc_ref,
                  grl_ref, gcl_ref, grc_ref, gcc_ref, gain_ref, yl_ref, yc_ref,
                  qb_ref, kt_ref, vb_ref, s_ref, hsum_ref):
    length = ML_CHUNK
    hd = ML_HEAD_DIM
    n_lat = SEQ // length
    n_chunks = n_lat + 1
    scale = hd ** -0.5

    def load(c, ctx_ref, lat_ref):
        return ctx_ref[...] if c == 0 else lat_ref[(c - 1) * length:c * length, :]

    for c in range(n_chunks):
        qb = (load(c, qc_ref, ql_ref) * scale).astype(BF16)
        kt = jnp.transpose(load(c, kc_ref, kl_ref)).astype(BF16)
        qb_ref[c] = qb
        kt_ref[c] = kt
        vb_ref[c] = load(c, vc_ref, vl_ref).astype(BF16)
        s_ref[c] = _dot(qb, kt)

    ri = lax.broadcasted_iota(I32, (length, length), 0)
    ci = lax.broadcasted_iota(I32, (length, length), 1)
    written = set()

    def chunk_step(direction, c, state, last):
        m, ct, n_vec = state
        reverse = direction == 1
        row_i = 2 * direction
        allowed = (ci >= ri) if reverse else (ci <= ri)
        allowed_t = (ri >= ci) if reverse else (ri <= ci)
        g_rows = grc_ref[...] if c == 0 else grl_ref[c - 1]
        g_cols = gcc_ref[...] if c == 0 else gcl_ref[c - 1]
        li_r = g_rows[row_i:row_i + 1, :]
        lf_r = g_rows[row_i + 1:row_i + 2, :]
        li_c = g_cols[:, row_i:row_i + 1]
        lf_c = g_cols[:, row_i + 1:row_i + 2]
        cum_c = jnp.sum(jnp.where(allowed, lf_r, 0.0), axis=1, keepdims=True)
        cum_r = jnp.sum(jnp.where(allowed_t, lf_c, 0.0), axis=0, keepdims=True)
        total = jnp.sum(lf_r, axis=1, keepdims=True)
        dmat = jnp.where(allowed, cum_c - cum_r + li_r, -jnp.inf)
        m_loc = jnp.max(dmat, axis=1, keepdims=True)
        s_loc = s_ref[c] * jnp.exp(dmat - m_loc)
        intra = _dot(s_loc.astype(BF16), vb_ref[c])
        intra_sum = jnp.sum(s_loc, axis=1, keepdims=True)
        inter = cum_c + m
        m_row = jnp.maximum(inter, m_loc)
        shrink = jnp.exp(-jnp.abs(inter - m_loc))
        f_loc = jnp.where(m_loc >= inter, 1.0, shrink)
        w_inter = jnp.where(inter >= m_loc, 1.0, shrink)
        q32 = load(c, qc_ref, ql_ref) * scale
        num = intra * f_loc + w_inter * _dot(qb_ref[c], ct.astype(BF16))
        den = intra_sum * f_loc + w_inter * jnp.sum(q32 * n_vec, axis=1, keepdims=True)
        h_out = num / jnp.maximum(jnp.abs(den), jnp.exp(-m_row))
        rows = slice(c * length, (c + 1) * length)
        if c in written:
            hsum_ref[rows, :] += h_out
        else:
            hsum_ref[rows, :] = h_out
            written.add(c)
        if last:
            return state
        g_r = total - cum_r + li_r
        g_c = total - cum_c + li_c
        m_new = jnp.maximum(total + m, jnp.max(g_r, axis=1, keepdims=True))
        wk_c = jnp.exp(g_c - m_new)
        decay = jnp.exp(total + m - m_new)
        wv = (wk_c * load(c, vc_ref, vl_ref)).astype(BF16)
        ct_new = decay * ct + _dot(kt_ref[c], wv)
        n_new = decay * n_vec + jnp.sum(wk_c * load(c, kc_ref, kl_ref), axis=0, keepdims=True)
        return m_new, ct_new, n_new

    orders = ([0] + list(range(1, n_chunks)), [0] + [n_lat - j for j in range(n_lat)])
    zero_state = (jnp.zeros((1, 1), F32), jnp.zeros((hd, hd), F32), jnp.zeros((1, hd), F32))
    states = [zero_state, zero_state]
    for pos in range(n_chunks):
        for direction in range(2):
            states[direction] = chunk_step(direction, orders[direction][pos], states[direction],
                                           last=pos == n_chunks - 1)

    gain = gain_ref[...]

    def finish(rows, o_pre):
        return (_rms(hsum_ref[rows, :]) * gain * jax.nn.sigmoid(o_pre)).astype(BF16)

    yc_ref[...] = finish(slice(0, CTX_LEN), oc_ref[...])
    yl_ref[...] = finish(slice(CTX_LEN, CTX_LEN + SEQ), ol_ref[...])


def _mlstm(p_ml, g_row_lat, g_col_lat, g_row_ctx, g_col_ctx, gain):
    assert CTX_LEN == ML_CHUNK
    hd = ML_HEAD_DIM
    length = ML_CHUNK
    n_lat = SEQ // length
    ctx0 = M_LAT // CTX_LEN

    def lat_spec(col):
        return pl.BlockSpec((SEQ, hd), lambda b, h: (b, col * ML_HEADS + h))

    def ctx_spec(col):
        return pl.BlockSpec((CTX_LEN, hd), lambda b, h: (ctx0 + b, col * ML_HEADS + h))

    return pl.pallas_call(
        _mlstm_kernel,
        grid=(BATCH, ML_HEADS),
        in_specs=[lat_spec(0), lat_spec(1), lat_spec(2), lat_spec(3),
                  ctx_spec(0), ctx_spec(1), ctx_spec(2), ctx_spec(3),
                  pl.BlockSpec((None, None, n_lat, 4, length), lambda b, h: (b, h, 0, 0, 0)),
                  pl.BlockSpec((None, None, n_lat, length, 4), lambda b, h: (b, h, 0, 0, 0)),
                  pl.BlockSpec((None, None, 4, length), lambda b, h: (b, h, 0, 0)),
                  pl.BlockSpec((None, None, length, 4), lambda b, h: (b, h, 0, 0)),
                  pl.BlockSpec((1, hd), lambda b, h: (0, h))],
        out_specs=[pl.BlockSpec((SEQ, hd), lambda b, h: (b, h)),
                   pl.BlockSpec((CTX_LEN, hd), lambda b, h: (b, h))],
        out_shape=[jax.ShapeDtypeStruct((M_LAT, ML_WIDTH), BF16),
                   jax.ShapeDtypeStruct((M_CTX, ML_WIDTH), BF16)],
        scratch_shapes=[pltpu.VMEM((1 + n_lat, length, hd), BF16),
                        pltpu.VMEM((1 + n_lat, hd, length), BF16),
                        pltpu.VMEM((1 + n_lat, length, hd), BF16),
                        pltpu.VMEM((1 + n_lat, length, length), F32),
                        pltpu.VMEM((CTX_LEN + SEQ, hd), F32)],
        compiler_params=_params(("parallel", "parallel")),
        name="mlstm",
    )(p_ml, p_ml, p_ml, p_ml, p_ml, p_ml, p_ml, p_ml, g_row_lat, g_col_lat, g_row_ctx, g_col_ctx, gain)


def _rope_partner(x, half):
    lane = lax.broadcasted_iota(I32, x.shape, 1)
    first = (lane & (2 * half - 1)) < half
    return jnp.where(first, pltpu.roll(x, LANES - half, 1), pltpu.roll(x, half, 1))


GA_PIECE = 512


def _ga_prep_kernel(h_ref, wa_ref, wb_ref, wc_ref, gq_ref, gk_ref, cos_ref, sin_ref, q_ref, k_ref, v_ref):
    h = h_ref[...]
    cos = cos_ref[...]
    sin = sin_ref[...]
    dh = GA_HEAD_DIM
    per_piece = GA_PIECE // dh

    def norm_rope(x, gain):
        y = _rms(x) * gain
        return y * cos + pltpu.roll(y, dh // 2, 1) * sin

    for piece, w_ref in enumerate((wa_ref, wb_ref)):
        p = _dot_nt(h, w_ref[...])
        for j in range(per_piece):
            q = norm_rope(p[:, j * dh:(j + 1) * dh], gq_ref[...])
            q_ref[piece * per_piece + j] = (q * dh ** -0.5).astype(BF16)
    p = _dot_nt(h, wc_ref[...])
    for g in range(GA_KV_HEADS):
        k_ref[g] = norm_rope(p[:, g * dh:(g + 1) * dh], gk_ref[...]).astype(BF16)
        v_ref[g] = p[:, GA_KV_WIDTH + g * dh:GA_KV_WIDTH + (g + 1) * dh].astype(BF16)


def _ga_prep(h, wt_ga, gain_q, gain_k, cos, sin, tm=512):
    assert GA_WIDTH == 2 * GA_PIECE and 2 * GA_KV_WIDTH == GA_PIECE
    d = h.shape[1]
    dh = GA_HEAD_DIM

    def head_spec(n_heads):
        return pl.BlockSpec((n_heads, tm, dh), lambda i: (0, i, 0))

    def piece_spec(j):
        return pl.BlockSpec((GA_PIECE, d), lambda i: (j, 0))

    return pl.pallas_call(
        _ga_prep_kernel,
        grid=(M_ALL // tm,),
        in_specs=[pl.BlockSpec((tm, d), lambda i: (i, 0)), piece_spec(0), piece_spec(1), piece_spec(2),
                  pl.BlockSpec((1, dh), lambda i: (0, 0)),
                  pl.BlockSpec((1, dh), lambda i: (0, 0)),
                  pl.BlockSpec((tm, dh), lambda i: (i, 0)),
                  pl.BlockSpec((tm, dh), lambda i: (i, 0))],
        out_specs=[head_spec(GA_HEADS), head_spec(GA_KV_HEADS), head_spec(GA_KV_HEADS)],
        out_shape=[jax.ShapeDtypeStruct((GA_HEADS, M_ALL, dh), BF16),
                   jax.ShapeDtypeStruct((GA_KV_HEADS, M_ALL, dh), BF16),
                   jax.ShapeDtypeStruct((GA_KV_HEADS, M_ALL, dh), BF16)],
        compiler_params=_params(("parallel",)),
        name="ga_prep",
    )(h, wt_ga, wt_ga, wt_ga, gain_q, gain_k, cos, sin)


def _la_prep_kernel(h_ref, wsm_ref, gq_ref, gkv_ref, wuq_ref, wukv_ref, cos_ref, sin_ref,
                    q_ref, k_ref, v_ref, tail_ref):
    p = _dot_nt(h_ref[...], wsm_ref[...])
    tail = p[:, SM_TAIL:SM_TAIL + LANES]
    tail_ref[...] = tail
    cq = (_rms(p[:, :LA_Q_RANK]) * gq_ref[...]).astype(BF16)
    ckv = (_rms(p[:, SM_CKV:SM_CKV + LA_KV_RANK]) * gkv_ref[...]).astype(BF16)
    qf = _dot(cq, wuq_ref[...])
    kvf = _dot(ckv, wukv_ref[...])
    cos = cos_ref[...]
    sin = sin_ref[...]
    scale = (LA_NOPE + LA_ROPE) ** -0.5
    low = lax.broadcasted_iota(I32, tail.shape, 1) < LA_ROPE

    def rope(x):
        return x * cos + _rope_partner(x, LA_ROPE // 4) * sin

    kr = rope(tail)
    kr2 = jnp.where(low, kr, pltpu.roll(kr, LA_ROPE, 1)).astype(BF16)
    heads_per_tile = LANES // LA_ROPE
    for j in range(LA_HEADS // heads_per_tile):
        c0 = LA_HEADS * LA_NOPE + j * LANES
        qr = rope(qf[:, c0:c0 + LANES]) * scale
        q_ref[2 * j, :, LA_NOPE:] = jnp.where(low, qr, 0.0).astype(BF16)
        q_ref[2 * j + 1, :, LA_NOPE:] = jnp.where(low, 0.0, qr).astype(BF16)
    for h in range(LA_HEADS):
        q_ref[h, :, :LA_NOPE] = (qf[:, h * LA_NOPE:(h + 1) * LA_NOPE] * scale).astype(BF16)
        kv0 = h * (LA_NOPE + LA_VDIM)
        k_ref[h, :, :LA_NOPE] = kvf[:, kv0:kv0 + LA_NOPE].astype(BF16)
        k_ref[h, :, LA_NOPE:] = kr2
        v_ref[h] = kvf[:, kv0 + LA_NOPE:kv0 + LA_NOPE + LA_VDIM].astype(BF16)


def _la_prep(h, wt_small, gain_q, gain_kv, w_uq, w_ukv, cos, sin, tm=512):
    assert LA_HEADS % (LANES // LA_ROPE) == 0 and LA_NOPE == LANES and LA_DK == LA_NOPE + LANES
    d = h.shape[1]

    def full(a):
        return pl.BlockSpec(a.shape, lambda i: (0,) * a.ndim)

    return pl.pallas_call(
        _la_prep_kernel,
        grid=(M_ALL // tm,),
        in_specs=[pl.BlockSpec((tm, d), lambda i: (i, 0)), full(wt_small), full(gain_q), full(gain_kv),
                  full(w_uq), full(w_ukv),
                  pl.BlockSpec((tm, LANES), lambda i: (i, 0)),
                  pl.BlockSpec((tm, LANES), lambda i: (i, 0))],
        out_specs=[pl.BlockSpec((LA_HEADS, tm, LA_DK), lambda i: (0, i, 0)),
                   pl.BlockSpec((LA_HEADS, tm, LA_DK), lambda i: (0, i, 0)),
                   pl.BlockSpec((LA_HEADS, tm, LA_VDIM), lambda i: (0, i, 0)),
                   pl.BlockSpec((tm, LANES), lambda i: (i, 0))],
        out_shape=[jax.ShapeDtypeStruct((LA_HEADS, M_ALL, LA_DK), BF16),
                   jax.ShapeDtypeStruct((LA_HEADS, M_ALL, LA_DK), BF16),
                   jax.ShapeDtypeStruct((LA_HEADS, M_ALL, LA_VDIM), BF16),
                   jax.ShapeDtypeStruct((M_ALL, LANES), F32)],
        compiler_params=_params(("parallel",)),
        name="la_prep",
    )(h, wt_small, gain_q, gain_kv, w_uq, w_ukv, cos, sin)


ATT_TQ = 256
ATT_ROWS = 1024


def _attn_kernel(q_ref, kl_ref, vl_ref, kc_ref, vc_ref, kca_ref, vca_ref, o_ref, *, n_lat_blocks):
    i = pl.program_id(1)

    @pl.when(i < n_lat_blocks)
    def _():
        n_sub = ATT_ROWS // ATT_TQ

        def scores(sub):
            q = q_ref[sub * ATT_TQ:(sub + 1) * ATT_TQ, :]
            return _dot_nt(q, kl_ref[...]), _dot_nt(q, kc_ref[...])

        def softmax(s_l, s_c):
            mx = jnp.maximum(jnp.max(s_l, axis=1, keepdims=True), jnp.max(s_c, axis=1, keepdims=True))
            p_l = jnp.exp(s_l - mx)
            p_c = jnp.exp(s_c - mx)
            den = jnp.sum(p_l, axis=1, keepdims=True) + jnp.sum(p_c, axis=1, keepdims=True)
            return p_l.astype(BF16), p_c.astype(BF16), den

        def values(sub, p_l, p_c, den):
            o = _dot(p_l, vl_ref[...]) + _dot(p_c, vc_ref[...])
            o_ref[sub * ATT_TQ:(sub + 1) * ATT_TQ, :] = (o / den).astype(o_ref.dtype)

        pending_s = scores(0)
        for sub in range(n_sub):
            nxt = scores(sub + 1) if sub + 1 < n_sub else None
            values(sub, *softmax(*pending_s))
            pending_s = nxt

    @pl.when(i >= n_lat_blocks)
    def _():
        for sub in range(ATT_ROWS // CTX_LEN):
            rows = slice(sub * CTX_LEN, (sub + 1) * CTX_LEN)
            s = _dot_nt(q_ref[rows, :], kca_ref[rows, :])
            mx = jnp.max(s, axis=1, keepdims=True)
            p = jnp.exp(s - mx)
            den = jnp.sum(p, axis=1, keepdims=True)
            o = _dot(p.astype(BF16), vca_ref[rows, :])
            o_ref[rows, :] = (o / den).astype(o_ref.dtype)


def _attention(q, k, v, kv_group, need_ctx, name):
    assert M_CTX == ATT_ROWS and ATT_ROWS % ATT_TQ == 0
    n_heads, _, dk = q.shape
    dv = v.shape[2]
    n_lat_blocks = M_LAT // ATT_ROWS
    per_sample = SEQ // ATT_ROWS
    ctx0 = M_LAT // CTX_LEN
    n_blocks = n_lat_blocks + (1 if need_ctx else 0)

    def sample(i):
        return jnp.minimum(i // per_sample, BATCH - 1)

    def lat_spec(dim):
        return pl.BlockSpec((None, SEQ, dim), lambda h, i: (h // kv_group, sample(i), 0))

    def ctx_spec(dim):
        return pl.BlockSpec((None, CTX_LEN, dim), lambda h, i: (h // kv_group, ctx0 + sample(i), 0))

    def ctx_all_spec(dim):
        return pl.BlockSpec((None, M_CTX, dim), lambda h, i: (h // kv_group, n_lat_blocks, 0))

    return pl.pallas_call(
        functools.partial(_attn_kernel, n_lat_blocks=n_lat_blocks),
        grid=(n_heads, n_blocks),
        in_specs=[pl.BlockSpec((None, ATT_ROWS, dk), lambda h, i: (h, i, 0)),
                  lat_spec(dk), lat_spec(dv), ctx_spec(dk), ctx_spec(dv), ctx_all_spec(dk), ctx_all_spec(dv)],
        out_specs=pl.BlockSpec((ATT_ROWS, dv), lambda h, i: (i, h)),
        out_shape=jax.ShapeDtypeStruct((n_blocks * ATT_ROWS, n_heads * dv), BF16),
        compiler_params=_params(("parallel", "arbitrary")),
        name=name,
    )(q, k, v, k, v, k, v)


PREFIX_BLOCK = 256


def _exclusive_prefix(x_bf, upper):
    n_tok = x_bf.shape[1]
    carry = jnp.zeros((x_bf.shape[0], 1), F32)
    parts = []
    for blk in range(n_tok // PREFIX_BLOCK):
        xb = x_bf[:, blk * PREFIX_BLOCK:(blk + 1) * PREFIX_BLOCK]
        parts.append(_dot(xb, upper) + carry)
        carry = carry + jnp.sum(xb.astype(F32), axis=1, keepdims=True)
    return parts[0] if len(parts) == 1 else jnp.concatenate(parts, axis=1)


def _bitonic_sort_descending(tiles):
    n_tiles = len(tiles)
    n = n_tiles * LANES
    lane = lax.broadcasted_iota(I32, tiles[0].shape, 1)
    k = 2
    while k <= n:
        j = k // 2
        while j >= 1:
            new_tiles = []
            for t in range(n_tiles):
                x = tiles[t]
                if j < LANES:
                    low = (lane & j) == 0
                    partner = jnp.where(low, pltpu.roll(x, LANES - j, 1), pltpu.roll(x, j, 1))
                else:
                    partner = tiles[t ^ (j // LANES)]
                big = jnp.maximum(x, partner)
                small = jnp.minimum(x, partner)
                up = ((t * LANES) & k) == 0
                if j >= LANES:
                    take_big = (((t * LANES) & j) == 0) == up
                    new_tiles.append(big if take_big else small)
                elif k >= LANES:
                    new_tiles.append(jnp.where(low, big, small) if up else jnp.where(low, small, big))
                else:
                    agree = (lane & j) * (k // j) == (lane & k)
                    new_tiles.append(jnp.where(agree, big, small))
            tiles = new_tiles
            j //= 2
        k *= 2
    return tiles


def _topk_kernel(aff_ref, rank_ref, *, cap):
    a = aff_ref[...]
    n_tok = a.shape[1]
    ordered = _bitonic_sort_descending([a[:, t * LANES:(t + 1) * LANES] for t in range(n_tok // LANES)])
    pos = cap - 1
    thr = ordered[pos // LANES][:, pos % LANES:pos % LANES + 1]
    gt = a > thr
    eq = a == thr
    need = cap - jnp.sum(jnp.where(gt, 1.0, 0.0), axis=1, keepdims=True)
    ri = lax.broadcasted_iota(I32, (PREFIX_BLOCK, PREFIX_BLOCK), 0)
    ci = lax.broadcasted_iota(I32, (PREFIX_BLOCK, PREFIX_BLOCK), 1)
    upper = jnp.where(ri < ci, 1.0, 0.0).astype(BF16)
    eq_before = _exclusive_prefix(jnp.where(eq, 1.0, 0.0).astype(BF16), upper)
    sel = gt | (eq & (eq_before < need))
    slot = _exclusive_prefix(jnp.where(sel, 1.0, 0.0).astype(BF16), upper)
    rank_ref[...] = jnp.where(sel, slot.astype(I32), -1)


def _topk(aff, n_tok, cap, blk0, n_sets):
    return pl.pallas_call(
        functools.partial(_topk_kernel, cap=cap),
        grid=(n_sets,),
        in_specs=[pl.BlockSpec((N_EXPERTS, n_tok), lambda s: (0, blk0 + s))],
        out_specs=pl.BlockSpec((N_EXPERTS, n_tok), lambda s: (0, s)),
        out_shape=jax.ShapeDtypeStruct((N_EXPERTS, n_sets * n_tok), I32),
        compiler_params=_params(("parallel",)),
        name=f"topk_{n_tok}",
    )(aff)


def _gather_kernel(rank_ref, aff_ref, h_ref, *rest, cap):
    xg_ref, val_ref = rest[-2:]
    for e in range(rank_ref.shape[0]):
        r = rank_ref[e]
        n_tok = r.shape[1]
        slot = lax.broadcasted_iota(I32, (cap, n_tok), 0)
        hit = r == slot
        onehot = jnp.where(hit, 1.0, 0.0).astype(BF16)
        xg_ref[e] = _dot(onehot, h_ref[...]).astype(xg_ref.dtype)
        val_ref[e] = jnp.sum(jnp.where(hit, aff_ref[e], 0.0), axis=1, keepdims=True)


def _gather(rank3, aff3, h, n_tok, cap, blk0, row0, rows_total, prev, experts_per_step):
    d = h.shape[1]
    rb0 = row0 // cap
    eg = experts_per_step
    in_specs = [pl.BlockSpec((eg, 1, n_tok), lambda b, e: (e, 0, b)),
                pl.BlockSpec((eg, 1, n_tok), lambda b, e: (e, 0, blk0 + b)),
                pl.BlockSpec((n_tok, d), lambda b, e: (blk0 + b, 0))]
    args = [rank3, aff3, h]
    aliases = {}
    if prev is not None:
        in_specs += [pl.BlockSpec(memory_space=pl.ANY), pl.BlockSpec(memory_space=pl.ANY)]
        args += list(prev)
        aliases = {3: 0, 4: 1}
    return pl.pallas_call(
        functools.partial(_gather_kernel, cap=cap),
        grid=(BATCH, N_EXPERTS // eg),
        in_specs=in_specs,
        out_specs=[pl.BlockSpec((eg, cap, d), lambda b, e: (e, rb0 + b, 0)),
                   pl.BlockSpec((eg, cap, 1), lambda b, e: (e, rb0 + b, 0))],
        out_shape=[jax.ShapeDtypeStruct((N_EXPERTS, rows_total, d), BF16),
                   jax.ShapeDtypeStruct((N_EXPERTS, rows_total, 1), F32)],
        input_output_aliases=aliases,
        compiler_params=_params(("parallel", "arbitrary")),
        name=f"moe_gather_{n_tok}",
    )(*args)


EXPERT_COLS = 512


def _expert_kernel(x_ref, w1_ref, w3_ref, w2_ref, val_ref, y_ref, acc_ref):
    f = pl.program_id(1)
    last = pl.num_programs(1) - 1
    col_slices = [slice(c0, c0 + EXPERT_COLS) for c0 in range(0, acc_ref.shape[1], EXPERT_COLS)]

    def hidden():
        x = x_ref[...]
        a = _dot(x, w1_ref[...].astype(BF16))
        u = _dot(x, w3_ref[...].astype(BF16))
        return (a * jax.nn.sigmoid(a) * u).astype(BF16)

    def down(hm, cols):
        return _dot(hm, w2_ref[:, cols].astype(BF16))

    @pl.when(f == 0)
    def _():
        hm = hidden()
        for cols in col_slices:
            acc_ref[:, cols] = down(hm, cols)

    @pl.when((f > 0) & (f < last))
    def _():
        hm = hidden()
        for cols in col_slices:
            acc_ref[:, cols] += down(hm, cols)

    @pl.when(f == last)
    def _():
        hm = hidden()
        val = val_ref[...]
        for cols in col_slices:
            y_ref[:, cols] = ((acc_ref[:, cols] + down(hm, cols)) * val).astype(y_ref.dtype)


def _experts(xg, vals, w_e1, w_e3, w_e2, layer, tf=256):
    n_exp, rows, d = xg.shape
    ff = w_e1.shape[3]
    assert ff // tf >= 2
    return pl.pallas_call(
        _expert_kernel,
        grid=(n_exp, ff // tf),
        in_specs=[pl.BlockSpec((None, rows, d), lambda e, f: (e, 0, 0)),
                  pl.BlockSpec((None, None, d, tf), lambda e, f: (layer, e, 0, f)),
                  pl.BlockSpec((None, None, d, tf), lambda e, f: (layer, e, 0, f)),
                  pl.BlockSpec((None, None, tf, d), lambda e, f: (layer, e, f, 0)),
                  pl.BlockSpec((None, rows, 1), lambda e, f: (e, 0, 0))],
        out_specs=pl.BlockSpec((None, rows, d), lambda e, f: (e, 0, 0)),
        out_shape=jax.ShapeDtypeStruct((n_exp, rows, d), BF16),
        scratch_shapes=[pltpu.VMEM((rows, d), F32)],
        compiler_params=_params(("parallel", "arbitrary")),
        name="moe_experts",
    )(xg, w_e1, w_e3, w_e2, vals)


COMBINE_LANES = 1024


def _combine_kernel(rank_ref, y_ref, x_ref, g_ref, scale_ref, shift_ref, *rest, cap, keep_x):
    lhs_ref = rest[-1]
    outs = rest[-3:-1] if keep_x else rest[-2:-1]
    n_slots = N_EXPERTS * cap
    chunk = min(COMBINE_LANES, n_slots)
    shift = cap.bit_length() - 1

    r = rank_ref[...].astype(F32).astype(BF16)
    tm = r.shape[0]
    for c0 in range(0, n_slots, chunk):
        lane = c0 + lax.broadcasted_iota(I32, (LANES, chunk), 1)
        row = lax.broadcasted_iota(I32, (LANES, chunk), 0)
        expand = jnp.where(lax.shift_right_logical(lane, shift) == row, 1.0, 0.0).astype(BF16)
        r_exp = _dot(r, expand)
        slot = (c0 + lax.broadcasted_iota(I32, (tm, chunk), 1)) & (cap - 1)
        lhs_ref[:, c0:c0 + chunk] = jnp.where(r_exp == slot.astype(F32), 1.0, 0.0).astype(BF16)

    y = y_ref[...]
    x_new = x_ref[...] + g_ref[...] * _dot(lhs_ref[...], y.reshape(n_slots, y.shape[2]))
    if keep_x:
        outs[0][...] = x_new
    outs[-1][...] = (_rms(x_new) * scale_ref[...] + shift_ref[...]).astype(outs[-1].dtype)


def _combine(rank_t, y, x, gate_tab, scale_tab, shift_tab, prev, n_tok, cap, blk0, row0, tm, keep_x, norm_dtype):
    assert cap & (cap - 1) == 0
    d = x.shape[1]
    tpb = n_tok // tm
    xrow0 = blk0 * tpb
    rb0 = row0 // cap

    def grp(b):
        return b if n_tok == SEQ else BATCH

    def tab_spec():
        return pl.BlockSpec((None, 1, d), lambda b, i: (grp(b), 0, 0))

    def row_spec():
        return pl.BlockSpec((tm, d), lambda b, i: (xrow0 + b * tpb + i, 0))

    in_specs = [pl.BlockSpec((tm, LANES), lambda b, i: (b * tpb + i, 0)),
                pl.BlockSpec((N_EXPERTS, cap, d), lambda b, i: (0, rb0 + b, 0), pipeline_mode=pl.Buffered(1)),
                row_spec(), tab_spec(), tab_spec(), tab_spec()]
    args = [rank_t, y, x, gate_tab, scale_tab, shift_tab]
    out_shape = [jax.ShapeDtypeStruct(x.shape, norm_dtype)]
    if keep_x:
        out_shape = [jax.ShapeDtypeStruct(x.shape, F32)] + out_shape
    aliases = {}
    if prev is not None:
        in_specs += [pl.BlockSpec(memory_space=pl.ANY)] * len(prev)
        aliases = {len(args) + k: k for k in range(len(prev))}
        args += list(prev)
    return pl.pallas_call(
        functools.partial(_combine_kernel, cap=cap, keep_x=keep_x),
        grid=(BATCH, tpb),
        in_specs=in_specs,
        out_specs=[row_spec() for _ in out_shape],
        out_shape=out_shape,
        scratch_shapes=[pltpu.VMEM((tm, N_EXPERTS * cap), BF16)],
        input_output_aliases=aliases,
        compiler_params=_params(("parallel", "arbitrary")),
        name=f"moe_combine_{n_tok}",
    )(*args)


def _expert_choice(h2, aff, x, gate_tab, next_scale, next_shift, w_e1, w_e3, w_e2, layer, need_ctx):
    m_rows = h2.shape[0]
    aff3 = aff.reshape(N_EXPERTS, 1, m_rows)
    sets = [(SEQ, 0)]
    if need_ctx:
        sets.append((CTX_LEN, M_LAT // CTX_LEN))
    caps = [CAPACITY_FACTOR * n_tok // N_EXPERTS for n_tok, _ in sets]
    rows_total = sum(BATCH * cap for cap in caps)
    ranks, gathered, row0 = [], None, 0
    for (n_tok, blk0), cap in zip(sets, caps):
        rank = _topk(aff, n_tok, cap, blk0, BATCH)
        gathered = _gather(rank.reshape(N_EXPERTS, 1, BATCH * n_tok), aff3, h2, n_tok, cap, blk0, row0, rows_total,
                           gathered, experts_per_step=1 if n_tok == SEQ else N_EXPERTS)
        ranks.append((rank, row0))
        row0 += BATCH * cap
    y = _experts(gathered[0], gathered[1], w_e1, w_e3, w_e2, layer)
    out = None
    for (n_tok, blk0), cap, (rank, row0) in zip(sets, caps, ranks):
        rank_t = jnp.pad(rank.T, ((0, 0), (0, LANES - N_EXPERTS)))
        out = _combine(rank_t, y, x, gate_tab, next_scale, next_shift, out, n_tok, cap, blk0, row0,
                       tm=min(512, n_tok), keep_x=need_ctx, norm_dtype=BF16 if need_ctx else F32)
    return out


def _rope_tables(dim):
    rows = SEQ // GRID_W
    quarter = dim // 4
    inv = ROPE_BASE ** (-jnp.arange(quarter, dtype=F32) / quarter)
    row = jnp.broadcast_to(jnp.arange(rows, dtype=F32)[:, None], (rows, GRID_W)).reshape(-1)
    col = jnp.broadcast_to(jnp.arange(GRID_W, dtype=F32)[None, :], (rows, GRID_W)).reshape(-1)
    ang_r = row[:, None] * inv[None, :]
    ang_c = col[:, None] * inv[None, :]
    cos = jnp.concatenate([jnp.cos(ang_r), jnp.cos(ang_r), jnp.cos(ang_c), jnp.cos(ang_c)], axis=-1)
    sin = jnp.concatenate([-jnp.sin(ang_r), jnp.sin(ang_r), -jnp.sin(ang_c), jnp.sin(ang_c)], axis=-1)
    cos = jnp.tile(cos, (BATCH, LANES // dim))
    sin = jnp.tile(sin, (BATCH, LANES // dim))
    cos = jnp.concatenate([cos, jnp.ones((M_CTX, LANES), F32)], axis=0)
    sin = jnp.concatenate([sin, jnp.zeros((M_CTX, LANES), F32)], axis=0)
    return cos, sin


def _halves_major(a, axis=-1):
    axis = axis % a.ndim
    quarter = a.shape[axis] // 4
    a4 = a.reshape(*a.shape[:axis], 2, 2, quarter, *a.shape[axis + 1:])
    return jnp.swapaxes(a4, axis, axis + 1).reshape(a.shape)


def _mod_tables(mods):
    d = D_MODEL
    return [mods[:N_GROUPS, k * d:(k + 1) * d].reshape(N_GROUPS, 1, d) for k in range(6)]


def _layer(x_lat, x_ctx, h_pre, mods, next_scale, next_shift, layer, wt_all, ml_gate_bias, ml_norm, la_q_norm,
           la_kv_norm, la_w_uq, la_w_ukv, ga_q_norm, ga_k_norm, w_br_ml, w_br_la, w_br_ga, w_out, w_router,
           w_e1, w_e3, w_e2, ropes, need_ctx):
    d = D_MODEL
    m_rows = M_ALL if need_ctx else M_LAT
    sh1, sc1, g1, sh2, sc2, g2 = _mod_tables(mods)
    (cos_la, sin_la), (cos_ga, sin_ga) = ropes

    n_qk = GA_HEADS + GA_KV_HEADS
    wt_rows = wt_all.reshape(-1, d)
    layer_row0 = layer * wt_all.shape[1]
    wt_sm, wt_gaf = lax.optimization_barrier((wt_rows[layer_row0 + OFF_MLG:layer_row0 + OFF_GAQ],
                                              wt_rows[layer_row0 + OFF_GAQ:layer_row0 + OFF_GATE]))
    wt_small = jnp.concatenate([wt_sm[OFF_CQ - OFF_MLG:], wt_sm[:OFF_CQ - OFF_MLG],
                                jnp.zeros((SMALL_WIDTH - (OFF_GAQ - OFF_MLG), d), F32)], axis=0).astype(BF16)
    wt_gaqk = _halves_major(wt_gaf[:OFF_GAV - OFF_GAQ].reshape(n_qk, GA_HEAD_DIM, d), axis=1).reshape(-1, d)
    wt_ga = jnp.concatenate([wt_gaqk, wt_gaf[OFF_GAV - OFF_GAQ:]], axis=0).astype(BF16)
    w_uq = la_w_uq.reshape(LA_Q_RANK, LA_HEADS, LA_NOPE + LA_ROPE)
    w_uq = jnp.concatenate([w_uq[:, :, :LA_NOPE].reshape(LA_Q_RANK, LA_HEADS * LA_NOPE),
                            w_uq[:, :, LA_NOPE:].reshape(LA_Q_RANK, LA_HEADS * LA_ROPE)], axis=1).astype(BF16)

    h = _rms_mod(x_lat, x_ctx, 1.0 + sc1, sh1, M_ALL) if h_pre is None else h_pre
    p_ml = _mm(h, wt_rows, layer_row0, OFF_MLG, name="proj_ml")
    q_la, k_la, v_la, tail = _la_prep(h, wt_small, la_q_norm.reshape(1, LA_Q_RANK),
                                      la_kv_norm.reshape(1, LA_KV_RANK), w_uq, la_w_ukv.astype(BF16), cos_la, sin_la)
    q_ga, k_ga, v_ga = _ga_prep(h, wt_ga, _halves_major(ga_q_norm.reshape(1, GA_HEAD_DIM)),
                                _halves_major(ga_k_norm.reshape(1, GA_HEAD_DIM)), cos_ga, sin_ga)

    gates = (tail[:, TAIL_MLG:TAIL_MLG + ML_GATES] + ml_gate_bias[None, :]).reshape(M_ALL, 2, 2, ML_HEADS)
    gates = jnp.stack([gates[:, 0, 0], jax.nn.log_sigmoid(gates[:, 0, 1]),
                       gates[:, 1, 0], jax.nn.log_sigmoid(gates[:, 1, 1])], axis=-1)
    n_lat = SEQ // ML_CHUNK
    g_lat = gates[:M_LAT].reshape(BATCH, n_lat, ML_CHUNK, ML_HEADS, 4)
    g_ctx = gates[M_LAT:].reshape(BATCH, CTX_LEN, ML_HEADS, 4)
    y_ml_lat, y_ml_ctx = _mlstm(p_ml, jnp.transpose(g_lat, (0, 3, 1, 4, 2)), jnp.transpose(g_lat, (0, 3, 1, 2, 4)),
                                jnp.transpose(g_ctx, (0, 2, 3, 1)), jnp.transpose(g_ctx, (0, 2, 1, 3)),
                                ml_norm.reshape(1, ML_WIDTH))

    y_la = _attention(q_la, k_la, v_la, 1, need_ctx, "attn_mla")
    y_ga = _attention(q_ga, k_ga, v_ga, GA_HEADS // GA_KV_HEADS, need_ctx, "attn_gqa")

    merged = _merge(h, y_ml_lat, y_ml_ctx, y_la, y_ga, wt_rows, layer, layer_row0 + OFF_GATE, w_br_ml, w_br_la,
                    w_br_ga, m_rows)
    x, h2, aff = _out_proj(merged, w_out.astype(BF16), w_router.T.astype(BF16), x_lat, x_ctx if need_ctx else None,
                           g1, 1.0 + sc2, sh2, m_rows)
    return _expert_choice(h2, aff, x, g2, next_scale, next_shift, w_e1, w_e3, w_e2, layer, need_ctx)


def kernel(x, c, ctx, c_ctx, w_mod, b_mod, w_in, ml_gate_bias, ml_norm, la_q_norm, la_kv_norm, la_w_uq, la_w_ukv,
           ga_q_norm, ga_k_norm, w_br_ml, w_br_la, w_br_ga, w_out, w_router, w_e1, w_e3, w_e2, final_norm):
    d = D_MODEL
    cc = jnp.concatenate([c, c_ctx[None, :], jnp.zeros((MODS_ROWS - N_GROUPS, d), F32)], axis=0)
    mods = _mods(cc, w_mod, b_mod.reshape(DEPTH, 1, 6 * d))
    ropes = (_rope_tables(LA_ROPE), tuple(_halves_major(t) for t in _rope_tables(GA_HEAD_DIM)))
    xs, xs_ctx, h_pre = x.reshape(M_LAT, d), ctx.reshape(M_CTX, d), None
    wt_in = jnp.swapaxes(w_in, 1, 2)
    for l in range(DEPTH):
        need_ctx = l < DEPTH - 1
        if need_ctx:
            sh1_next, sc1_next = _mod_tables(mods[l + 1])[:2]
            next_scale, next_shift = 1.0 + sc1_next, sh1_next
        else:
            next_scale = jnp.broadcast_to(final_norm.reshape(1, 1, d), (N_GROUPS, 1, d))
            next_shift = jnp.zeros((N_GROUPS, 1, d), F32)
        out = _layer(xs, xs_ctx, h_pre, mods[l], next_scale, next_shift, l, wt_in, ml_gate_bias[l], ml_norm[l],
                     la_q_norm[l], la_kv_norm[l], la_w_uq[l], la_w_ukv[l], ga_q_norm[l], ga_k_norm[l],
                     w_br_ml, w_br_la, w_br_ga, w_out[l], w_router[l], w_e1, w_e3, w_e2, ropes, need_ctx)
        if need_ctx:
            xs, h_pre = out
            xs_ctx = None
    return out[0].reshape(BATCH, SEQ, d)
```

```python
import functools

import jax
import jax.numpy as jnp
from jax import lax
from jax.experimental import pallas as pl
from jax.experimental.pallas import tpu as pltpu

F32 = jnp.float32
BF16 = jnp.bfloat16
I32 = jnp.int32

D_MODEL = 2048
BATCH = 4
SEQ = 2048
DEPTH = 2
GRID_W = 64
CTX_LEN = 256
EPS = 1e-6
ROPE_BASE = 10000.0

ML_HEADS = 4
ML_HEAD_DIM = 256
ML_WIDTH = ML_HEADS * ML_HEAD_DIM
ML_GATES = 2 * 2 * ML_HEADS
ML_CHUNK = 256

LA_HEADS = 8
LA_NOPE = 128
LA_ROPE = 64
LA_VDIM = 128
LA_Q_RANK = 512
LA_KV_RANK = 256
LA_DK = 256

GA_HEADS = 8
GA_KV_HEADS = 2
GA_HEAD_DIM = 128
GA_WIDTH = GA_HEADS * GA_HEAD_DIM
GA_KV_WIDTH = GA_KV_HEADS * GA_HEAD_DIM

N_EXPERTS = 16
CAPACITY_FACTOR = 2

M_LAT = BATCH * SEQ
M_CTX = BATCH * CTX_LEN
M_ALL = M_LAT + M_CTX
MODS_ROWS = 16
N_GROUPS = BATCH + 1

LANES = 128
SUBLANES = 8

OFF_MLG = 4 * ML_WIDTH
OFF_CQ = OFF_MLG + ML_GATES
OFF_CKV = OFF_CQ + LA_Q_RANK
OFF_KR = OFF_CKV + LA_KV_RANK
OFF_GAQ = OFF_KR + LA_ROPE
OFF_GAK = OFF_GAQ + GA_WIDTH
OFF_GAV = OFF_GAK + GA_KV_WIDTH
OFF_GATE = OFF_GAV + GA_KV_WIDTH
SM_CKV = LA_Q_RANK
SM_TAIL = SM_CKV + LA_KV_RANK
SMALL_WIDTH = SM_TAIL + LANES
TAIL_MLG = LA_ROPE

VMEM_LIMIT = 56 * 1024 * 1024


def _params(semantics, vmem=VMEM_LIMIT):
    return pltpu.CompilerParams(dimension_semantics=semantics, vmem_limit_bytes=vmem)


def _dot(a, b):
    return jnp.dot(a, b, preferred_element_type=F32)


def _dot_nt(a, b):
    return lax.dot_general(a, b, (((1,), (1,)), ((), ())), preferred_element_type=F32)


def _rms(x):
    return x * lax.rsqrt(jnp.mean(x * x, axis=-1, keepdims=True) + EPS)


def _group_of_block(tm):
    return lambda i: (i * tm) // SEQ


def _mods_kernel(c_ref, w_ref, b_ref, o_ref):
    c = c_ref[...]
    a = (c * jax.nn.sigmoid(c)).astype(BF16)
    part = _dot(a, w_ref[...].astype(BF16))

    @pl.when(pl.program_id(1) == 0)
    def _():
        o_ref[...] = part + b_ref[...]

    @pl.when(pl.program_id(1) > 0)
    def _():
        o_ref[...] += part


def _mods(cc, w_mod, b_mod, tk=256):
    depth, d, n = w_mod.shape
    rows = cc.shape[0]
    return pl.pallas_call(
        _mods_kernel,
        grid=(depth, d // tk),
        in_specs=[pl.BlockSpec((rows, tk), lambda l, k: (0, k)),
                  pl.BlockSpec((None, tk, n), lambda l, k: (l, k, 0)),
                  pl.BlockSpec((None, 1, n), lambda l, k: (l, 0, 0))],
        out_specs=pl.BlockSpec((None, rows, n), lambda l, k: (l, 0, 0)),
        out_shape=jax.ShapeDtypeStruct((depth, rows, n), F32),
        compiler_params=_params(("parallel", "arbitrary")),
        name="mods",
    )(cc, w_mod, b_mod)


def _row_sources(x_lat, x_ctx, tm, n_grid_axes=1):
    d = x_lat.shape[1]
    n_lat_tiles = M_LAT // tm

    def spec(fn):
        return pl.BlockSpec((tm, d), (lambda i: (fn(i), 0)) if n_grid_axes == 1 else (lambda j, i: (fn(i), 0)))

    if x_ctx is None:
        return [spec(lambda i: i)], [x_lat], lambda i, refs: refs[0][...]
    specs = [spec(lambda i: jnp.minimum(i, n_lat_tiles - 1)), spec(lambda i: jnp.maximum(i - n_lat_tiles, 0))]
    return specs, [x_lat, x_ctx], lambda i, refs: jnp.where(i < n_lat_tiles, refs[0][...], refs[1][...])


def _rms_kernel(*refs, pick):
    scale_ref, shift_ref, o_ref = refs[-3:]
    x = pick(pl.program_id(0), refs[:-3])
    o_ref[...] = (_rms(x) * scale_ref[...] + shift_ref[...]).astype(o_ref.dtype)


def _rms_mod(x_lat, x_ctx, scale_tab, shift_tab, m_rows, tm=512):
    d = x_lat.shape[1]
    grp = _group_of_block(tm)
    x_specs, x_args, pick = _row_sources(x_lat, x_ctx, tm)
    return pl.pallas_call(
        functools.partial(_rms_kernel, pick=pick),
        grid=(m_rows // tm,),
        in_specs=x_specs + [pl.BlockSpec((None, 1, d), lambda i: (grp(i), 0, 0)),
                            pl.BlockSpec((None, 1, d), lambda i: (grp(i), 0, 0))],
        out_specs=pl.BlockSpec((tm, d), lambda i: (i, 0)),
        out_shape=jax.ShapeDtypeStruct((m_rows, d), BF16),
        compiler_params=_params(("parallel",)),
        name="rms_mod",
    )(*x_args, scale_tab, shift_tab)


def _mm_kernel(a_ref, wt_ref, o_ref, wbf_ref):
    @pl.when(pl.program_id(1) == 0)
    def _():
        wbf_ref[...] = wt_ref[...].astype(BF16)

    o_ref[...] = _dot_nt(a_ref[...], wbf_ref[...]).astype(o_ref.dtype)


def _mm(a, wt_all, row0, n_cols, *, tm=1024, tn=1024, out_dtype=F32, name="mm"):
    m_rows, k = a.shape
    assert m_rows % tm == 0 and n_cols % tn == 0
    return pl.pallas_call(
        _mm_kernel,
        grid=(n_cols // tn, m_rows // tm),
        in_specs=[pl.BlockSpec((tm, k), lambda j, i: (i, 0)),
                  pl.BlockSpec((pl.Element(tn), pl.Element(k)),
                               lambda j, i: (pl.multiple_of(row0 + j * tn, SUBLANES), 0))],
        out_specs=pl.BlockSpec((tm, tn), lambda j, i: (i, j)),
        out_shape=jax.ShapeDtypeStruct((m_rows, n_cols), out_dtype),
        scratch_shapes=[pltpu.VMEM((tn, k), BF16)],
        compiler_params=_params(("parallel", "arbitrary")),
        name=name,
    )(a, wt_all)


def _out_proj_kernel(a_ref, w_ref, wr_ref, *refs, pick):
    g_ref, scale_ref, shift_ref, x_out_ref, h_out_ref, aff_ref = refs[-6:]
    x = pick(pl.program_id(0), refs[:-6])
    x_new = x + g_ref[...] * _dot(a_ref[...], w_ref[...])
    x_out_ref[...] = x_new
    h2 = (_rms(x_new) * scale_ref[...] + shift_ref[...]).astype(h_out_ref.dtype)
    h_out_ref[...] = h2
    logits = _dot_nt(wr_ref[...], h2)
    e = jnp.exp(logits - jnp.max(logits, axis=0, keepdims=True))
    aff_ref[...] = e / jnp.sum(e, axis=0, keepdims=True)


def _out_proj(a, w, w_router_t, x_lat, x_ctx, gate_tab, scale_tab, shift_tab, m_rows, tm=512):
    k = a.shape[1]
    d = w.shape[1]
    grp = _group_of_block(tm)
    x_specs, x_args, pick = _row_sources(x_lat, x_ctx, tm)
    tab_spec = pl.BlockSpec((None, 1, d), lambda i: (grp(i), 0, 0))
    return pl.pallas_call(
        functools.partial(_out_proj_kernel, pick=pick),
        grid=(m_rows // tm,),
        in_specs=[pl.BlockSpec((tm, k), lambda i: (i, 0)),
                  pl.BlockSpec((k, d), lambda i: (0, 0)),
                  pl.BlockSpec((N_EXPERTS, d), lambda i: (0, 0))] + x_specs + [tab_spec, tab_spec, tab_spec],
        out_specs=[pl.BlockSpec((tm, d), lambda i: (i, 0)), pl.BlockSpec((tm, d), lambda i: (i, 0)),
                   pl.BlockSpec((N_EXPERTS, tm), lambda i: (0, i))],
        out_shape=[jax.ShapeDtypeStruct((m_rows, d), F32), jax.ShapeDtypeStruct((m_rows, d), BF16),
                   jax.ShapeDtypeStruct((N_EXPERTS, m_rows), F32)],
        compiler_params=_params(("parallel",)),
        name="out_proj",
    )(a, w, w_router_t, *x_args, gate_tab, scale_tab, shift_tab)


def _merge_kernel(h_ref, y0l_ref, y0c_ref, y1_ref, y2_ref, wg0_ref, wg1_ref, wg2_ref, wb0_ref, wb1_ref, wb2_ref,
                  o_ref, wg_bf_ref, wb_bf_ref, *, n_lat_tiles):
    @pl.when(pl.program_id(1) == 0)
    def _():
        for b, (wg_ref, wb_ref) in enumerate(((wg0_ref, wb0_ref), (wg1_ref, wb1_ref), (wg2_ref, wb2_ref))):
            wg_bf_ref[b] = wg_ref[...].astype(BF16)
            wb_bf_ref[b] = wb_ref[...].astype(BF16)

    h = h_ref[...]
    y0 = jnp.where(pl.program_id(1) < n_lat_tiles, y0l_ref[...], y0c_ref[...])
    acc = jax.nn.sigmoid(_dot_nt(h, wg_bf_ref[0])) * _dot(y0, wb_bf_ref[0])
    acc = acc + jax.nn.sigmoid(_dot_nt(h, wg_bf_ref[1])) * _dot(y1_ref[...], wb_bf_ref[1])
    acc = acc + jax.nn.sigmoid(_dot_nt(h, wg_bf_ref[2])) * _dot(y2_ref[...], wb_bf_ref[2])
    o_ref[...] = acc.astype(o_ref.dtype)


def _merge(h, y_ml_lat, y_ml_ctx, y_la, y_ga, wt_all, layer, gate_row0, w_br_ml, w_br_la, w_br_ga, m_rows,
           tm=512, tn=512):
    d = D_MODEL
    nb = d // tn
    n_lat_tiles = M_LAT // tm
    y_spec = pl.BlockSpec((tm, ML_WIDTH), lambda j, i: (i, 0))
    wb_spec = pl.BlockSpec((None, ML_WIDTH, tn), lambda j, i: (layer, 0, j), pipeline_mode=pl.Buffered(1))

    def wg_spec(branch):
        return pl.BlockSpec((pl.Element(tn), pl.Element(d)),
                            lambda j, i: (pl.multiple_of(gate_row0 + branch * d + j * tn, SUBLANES), 0),
                            pipeline_mode=pl.Buffered(1))

    return pl.pallas_call(
        functools.partial(_merge_kernel, n_lat_tiles=n_lat_tiles),
        grid=(nb, m_rows // tm),
        in_specs=[pl.BlockSpec((tm, d), lambda j, i: (i, 0)),
                  pl.BlockSpec((tm, ML_WIDTH), lambda j, i: (jnp.minimum(i, n_lat_tiles - 1), 0)),
                  pl.BlockSpec((tm, ML_WIDTH), lambda j, i: (jnp.maximum(i - n_lat_tiles, 0), 0)),
                  y_spec, y_spec, wg_spec(0), wg_spec(1), wg_spec(2), wb_spec, wb_spec, wb_spec],
        out_specs=pl.BlockSpec((tm, tn), lambda j, i: (i, j)),
        out_shape=jax.ShapeDtypeStruct((m_rows, d), BF16),
        scratch_shapes=[pltpu.VMEM((3, tn, d), BF16), pltpu.VMEM((3, ML_WIDTH, tn), BF16)],
        compiler_params=_params(("parallel", "arbitrary")),
        name="merge",
    )(h, y_ml_lat, y_ml_ctx, y_la, y_ga, wt_all, wt_all, wt_all, w_br_ml, w_br_la, w_br_ga)


def _mlstm_kernel(ql_ref, kl_ref, vl_ref, ol_ref, qc_ref, kc_ref, vc_ref, oc_ref,
                  grl_ref, gcl_ref, grc_ref, gcc_ref, gain_ref, yl_ref, yc_ref,
                  qb_ref, kt_ref, vb_ref, s_ref, hsum_ref):
    length = ML_CHUNK
    hd = ML_HEAD_DIM
    n_lat = SEQ // length
    n_chunks = n_lat + 1
    scale = hd ** -0.5

    def load(c, ctx_ref, lat_ref):
        return ctx_ref[...] if c == 0 else lat_ref[(c - 1) * length:c * length, :]

    for c in range(n_chunks):
        qb = (load(c, qc_ref, ql_ref) * scale).astype(BF16)
        kt = jnp.transpose(load(c, kc_ref, kl_ref)).astype(BF16)
        qb_ref[c] = qb
        kt_ref[c] = kt
        vb_ref[c] = load(c, vc_ref, vl_ref).astype(BF16)
        s_ref[c] = _dot(qb, kt)

    ri = lax.broadcasted_iota(I32, (length, length), 0)
    ci = lax.broadcasted_iota(I32, (length, length), 1)
    written = set()

    def chunk_step(direction, c, state, last):
        m, ct, n_vec = state
        reverse = direction == 1
        row_i = 2 * direction
        allowed = (ci >= ri) if reverse else (ci <= ri)
        allowed_t = (ri >= ci) if reverse else (ri <= ci)
        g_rows = grc_ref[...] if c == 0 else grl_ref[c - 1]
        g_cols = gcc_ref[...] if c == 0 else gcl_ref[c - 1]
        li_r = g_rows[row_i:row_i + 1, :]
        lf_r = g_rows[row_i + 1:row_i + 2, :]
        li_c = g_cols[:, row_i:row_i + 1]
        lf_c = g_cols[:, row_i + 1:row_i + 2]
        cum_c = jnp.sum(jnp.where(allowed, lf_r, 0.0), axis=1, keepdims=True)
        cum_r = jnp.sum(jnp.where(allowed_t, lf_c, 0.0), axis=0, keepdims=True)
        total = jnp.sum(lf_r, axis=1, keepdims=True)
        dmat = jnp.where(allowed, cum_c - cum_r + li_r, -jnp.inf)
        m_loc = jnp.max(dmat, axis=1, keepdims=True)
        s_loc = s_ref[c] * jnp.exp(dmat - m_loc)
        intra = _dot(s_loc.astype(BF16), vb_ref[c])
        intra_sum = jnp.sum(s_loc, axis=1, keepdims=True)
        inter = cum_c + m
        m_row = jnp.maximum(inter, m_loc)
        shrink = jnp.exp(-jnp.abs(inter - m_loc))
        f_loc = jnp.where(m_loc >= inter, 1.0, shrink)
        w_inter = jnp.where(inter >= m_loc, 1.0, shrink)
        q32 = load(c, qc_ref, ql_ref) * scale
        num = intra * f_loc + w_inter * _dot(qb_ref[c], ct.astype(BF16))
        den = intra_sum * f_loc + w_inter * jnp.sum(q32 * n_vec, axis=1, keepdims=True)
        h_out = num / jnp.maximum(jnp.abs(den), jnp.exp(-m_row))
        rows = slice(c * length, (c + 1) * length)
        if c in written:
            hsum_ref[rows, :] += h_out
        else:
            hsum_ref[rows, :] = h_out
            written.add(c)
        if last:
            return state
        g_r = total - cum_r + li_r
        g_c = total - cum_c + li_c
        m_new = jnp.maximum(total + m, jnp.max(g_r, axis=1, keepdims=True))
        wk_c = jnp.exp(g_c - m_new)
        decay = jnp.exp(total + m - m_new)
        wv = (wk_c * load(c, vc_ref, vl_ref)).astype(BF16)
        ct_new = decay * ct + _dot(kt_ref[c], wv)
        n_new = decay * n_vec + jnp.sum(wk_c * load(c, kc_ref, kl_ref), axis=0, keepdims=True)
        return m_new, ct_new, n_new

    orders = ([0] + list(range(1, n_chunks)), [0] + [n_lat - j for j in range(n_lat)])
    zero_state = (jnp.zeros((1, 1), F32), jnp.zeros((hd, hd), F32), jnp.zeros((1, hd), F32))
    states = [zero_state, zero_state]
    for pos in range(n_chunks):
        for direction in range(2):
            states[direction] = chunk_step(direction, orders[direction][pos], states[direction],
                                           last=pos == n_chunks - 1)

    gain = gain_ref[...]

    def finish(rows, o_pre):
        return (_rms(hsum_ref[rows, :]) * gain * jax.nn.sigmoid(o_pre)).astype(BF16)

    yc_ref[...] = finish(slice(0, CTX_LEN), oc_ref[...])
    yl_ref[...] = finish(slice(CTX_LEN, CTX_LEN + SEQ), ol_ref[...])


def _mlstm(p_ml, g_row_lat, g_col_lat, g_row_ctx, g_col_ctx, gain):
    assert CTX_LEN == ML_CHUNK
    hd = ML_HEAD_DIM
    length = ML_CHUNK
    n_lat = SEQ // length
    ctx0 = M_LAT // CTX_LEN

    def lat_spec(col):
        return pl.BlockSpec((SEQ, hd), lambda b, h: (b, col * ML_HEADS + h))

    def ctx_spec(col):
        return pl.BlockSpec((CTX_LEN, hd), lambda b, h: (ctx0 + b, col * ML_HEADS + h))

    return pl.pallas_call(
        _mlstm_kernel,
        grid=(BATCH, ML_HEADS),
        in_specs=[lat_spec(0), lat_spec(1), lat_spec(2), lat_spec(3),
                  ctx_spec(0), ctx_spec(1), ctx_spec(2), ctx_spec(3),
                  pl.BlockSpec((None, None, n_lat, 4, length), lambda b, h: (b, h, 0, 0, 0)),
                  pl.BlockSpec((None, None, n_lat, length, 4), lambda b, h: (b, h, 0, 0, 0)),
                  pl.BlockSpec((None, None, 4, length), lambda b, h: (b, h, 0, 0)),
                  pl.BlockSpec((None, None, length, 4), lambda b, h: (b, h, 0, 0)),
                  pl.BlockSpec((1, hd), lambda b, h: (0, h))],
        out_specs=[pl.BlockSpec((SEQ, hd), lambda b, h: (b, h)),
                   pl.BlockSpec((CTX_LEN, hd), lambda b, h: (b, h))],
        out_shape=[jax.ShapeDtypeStruct((M_LAT, ML_WIDTH), BF16),
                   jax.ShapeDtypeStruct((M_CTX, ML_WIDTH), BF16)],
        scratch_shapes=[pltpu.VMEM((1 + n_lat, length, hd), BF16),
                        pltpu.VMEM((1 + n_lat, hd, length), BF16),
                        pltpu.VMEM((1 + n_lat, length, hd), BF16),
                        pltpu.VMEM((1 + n_lat, length, length), F32),
                        pltpu.VMEM((CTX_LEN + SEQ, hd), F32)],
        compiler_params=_params(("parallel", "parallel")),
        name="mlstm",
    )(p_ml, p_ml, p_ml, p_ml, p_ml, p_ml, p_ml, p_ml, g_row_lat, g_col_lat, g_row_ctx, g_col_ctx, gain)


def _rope_partner(x, half):
    lane = lax.broadcasted_iota(I32, x.shape, 1)
    first = (lane & (2 * half - 1)) < half
    return jnp.where(first, pltpu.roll(x, LANES - half, 1), pltpu.roll(x, half, 1))


GA_PIECE = 512


def _ga_prep_kernel(h_ref, wa_ref, wb_ref, wc_ref, gq_ref, gk_ref, cos_ref, sin_ref, q_ref, k_ref, v_ref):
    h = h_ref[...]
    cos = cos_ref[...]
    sin = sin_ref[...]
    dh = GA_HEAD_DIM
    per_piece = GA_PIECE // dh

    def norm_rope(x, gain):
        y = _rms(x) * gain
        return y * cos + pltpu.roll(y, dh // 2, 1) * sin

    for piece, w_ref in enumerate((wa_ref, wb_ref)):
        p = _dot_nt(h, w_ref[...])
        for j in range(per_piece):
            q = norm_rope(p[:, j * dh:(j + 1) * dh], gq_ref[...])
            q_ref[piece * per_piece + j] = (q * dh ** -0.5).astype(BF16)
    p = _dot_nt(h, wc_ref[...])
    for g in range(GA_KV_HEADS):
        k_ref[g] = norm_rope(p[:, g * dh:(g + 1) * dh], gk_ref[...]).astype(BF16)
        v_ref[g] = p[:, GA_KV_WIDTH + g * dh:GA_KV_WIDTH + (g + 1) * dh].astype(BF16)


def _ga_prep(h, wt_ga, gain_q, gain_k, cos, sin, tm=512):
    assert GA_WIDTH == 2 * GA_PIECE and 2 * GA_KV_WIDTH == GA_PIECE
    d = h.shape[1]
    dh = GA_HEAD_DIM

    def head_spec(n_heads):
        return pl.BlockSpec((n_heads, tm, dh), lambda i: (0, i, 0))

    def piece_spec(j):
        return pl.BlockSpec((GA_PIECE, d), lambda i: (j, 0))

    return pl.pallas_call(
        _ga_prep_kernel,
        grid=(M_ALL // tm,),
        in_specs=[pl.BlockSpec((tm, d), lambda i: (i, 0)), piece_spec(0), piece_spec(1), piece_spec(2),
                  pl.BlockSpec((1, dh), lambda i: (0, 0)),
                  pl.BlockSpec((1, dh), lambda i: (0, 0)),
                  pl.BlockSpec((tm, dh), lambda i: (i, 0)),
                  pl.BlockSpec((tm, dh), lambda i: (i, 0))],
        out_specs=[head_spec(GA_HEADS), head_spec(GA_KV_HEADS), head_spec(GA_KV_HEADS)],
        out_shape=[jax.ShapeDtypeStruct((GA_HEADS, M_ALL, dh), BF16),
                   jax.ShapeDtypeStruct((GA_KV_HEADS, M_ALL, dh), BF16),
                   jax.ShapeDtypeStruct((GA_KV_HEADS, M_ALL, dh), BF16)],
        compiler_params=_params(("parallel",)),
        name="ga_prep",
    )(h, wt_ga, wt_ga, wt_ga, gain_q, gain_k, cos, sin)


def _la_prep_kernel(h_ref, wsm_ref, gq_ref, gkv_ref, wuq_ref, wukv_ref, cos_ref, sin_ref,
                    q_ref, k_ref, v_ref, tail_ref):
    p = _dot_nt(h_ref[...], wsm_ref[...])
    tail = p[:, SM_TAIL:SM_TAIL + LANES]
    tail_ref[...] = tail
    cq = (_rms(p[:, :LA_Q_RANK]) * gq_ref[...]).astype(BF16)
    ckv = (_rms(p[:, SM_CKV:SM_CKV + LA_KV_RANK]) * gkv_ref[...]).astype(BF16)
    qf = _dot(cq, wuq_ref[...])
    kvf = _dot(ckv, wukv_ref[...])
    cos = cos_ref[...]
    sin = sin_ref[...]
    scale = (LA_NOPE + LA_ROPE) ** -0.5
    low = lax.broadcasted_iota(I32, tail.shape, 1) < LA_ROPE

    def rope(x):
        return x * cos + _rope_partner(x, LA_ROPE // 4) * sin

    kr = rope(tail)
    kr2 = jnp.where(low, kr, pltpu.roll(kr, LA_ROPE, 1)).astype(BF16)
    heads_per_tile = LANES // LA_ROPE
    for j in range(LA_HEADS // heads_per_tile):
        c0 = LA_HEADS * LA_NOPE + j * LANES
        qr = rope(qf[:, c0:c0 + LANES]) * scale
        q_ref[2 * j, :, LA_NOPE:] = jnp.where(low, qr, 0.0).astype(BF16)
        q_ref[2 * j + 1, :, LA_NOPE:] = jnp.where(low, 0.0, qr).astype(BF16)
    for h in range(LA_HEADS):
        q_ref[h, :, :LA_NOPE] = (qf[:, h * LA_NOPE:(h + 1) * LA_NOPE] * scale).astype(BF16)
        kv0 = h * (LA_NOPE + LA_VDIM)
        k_ref[h, :, :LA_NOPE] = kvf[:, kv0:kv0 + LA_NOPE].astype(BF16)
        k_ref[h, :, LA_NOPE:] = kr2
        v_ref[h] = kvf[:, kv0 + LA_NOPE:kv0 + LA_NOPE + LA_VDIM].astype(BF16)


def _la_prep(h, wt_small, gain_q, gain_kv, w_uq, w_ukv, cos, sin, tm=512):
    assert LA_HEADS % (LANES // LA_ROPE) == 0 and LA_NOPE == LANES and LA_DK == LA_NOPE + LANES
    d = h.shape[1]

    def full(a):
        return pl.BlockSpec(a.shape, lambda i: (0,) * a.ndim)

    return pl.pallas_call(
        _la_prep_kernel,
        grid=(M_ALL // tm,),
        in_specs=[pl.BlockSpec((tm, d), lambda i: (i, 0)), full(wt_small), full(gain_q), full(gain_kv),
                  full(w_uq), full(w_ukv),
                  pl.BlockSpec((tm, LANES), lambda i: (i, 0)),
                  pl.BlockSpec((tm, LANES), lambda i: (i, 0))],
        out_specs=[pl.BlockSpec((LA_HEADS, tm, LA_DK), lambda i: (0, i, 0)),
                   pl.BlockSpec((LA_HEADS, tm, LA_DK), lambda i: (0, i, 0)),
                   pl.BlockSpec((LA_HEADS, tm, LA_VDIM), lambda i: (0, i, 0)),
                   pl.BlockSpec((tm, LANES), lambda i: (i, 0))],
        out_shape=[jax.ShapeDtypeStruct((LA_HEADS, M_ALL, LA_DK), BF16),
                   jax.ShapeDtypeStruct((LA_HEADS, M_ALL, LA_DK), BF16),
                   jax.ShapeDtypeStruct((LA_HEADS, M_ALL, LA_VDIM), BF16),
                   jax.ShapeDtypeStruct((M_ALL, LANES), F32)],
        compiler_params=_params(("parallel",)),
        name="la_prep",
    )(h, wt_small, gain_q, gain_kv, w_uq, w_ukv, cos, sin)


ATT_TQ = 256
ATT_ROWS = 1024


def _attn_kernel(q_ref, kl_ref, vl_ref, kc_ref, vc_ref, kca_ref, vca_ref, o_ref, *, n_lat_blocks):
    i = pl.program_id(1)

    @pl.when(i < n_lat_blocks)
    def _():
        n_sub = ATT_ROWS // ATT_TQ

        def scores(sub):
            q = q_ref[sub * ATT_TQ:(sub + 1) * ATT_TQ, :]
            return _dot_nt(q, kl_ref[...]), _dot_nt(q, kc_ref[...])

        def softmax(s_l, s_c):
            mx = jnp.maximum(jnp.max(s_l, axis=1, keepdims=True), jnp.max(s_c, axis=1, keepdims=True))
            p_l = jnp.exp(s_l - mx)
            p_c = jnp.exp(s_c - mx)
            den = jnp.sum(p_l, axis=1, keepdims=True) + jnp.sum(p_c, axis=1, keepdims=True)
            return p_l.astype(BF16), p_c.astype(BF16), den

        def values(sub, p_l, p_c, den):
            o = _dot(p_l, vl_ref[...]) + _dot(p_c, vc_ref[...])
            o_ref[sub * ATT_TQ:(sub + 1) * ATT_TQ, :] = (o / den).astype(o_ref.dtype)

        pending_s = scores(0)
        for sub in range(n_sub):
            nxt = scores(sub + 1) if sub + 1 < n_sub else None
            values(sub, *softmax(*pending_s))
            pending_s = nxt

    @pl.when(i >= n_lat_blocks)
    def _():
        for sub in range(ATT_ROWS // CTX_LEN):
            rows = slice(sub * CTX_LEN, (sub + 1) * CTX_LEN)
            s = _dot_nt(q_ref[rows, :], kca_ref[rows, :])
            mx = jnp.max(s, axis=1, keepdims=True)
            p = jnp.exp(s - mx)
            den = jnp.sum(p, axis=1, keepdims=True)
            o = _dot(p.astype(BF16), vca_ref[rows, :])
            o_ref[rows, :] = (o / den).astype(o_ref.dtype)


def _attention(q, k, v, kv_group, need_ctx, name):
    assert M_CTX == ATT_ROWS and ATT_ROWS % ATT_TQ == 0
    n_heads, _, dk = q.shape
    dv = v.shape[2]
    n_lat_blocks = M_LAT // ATT_ROWS
    per_sample = SEQ // ATT_ROWS
    ctx0 = M_LAT // CTX_LEN
    n_blocks = n_lat_blocks + (1 if need_ctx else 0)

    def sample(i):
        return jnp.minimum(i // per_sample, BATCH - 1)

    def lat_spec(dim):
        return pl.BlockSpec((None, SEQ, dim), lambda h, i: (h // kv_group, sample(i), 0))

    def ctx_spec(dim):
        return pl.BlockSpec((None, CTX_LEN, dim), lambda h, i: (h // kv_group, ctx0 + sample(i), 0))

    def ctx_all_spec(dim):
        return pl.BlockSpec((None, M_CTX, dim), lambda h, i: (h // kv_group, n_lat_blocks, 0))

    return pl.pallas_call(
        functools.partial(_attn_kernel, n_lat_blocks=n_lat_blocks),
        grid=(n_heads, n_blocks),
        in_specs=[pl.BlockSpec((None, ATT_ROWS, dk), lambda h, i: (h, i, 0)),
                  lat_spec(dk), lat_spec(dv), ctx_spec(dk), ctx_spec(dv), ctx_all_spec(dk), ctx_all_spec(dv)],
        out_specs=pl.BlockSpec((ATT_ROWS, dv), lambda h, i: (i, h)),
        out_shape=jax.ShapeDtypeStruct((n_blocks * ATT_ROWS, n_heads * dv), BF16),
        compiler_params=_params(("parallel", "arbitrary")),
        name=name,
    )(q, k, v, k, v, k, v)


PREFIX_BLOCK = 256


def _exclusive_prefix(x_bf, upper):
    n_tok = x_bf.shape[1]
    carry = jnp.zeros((x_bf.shape[0], 1), F32)
    parts = []
    for blk in range(n_tok // PREFIX_BLOCK):
        xb = x_bf[:, blk * PREFIX_BLOCK:(blk + 1) * PREFIX_BLOCK]
        parts.append(_dot(xb, upper) + carry)
        carry = carry + jnp.sum(xb.astype(F32), axis=1, keepdims=True)
    return parts[0] if len(parts) == 1 else jnp.concatenate(parts, axis=1)


def _bitonic_sort_descending(tiles):
    n_tiles = len(tiles)
    n = n_tiles * LANES
    lane = lax.broadcasted_iota(I32, tiles[0].shape, 1)
    k = 2
    while k <= n:
        j = k // 2
        while j >= 1:
            new_tiles = []
            for t in range(n_tiles):
                x = tiles[t]
                if j < LANES:
                    low = (lane & j) == 0
                    partner = jnp.where(low, pltpu.roll(x, LANES - j, 1), pltpu.roll(x, j, 1))
                else:
                    partner = tiles[t ^ (j // LANES)]
                big = jnp.maximum(x, partner)
                small = jnp.minimum(x, partner)
                up = ((t * LANES) & k) == 0
                if j >= LANES:
                    take_big = (((t * LANES) & j) == 0) == up
                    new_tiles.append(big if take_big else small)
                elif k >= LANES:
                    new_tiles.append(jnp.where(low, big, small) if up else jnp.where(low, small, big))
                else:
                    agree = (lane & j) * (k // j) == (lane & k)
                    new_tiles.append(jnp.where(agree, big, small))
            tiles = new_tiles
            j //= 2
        k *= 2
    return tiles


def _topk_kernel(aff_ref, rank_ref, *, cap):
    a = aff_ref[...]
    n_tok = a.shape[1]
    ordered = _bitonic_sort_descending([a[:, t * LANES:(t + 1) * LANES] for t in range(n_tok // LANES)])
    pos = cap - 1
    thr = ordered[pos // LANES][:, pos % LANES:pos % LANES + 1]
    gt = a > thr
    eq = a == thr
    need = cap - jnp.sum(jnp.where(gt, 1.0, 0.0), axis=1, keepdims=True)
    ri = lax.broadcasted_iota(I32, (PREFIX_BLOCK, PREFIX_BLOCK), 0)
    ci = lax.broadcasted_iota(I32, (PREFIX_BLOCK, PREFIX_BLOCK), 1)
    upper = jnp.where(ri < ci, 1.0, 0.0).astype(BF16)
    eq_before = _exclusive_prefix(jnp.where(eq, 1.0, 0.0).astype(BF16), upper)
    sel = gt | (eq & (eq_before < need))
    slot = _exclusive_prefix(jnp.where(sel, 1.0, 0.0).astype(BF16), upper)
    rank_ref[...] = jnp.where(sel, slot.astype(I32), -1)


def _topk(aff, n_tok, cap, blk0, n_sets):
    return pl.pallas_call(
        functools.partial(_topk_kernel, cap=cap),
        grid=(n_sets,),
        in_specs=[pl.BlockSpec((N_EXPERTS, n_tok), lambda s: (0, blk0 + s))],
        out_specs=pl.BlockSpec((N_EXPERTS, n_tok), lambda s: (0, s)),
        out_shape=jax.ShapeDtypeStruct((N_EXPERTS, n_sets * n_tok), I32),
        compiler_params=_params(("parallel",)),
        name=f"topk_{n_tok}",
    )(aff)


GATHER_EXPERTS = 4


def _gather_kernel(rank_ref, aff_ref, h_ref, *rest, cap):
    xg_ref, val_ref = rest[-2:]
    for e in range(rank_ref.shape[0]):
        r = rank_ref[e]
        n_tok = r.shape[1]
        slot = lax.broadcasted_iota(I32, (cap, n_tok), 0)
        hit = r == slot
        onehot = jnp.where(hit, 1.0, 0.0).astype(BF16)
        xg_ref[e] = _dot(onehot, h_ref[...]).astype(xg_ref.dtype)
        val_ref[e] = jnp.sum(jnp.where(hit, aff_ref[e], 0.0), axis=1, keepdims=True)


def _gather(rank3, aff3, h, n_tok, cap, blk0, row0, rows_total, prev, experts_per_step):
    d = h.shape[1]
    rb0 = row0 // cap
    eg = experts_per_step
    in_specs = [pl.BlockSpec((eg, 1, n_tok), lambda b, e: (e, 0, b)),
                pl.BlockSpec((eg, 1, n_tok), lambda b, e: (e, 0, blk0 + b)),
                pl.BlockSpec((n_tok, d), lambda b, e: (blk0 + b, 0))]
    args = [rank3, aff3, h]
    aliases = {}
    if prev is not None:
        in_specs += [pl.BlockSpec(memory_space=pl.ANY), pl.BlockSpec(memory_space=pl.ANY)]
        args += list(prev)
        aliases = {3: 0, 4: 1}
    return pl.pallas_call(
        functools.partial(_gather_kernel, cap=cap),
        grid=(BATCH, N_EXPERTS // eg),
        in_specs=in_specs,
        out_specs=[pl.BlockSpec((eg, cap, d), lambda b, e: (e, rb0 + b, 0)),
                   pl.BlockSpec((eg, cap, 1), lambda b, e: (e, rb0 + b, 0))],
        out_shape=[jax.ShapeDtypeStruct((N_EXPERTS, rows_total, d), BF16),
                   jax.ShapeDtypeStruct((N_EXPERTS, rows_total, 1), F32)],
        input_output_aliases=aliases,
        compiler_params=_params(("parallel", "arbitrary")),
        name=f"moe_gather_{n_tok}",
    )(*args)


EXPERT_COLS = 512


def _expert_kernel(x_ref, w1_ref, w3_ref, w2_ref, val_ref, y_ref, acc_ref):
    f = pl.program_id(1)
    last = pl.num_programs(1) - 1
    col_slices = [slice(c0, c0 + EXPERT_COLS) for c0 in range(0, acc_ref.shape[1], EXPERT_COLS)]

    def hidden():
        x = x_ref[...]
        a = _dot(x, w1_ref[...].astype(BF16))
        u = _dot(x, w3_ref[...].astype(BF16))
        return (a * jax.nn.sigmoid(a) * u).astype(BF16)

    def down(hm, cols):
        return _dot(hm, w2_ref[:, cols].astype(BF16))

    @pl.when(f == 0)
    def _():
        hm = hidden()
        for cols in col_slices:
            acc_ref[:, cols] = down(hm, cols)

    @pl.when((f > 0) & (f < last))
    def _():
        hm = hidden()
        for cols in col_slices:
            acc_ref[:, cols] += down(hm, cols)

    @pl.when(f == last)
    def _():
        hm = hidden()
        val = val_ref[...]
        for cols in col_slices:
            y_ref[:, cols] = ((acc_ref[:, cols] + down(hm, cols)) * val).astype(y_ref.dtype)


def _experts(xg, vals, w_e1, w_e3, w_e2, layer, tf=256):
    n_exp, rows, d = xg.shape
    ff = w_e1.shape[3]
    assert ff // tf >= 2
    return pl.pallas_call(
        _expert_kernel,
        grid=(n_exp, ff // tf),
        in_specs=[pl.BlockSpec((None, rows, d), lambda e, f: (e, 0, 0)),
                  pl.BlockSpec((None, None, d, tf), lambda e, f: (layer, e, 0, f)),
                  pl.BlockSpec((None, None, d, tf), lambda e, f: (layer, e, 0, f)),
                  pl.BlockSpec((None, None, tf, d), lambda e, f: (layer, e, f, 0)),
                  pl.BlockSpec((None, rows, 1), lambda e, f: (e, 0, 0))],
        out_specs=pl.BlockSpec((None, rows, d), lambda e, f: (e, 0, 0)),
        out_shape=jax.ShapeDtypeStruct((n_exp, rows, d), BF16),
        scratch_shapes=[pltpu.VMEM((rows, d), F32)],
        compiler_params=_params(("parallel", "arbitrary")),
        name="moe_experts",
    )(xg, w_e1, w_e3, w_e2, vals)


COMBINE_LANES = 1024


def _combine_kernel(rank_ref, y_ref, x_ref, g_ref, scale_ref, shift_ref, *rest, cap, keep_x):
    lhs_ref = rest[-1]
    outs = rest[-3:-1] if keep_x else rest[-2:-1]
    n_slots = N_EXPERTS * cap
    chunk = min(COMBINE_LANES, n_slots)
    shift = cap.bit_length() - 1

    r = rank_ref[...].astype(F32).astype(BF16)
    tm = r.shape[0]
    for c0 in range(0, n_slots, chunk):
        lane = c0 + lax.broadcasted_iota(I32, (LANES, chunk), 1)
        row = lax.broadcasted_iota(I32, (LANES, chunk), 0)
        expand = jnp.where(lax.shift_right_logical(lane, shift) == row, 1.0, 0.0).astype(BF16)
        r_exp = _dot(r, expand)
        slot = (c0 + lax.broadcasted_iota(I32, (tm, chunk), 1)) & (cap - 1)
        lhs_ref[:, c0:c0 + chunk] = jnp.where(r_exp == slot.astype(F32), 1.0, 0.0).astype(BF16)

    y = y_ref[...]
    x_new = x_ref[...] + g_ref[...] * _dot(lhs_ref[...], y.reshape(n_slots, y.shape[2]))
    if keep_x:
        outs[0][...] = x_new
    outs[-1][...] = (_rms(x_new) * scale_ref[...] + shift_ref[...]).astype(outs[-1].dtype)


def _combine(rank_t, y, x, gate_tab, scale_tab, shift_tab, prev, n_tok, cap, blk0, row0, tm, keep_x, norm_dtype):
    assert cap & (cap - 1) == 0
    d = x.shape[1]
    tpb = n_tok // tm
    xrow0 = blk0 * tpb
    rb0 = row0 // cap

    def grp(b):
        return b if n_tok == SEQ else BATCH

    def tab_spec():
        return pl.BlockSpec((None, 1, d), lambda b, i: (grp(b), 0, 0))

    def row_spec():
        return pl.BlockSpec((tm, d), lambda b, i: (xrow0 + b * tpb + i, 0))

    in_specs = [pl.BlockSpec((tm, LANES), lambda b, i: (b * tpb + i, 0)),
                pl.BlockSpec((N_EXPERTS, cap, d), lambda b, i: (0, rb0 + b, 0), pipeline_mode=pl.Buffered(1)),
                row_spec(), tab_spec(), tab_spec(), tab_spec()]
    args = [rank_t, y, x, gate_tab, scale_tab, shift_tab]
    out_shape = [jax.ShapeDtypeStruct(x.shape, norm_dtype)]
    if keep_x:
        out_shape = [jax.ShapeDtypeStruct(x.shape, F32)] + out_shape
    aliases = {}
    if prev is not None:
        in_specs += [pl.BlockSpec(memory_space=pl.ANY)] * len(prev)
        aliases = {len(args) + k: k for k in range(len(prev))}
        args += list(prev)
    return pl.pallas_call(
        functools.partial(_combine_kernel, cap=cap, keep_x=keep_x),
        grid=(BATCH, tpb),
        in_specs=in_specs,
        out_specs=[row_spec() for _ in out_shape],
        out_shape=out_shape,
        scratch_shapes=[pltpu.VMEM((tm, N_EXPERTS * cap), BF16)],
        input_output_aliases=aliases,
        compiler_params=_params(("parallel", "arbitrary")),
        name=f"moe_combine_{n_tok}",
    )(*args)


def _expert_choice(h2, aff, x, gate_tab, next_scale, next_shift, w_e1, w_e3, w_e2, layer, need_ctx):
    m_rows = h2.shape[0]
    aff3 = aff.reshape(N_EXPERTS, 1, m_rows)
    sets = [(SEQ, 0)]
    if need_ctx:
        sets.append((CTX_LEN, M_LAT // CTX_LEN))
    caps = [CAPACITY_FACTOR * n_tok // N_EXPERTS for n_tok, _ in sets]
    rows_total = sum(BATCH * cap for cap in caps)
    ranks, gathered, row0 = [], None, 0
    for (n_tok, blk0), cap in zip(sets, caps):
        rank = _topk(aff, n_tok, cap, blk0, BATCH)
        gathered = _gather(rank.reshape(N_EXPERTS, 1, BATCH * n_tok), aff3, h2, n_tok, cap, blk0, row0, rows_total,
                           gathered, experts_per_step=GATHER_EXPERTS if n_tok == SEQ else N_EXPERTS)
        ranks.append((rank, row0))
        row0 += BATCH * cap
    y = _experts(gathered[0], gathered[1], w_e1, w_e3, w_e2, layer)
    out = None
    for (n_tok, blk0), cap, (rank, row0) in zip(sets, caps, ranks):
        rank_t = jnp.pad(rank.T, ((0, 0), (0, LANES - N_EXPERTS)))
        out = _combine(rank_t, y, x, gate_tab, next_scale, next_shift, out, n_tok, cap, blk0, row0,
                       tm=min(512, n_tok), keep_x=need_ctx, norm_dtype=BF16 if need_ctx else F32)
    return out


def _rope_tables(dim):
    rows = SEQ // GRID_W
    quarter = dim // 4
    inv = ROPE_BASE ** (-jnp.arange(quarter, dtype=F32) / quarter)
    row = jnp.broadcast_to(jnp.arange(rows, dtype=F32)[:, None], (rows, GRID_W)).reshape(-1)
    col = jnp.broadcast_to(jnp.arange(GRID_W, dtype=F32)[None, :], (rows, GRID_W)).reshape(-1)
    ang_r = row[:, None] * inv[None, :]
    ang_c = col[:, None] * inv[None, :]
    cos = jnp.concatenate([jnp.cos(ang_r), jnp.cos(ang_r), jnp.cos(ang_c), jnp.cos(ang_c)], axis=-1)
    sin = jnp.concatenate([-jnp.sin(ang_r), jnp.sin(ang_r), -jnp.sin(ang_c), jnp.sin(ang_c)], axis=-1)
    cos = jnp.tile(cos, (BATCH, LANES // dim))
    sin = jnp.tile(sin, (BATCH, LANES // dim))
    cos = jnp.concatenate([cos, jnp.ones((M_CTX, LANES), F32)], axis=0)
    sin = jnp.concatenate([sin, jnp.zeros((M_CTX, LANES), F32)], axis=0)
    return cos, sin


def _halves_major(a, axis=-1):
    axis = axis % a.ndim
    quarter = a.shape[axis] // 4
    a4 = a.reshape(*a.shape[:axis], 2, 2, quarter, *a.shape[axis + 1:])
    return jnp.swapaxes(a4, axis, axis + 1).reshape(a.shape)


def _mod_tables(mods):
    d = D_MODEL
    return [mods[:N_GROUPS, k * d:(k + 1) * d].reshape(N_GROUPS, 1, d) for k in range(6)]


def _layer(x_lat, x_ctx, h_pre, mods, next_scale, next_shift, layer, wt_all, ml_gate_bias, ml_norm, la_q_norm,
           la_kv_norm, la_w_uq, la_w_ukv, ga_q_norm, ga_k_norm, w_br_ml, w_br_la, w_br_ga, w_out, w_router,
           w_e1, w_e3, w_e2, ropes, need_ctx):
    d = D_MODEL
    m_rows = M_ALL if need_ctx else M_LAT
    sh1, sc1, g1, sh2, sc2, g2 = _mod_tables(mods)
    (cos_la, sin_la), (cos_ga, sin_ga) = ropes

    n_qk = GA_HEADS + GA_KV_HEADS
    wt_rows = wt_all.reshape(-1, d)
    layer_row0 = layer * wt_all.shape[1]
    wt_sm, wt_gaf = lax.optimization_barrier((wt_rows[layer_row0 + OFF_MLG:layer_row0 + OFF_GAQ],
                                              wt_rows[layer_row0 + OFF_GAQ:layer_row0 + OFF_GATE]))
    wt_small = jnp.concatenate([wt_sm[OFF_CQ - OFF_MLG:], wt_sm[:OFF_CQ - OFF_MLG],
                                jnp.zeros((SMALL_WIDTH - (OFF_GAQ - OFF_MLG), d), F32)], axis=0).astype(BF16)
    wt_gaqk = _halves_major(wt_gaf[:OFF_GAV - OFF_GAQ].reshape(n_qk, GA_HEAD_DIM, d), axis=1).reshape(-1, d)
    wt_ga = jnp.concatenate([wt_gaqk, wt_gaf[OFF_GAV - OFF_GAQ:]], axis=0).astype(BF16)
    w_uq = la_w_uq.reshape(LA_Q_RANK, LA_HEADS, LA_NOPE + LA_ROPE)
    w_uq = jnp.concatenate([w_uq[:, :, :LA_NOPE].reshape(LA_Q_RANK, LA_HEADS * LA_NOPE),
                            w_uq[:, :, LA_NOPE:].reshape(LA_Q_RANK, LA_HEADS * LA_ROPE)], axis=1).astype(BF16)

    h = _rms_mod(x_lat, x_ctx, 1.0 + sc1, sh1, M_ALL) if h_pre is None else h_pre
    p_ml = _mm(h, wt_rows, layer_row0, OFF_MLG, name="proj_ml")
    q_la, k_la, v_la, tail = _la_prep(h, wt_small, la_q_norm.reshape(1, LA_Q_RANK),
                                      la_kv_norm.reshape(1, LA_KV_RANK), w_uq, la_w_ukv.astype(BF16), cos_la, sin_la)
    q_ga, k_ga, v_ga = _ga_prep(h, wt_ga, _halves_major(ga_q_norm.reshape(1, GA_HEAD_DIM)),
                                _halves_major(ga_k_norm.reshape(1, GA_HEAD_DIM)), cos_ga, sin_ga)

    gates = (tail[:, TAIL_MLG:TAIL_MLG + ML_GATES] + ml_gate_bias[None, :]).reshape(M_ALL, 2, 2, ML_HEADS)
    gates = jnp.stack([gates[:, 0, 0], jax.nn.log_sigmoid(gates[:, 0, 1]),
                       gates[:, 1, 0], jax.nn.log_sigmoid(gates[:, 1, 1])], axis=-1)
    n_lat = SEQ // ML_CHUNK
    g_lat = gates[:M_LAT].reshape(BATCH, n_lat, ML_CHUNK, ML_HEADS, 4)
    g_ctx = gates[M_LAT:].reshape(BATCH, CTX_LEN, ML_HEADS, 4)
    y_ml_lat, y_ml_ctx = _mlstm(p_ml, jnp.transpose(g_lat, (0, 3, 1, 4, 2)), jnp.transpose(g_lat, (0, 3, 1, 2, 4)),
                                jnp.transpose(g_ctx, (0, 2, 3, 1)), jnp.transpose(g_ctx, (0, 2, 1, 3)),
                                ml_norm.reshape(1, ML_WIDTH))

    y_la = _attention(q_la, k_la, v_la, 1, need_ctx, "attn_mla")
    y_ga = _attention(q_ga, k_ga, v_ga, GA_HEADS // GA_KV_HEADS, need_ctx, "attn_gqa")

    merged = _merge(h, y_ml_lat, y_ml_ctx, y_la, y_ga, wt_rows, layer, layer_row0 + OFF_GATE, w_br_ml, w_br_la,
                    w_br_ga, m_rows)
    x, h2, aff = _out_proj(merged, w_out.astype(BF16), w_router.T.astype(BF16), x_lat, x_ctx if need_ctx else None,
                           g1, 1.0 + sc2, sh2, m_rows)
    return _expert_choice(h2, aff, x, g2, next_scale, next_shift, w_e1, w_e3, w_e2, layer, need_ctx)


def kernel(x, c, ctx, c_ctx, w_mod, b_mod, w_in, ml_gate_bias, ml_norm, la_q_norm, la_kv_norm, la_w_uq, la_w_ukv,
           ga_q_norm, ga_k_norm, w_br_ml, w_br_la, w_br_ga, w_out, w_router, w_e1, w_e3, w_e2, final_norm):
    d = D_MODEL
    cc = jnp.concatenate([c, c_ctx[None, :], jnp.zeros((MODS_ROWS - N_GROUPS, d), F32)], axis=0)
    mods = _mods(cc, w_mod, b_mod.reshape(DEPTH, 1, 6 * d))
    ropes = (_rope_tables(LA_ROPE), tuple(_halves_major(t) for t in _rope_tables(GA_HEAD_DIM)))
    xs, xs_ctx, h_pre = x.reshape(M_LAT, d), ctx.reshape(M_CTX, d), None
    wt_in = jnp.swapaxes(w_in, 1, 2)
    for l in range(DEPTH):
        need_ctx = l < DEPTH - 1
        if need_ctx:
            sh1_next, sc1_next = _mod_tables(mods[l + 1])[:2]
            next_scale, next_shift = 1.0 + sc1_next, sh1_next
        else:
            next_scale = jnp.broadcast_to(final_norm.reshape(1, 1, d), (N_GROUPS, 1, d))
            next_shift = jnp.zeros((N_GROUPS, 1, d), F32)
        out = _layer(xs, xs_ctx, h_pre, mods[l], next_scale, next_shift, l, wt_in, ml_gate_bias[l], ml_norm[l],
                     la_q_norm[l], la_kv_norm[l], la_w_uq[l], la_w_ukv[l], ga_q_norm[l], ga_k_norm[l],
                     w_br_ml, w_br_la, w_br_ga, w_out[l], w_router[l], w_e1, w_e3, w_e2, ropes, need_ctx)
        if need_ctx:
            xs, h_pre = out
            xs_ctx = None
    return out[0].reshape(BATCH, SEQ, d)
```
